```python
import jax
import jax.numpy as jnp
from jax import lax
import numpy as np

D_MODEL = 1024
BATCH = 2
SEQ = 16384
DEPTH = 1
DEC_BATCH = 128
DEC_SEQ = 8
PAST_LEN = 8192
PAGE_SIZE = 128

HEAD_DIM = 64
SB_HEADS = 8
SB_WIDTH = SB_HEADS * HEAD_DIM
SA_HEADS = 8
SA_KV_HEADS = 4
SA_GROUP = SA_HEADS // SA_KV_HEADS
SA_WIDTH = SA_HEADS * HEAD_DIM
SA_KV_WIDTH = SA_KV_HEADS * HEAD_DIM
IDX_HEADS = 8
IDX_DIM = 64
TOP_K_MAX = 256
ROT_DIM = HEAD_DIM // 4
IDX_ROT_DIM = IDX_DIM // 4
ROPE_THETA = 500000.0
D_FF = 4 * D_MODEL
Q_BLOCK = 128
RMS_EPS = 1e-6
IN_SPLITS = (SB_WIDTH, SB_WIDTH, SB_WIDTH, SA_WIDTH, SA_KV_WIDTH, SA_KV_WIDTH,
             IDX_HEADS * IDX_DIM, IDX_DIM, IDX_HEADS, 2 * D_MODEL)
IN_WIDTH = sum(IN_SPLITS)

kernel_name = 'stickbreak_dsa_gated_hybrid_step'


def _rms(x, g):
    xf = x.astype(jnp.float32)
    y = xf * lax.rsqrt(jnp.mean(xf * xf, axis=-1, keepdims=True) + RMS_EPS)
    return (y * g.astype(jnp.float32)).astype(x.dtype)


def _rope(x, pos, rot_dim):
    half = rot_dim // 2
    inv = 1.0 / (ROPE_THETA ** (jnp.arange(0, rot_dim, 2, dtype=jnp.float32) / rot_dim))
    ang = pos[:, None] * inv[None, :]
    cos = jnp.cos(ang)[:, None, :]
    sin = jnp.sin(ang)[:, None, :]
    xf = x.astype(jnp.float32)
    x1 = xf[..., :half]
    x2 = xf[..., half:rot_dim]
    out = jnp.concatenate([x1 * cos - x2 * sin, x2 * cos + x1 * sin, xf[..., rot_dim:]], axis=-1)
    return out.astype(x.dtype)


def _project(h, pos, w_in, b_gate, q_norm_g, k_norm_g, idx_k_norm_g):
    B, T, _ = h.shape
    p = h @ w_in
    parts, off = [], 0
    for n in IN_SPLITS:
        parts.append(p[..., off:off + n])
        off += n
    sq, sk, sv, aq, ak, av, iq, ik, iw, gates = parts
    sq = sq.reshape(B, T, SB_HEADS, HEAD_DIM)
    sk = sk.reshape(B, T, SB_HEADS, HEAD_DIM)
    sv = sv.reshape(B, T, SB_HEADS, HEAD_DIM)
    aq = _rope(_rms(aq.reshape(B, T, SA_HEADS, HEAD_DIM), q_norm_g), pos, ROT_DIM)
    ak = _rope(_rms(ak.reshape(B, T, SA_KV_HEADS, HEAD_DIM), k_norm_g), pos, ROT_DIM)
    av = av.reshape(B, T, SA_KV_HEADS, HEAD_DIM)
    iq = _rope(iq.reshape(B, T, IDX_HEADS, IDX_DIM), pos, IDX_ROT_DIM)
    ik = _rope(_rms(ik, idx_k_norm_g)[:, :, None, :], pos, IDX_ROT_DIM)[:, :, 0, :]
    g = jax.nn.sigmoid((gates + b_gate).astype(jnp.float32)).astype(h.dtype)
    return sq, sk, sv, aq, ak, av, iq, ik, iw, g[..., :D_MODEL], g[..., D_MODEL:]


def _stick_break(z, mask, log_survive):
    l = jnp.where(mask, jax.nn.log_sigmoid(-z), 0.0)
    r = lax.cumsum(l, axis=z.ndim - 1, reverse=True)
    a = jnp.where(mask, jnp.exp(jax.nn.log_sigmoid(z) + (r - l) + log_survive[..., None]), 0.0)
    return a, log_survive + r[..., 0]


def _sb_prompt(q, k, v):
    B, S, H, D = q.shape
    nb = S // Q_BLOCK
    kf = k.astype(jnp.float32)
    vf = v.astype(jnp.float32)
    qb = jnp.moveaxis(q.reshape(B, nb, Q_BLOCK, H, D), 1, 0)
    starts = jnp.arange(nb, dtype=jnp.int32) * Q_BLOCK
    key_pos = jnp.arange(S, dtype=jnp.int32)

    def block(args):
        qblk, t0 = args
        t = t0 + jnp.arange(Q_BLOCK, dtype=jnp.int32)
        z = jnp.einsum('bqhd,bshd->bhqs', qblk.astype(jnp.float32), kf) * (HEAD_DIM ** -0.5)
        mask = key_pos[None, :] < t[:, None]
        a, _ = _stick_break(z, mask, jnp.zeros(z.shape[:-1], jnp.float32))
        return jnp.einsum('bhqs,bshd->bqhd', a, vf)

    o = lax.map(block, (qb, starts))
    return jnp.moveaxis(o, 0, 1).reshape(B, S, H * D).astype(q.dtype)


def _sb_sample(q, k, v, cache_k, cache_v, page_table):
    B, T, H, D = q.shape
    qf = q.astype(jnp.float32) * (HEAD_DIM ** -0.5)
    z = jnp.einsum('bthd,bshd->bhts', qf, k.astype(jnp.float32))
    tpos = jnp.arange(T, dtype=jnp.int32)
    a, surv = _stick_break(z, tpos[None, :] < tpos[:, None], jnp.zeros((B, H, T), jnp.float32))
    o = jnp.einsum('bhts,bshd->bthd', a, v.astype(jnp.float32))
    page_mask = jnp.ones((T, PAGE_SIZE), dtype=bool)

    def page_step(carry, phys):
        o, surv = carry
        kp = cache_k[phys].astype(jnp.float32)
        vp = cache_v[phys].astype(jnp.float32)
        zp = jnp.einsum('bthd,bshd->bhts', qf, kp)
        ap, surv = _stick_break(zp, page_mask, surv)
        return (o + jnp.einsum('bhts,bshd->bthd', ap, vp), surv), None

    (o, _), _ = lax.scan(page_step, (o, surv), jnp.flip(page_table, axis=1).T)
    return o.reshape(B, T, H * D).astype(q.dtype)


def _index_scores(qi, wi, ki):
    s = jax.nn.relu(jnp.einsum('bqhd,bld->bqhl', qi.astype(jnp.float32), ki.astype(jnp.float32)) * (IDX_DIM ** -0.5))
    return jnp.einsum('bqh,bqhl->bql', wi.astype(jnp.float32) * (IDX_HEADS ** -0.5), s)


def _sparse_attend(q, k_sel, v_sel, valid):
    B, Q = q.shape[:2]
    qg = q.astype(jnp.float32).reshape(B, Q, SA_KV_HEADS, SA_GROUP, HEAD_DIM)
    logits = jnp.einsum('bqkgd,bqskd->bqkgs', qg, k_sel.astype(jnp.float32)) * (HEAD_DIM ** -0.5)
    logits = jnp.where(valid[:, :, None, None, :], logits, -jnp.inf)
    p = jax.nn.softmax(logits, axis=-1)
    o = jnp.einsum('bqkgs,bqskd->bqkgd', p, v_sel.astype(jnp.float32))
    return o.reshape(B, Q, SA_WIDTH)


def _dsa_prompt(aq, ak, av, iq, iw, ik):
    B, S = aq.shape[:2]
    nb = S // Q_BLOCK
    k_top = min(TOP_K_MAX, S // 4)
    key_pos = jnp.arange(S, dtype=jnp.int32)
    bidx = jnp.arange(B)[:, None, None]

    def blocks(a):
        return jnp.moveaxis(a.reshape((B, nb, Q_BLOCK) + a.shape[2:]), 1, 0)

    starts = jnp.arange(nb, dtype=jnp.int32) * Q_BLOCK

    def block(args):
        aqb, iqb, iwb, t0 = args
        t = t0 + jnp.arange(Q_BLOCK, dtype=jnp.int32)
        scores = jnp.where(key_pos[None, None, :] <= t[None, :, None], _index_scores(iqb, iwb, ik), -jnp.inf)
        _, idx = lax.top_k(scores, k_top)
        valid = idx <= t[None, :, None]
        return _sparse_attend(aqb, ak[bidx, idx], av[bidx, idx], valid)

    o = lax.map(block, (blocks(aq), blocks(iq), blocks(iw), starts))
    return jnp.moveaxis(o, 0, 1).reshape(B, S, SA_WIDTH).astype(aq.dtype)


def _dsa_sample(aq, ak, av, iq, iw, ik, cache_k, cache_v, cache_ik, page_table):
    B, T = aq.shape[:2]
    past_len = page_table.shape[1] * PAGE_SIZE
    L = past_len + T
    k_top = min(TOP_K_MAX, L // 4)
    bidx = jnp.arange(B)[:, None, None]
    ik_past = cache_ik[page_table].reshape(B, past_len, IDX_DIM)
    ik_all = jnp.concatenate([ik_past, ik.astype(ik_past.dtype)], axis=1)
    t_abs = past_len + jnp.arange(T, dtype=jnp.int32)
    key_pos = jnp.arange(L, dtype=jnp.int32)
    scores = jnp.where(key_pos[None, None, :] <= t_abs[None, :, None], _index_scores(iq, iw, ik_all), -jnp.inf)
    _, idx = lax.top_k(scores, k_top)
    valid = idx <= t_abs[None, :, None]
    is_new = (idx >= past_len)[..., None, None]
    s_past = jnp.minimum(idx, past_len - 1)
    phys = page_table[bidx, s_past // PAGE_SIZE] * PAGE_SIZE + s_past % PAGE_SIZE
    s_new = jnp.clip(idx - past_len, 0, T - 1)
    flat_k = cache_k.reshape(-1, SA_KV_HEADS, HEAD_DIM)
    flat_v = cache_v.reshape(-1, SA_KV_HEADS, HEAD_DIM)
    k_sel = jnp.where(is_new, ak[bidx, s_new].astype(flat_k.dtype), flat_k[phys])
    v_sel = jnp.where(is_new, av[bidx, s_new].astype(flat_v.dtype), flat_v[phys])
    return _sparse_attend(aq, k_sel, v_sel, valid).astype(aq.dtype)


def _layer_tail(x, o_sb, o_sa, g_sb, g_sa, w_sb_out, w_sa_out, w_o, norm2_g, w_up, w_down):
    merged = g_sb * (o_sb @ w_sb_out) + g_sa * (o_sa @ w_sa_out)
    x = x + merged @ w_o
    u = jax.nn.relu(_rms(x, norm2_g) @ w_up)
    return x + (u * u) @ w_down


def setup_inputs(seed: int = 0) -> dict:
    key = jax.random.key(seed)
    ks = jax.random.split(key, 24)
    n_pages = PAST_LEN // PAGE_SIZE
    used = DEC_BATCH * n_pages
    n_phys = used + (used + 3) // 4
    f32 = jnp.float32

    def nrm(k, shape, scale=1.0):
        return jax.random.normal(k, shape, f32) * scale

    page_table = jax.random.permutation(ks[7], n_phys)[:used].reshape(DEC_BATCH, n_pages).astype(jnp.int32)
    return {
        'x_prompt': nrm(ks[0], (BATCH, SEQ, D_MODEL)),
        'x_sample': nrm(ks[1], (DEC_BATCH, DEC_SEQ, D_MODEL)),
        'cache_sb_k': nrm(ks[2], (DEPTH, n_phys, PAGE_SIZE, SB_HEADS, HEAD_DIM)),
        'cache_sb_v': nrm(ks[3], (DEPTH, n_phys, PAGE_SIZE, SB_HEADS, HEAD_DIM)),
        'cache_sa_k': nrm(ks[4], (DEPTH, n_phys, PAGE_SIZE, SA_KV_HEADS, HEAD_DIM)),
        'cache_sa_v': nrm(ks[5], (DEPTH, n_phys, PAGE_SIZE, SA_KV_HEADS, HEAD_DIM)),
        'cache_idx_k': nrm(ks[6], (DEPTH, n_phys, PAGE_SIZE, IDX_DIM)),
        'page_table': page_table,
        'norm1_g': 1.0 + nrm(ks[8], (DEPTH, D_MODEL), 0.02),
        'w_in': nrm(ks[9], (DEPTH, D_MODEL, IN_WIDTH), D_MODEL ** -0.5),
        'b_gate': nrm(ks[10], (DEPTH, 2 * D_MODEL), 0.1),
        'q_norm_g': 1.0 + nrm(ks[11], (DEPTH, HEAD_DIM), 0.02),
        'k_norm_g': 1.0 + nrm(ks[12], (DEPTH, HEAD_DIM), 0.02),
        'idx_k_norm_g': 1.0 + nrm(ks[13], (DEPTH, IDX_DIM), 0.02),
        'w_sb_out': nrm(ks[14], (DEPTH, SB_WIDTH, D_MODEL), SB_WIDTH ** -0.5),
        'w_sa_out': nrm(ks[15], (DEPTH, SA_WIDTH, D_MODEL), SA_WIDTH ** -0.5),
        'w_o': nrm(ks[16], (DEPTH, D_MODEL, D_MODEL), D_MODEL ** -0.5),
        'norm2_g': 1.0 + nrm(ks[17], (DEPTH, D_MODEL), 0.02),
        'w_up': nrm(ks[18], (DEPTH, D_MODEL, D_FF), D_MODEL ** -0.5),
        'w_down': nrm(ks[19], (DEPTH, D_FF, D_MODEL), D_FF ** -0.5),
    }


def reference(x_prompt, x_sample, cache_sb_k, cache_sb_v, cache_sa_k, cache_sa_v, cache_idx_k, page_table,
              norm1_g, w_in, b_gate, q_norm_g, k_norm_g, idx_k_norm_g, w_sb_out, w_sa_out, w_o,
              norm2_g, w_up, w_down):
    past_len = page_table.shape[1] * PAGE_SIZE
    pos_p = jnp.arange(x_prompt.shape[1], dtype=jnp.float32)
    pos_s = past_len + jnp.arange(x_sample.shape[1], dtype=jnp.float32)
    xp, xs = x_prompt, x_sample
    new_p = [[], [], [], [], []]
    new_s = [[], [], [], [], []]
    for layer in range(DEPTH):
        hp = _rms(xp, norm1_g[layer])
        sq, sk, sv, aq, ak, av, iq, ik, iw, g_sb, g_sa = _project(
            hp, pos_p, w_in[layer], b_gate[layer], q_norm_g[layer], k_norm_g[layer], idx_k_norm_g[layer])
        o_sb = _sb_prompt(sq, sk, sv)
        o_sa = _dsa_prompt(aq, ak, av, iq, iw, ik)
        xp = _layer_tail(xp, o_sb, o_sa, g_sb, g_sa, w_sb_out[layer], w_sa_out[layer], w_o[layer],
                         norm2_g[layer], w_up[layer], w_down[layer])
        for lst, arr in zip(new_p, (sk, sv, ak, av, ik)):
            lst.append(arr)
        hs = _rms(xs, norm1_g[layer])
        sq, sk, sv, aq, ak, av, iq, ik, iw, g_sb, g_sa = _project(
            hs, pos_s, w_in[layer], b_gate[layer], q_norm_g[layer], k_norm_g[layer], idx_k_norm_g[layer])
        o_sb = _sb_sample(sq, sk, sv, cache_sb_k[layer], cache_sb_v[layer], page_table)
        o_sa = _dsa_sample(aq, ak, av, iq, iw, ik, cache_sa_k[layer], cache_sa_v[layer], cache_idx_k[layer], page_table)
        xs = _layer_tail(xs, o_sb, o_sa, g_sb, g_sa, w_sb_out[layer], w_sa_out[layer], w_o[layer],
                         norm2_g[layer], w_up[layer], w_down[layer])
        for lst, arr in zip(new_s, (sk, sv, ak, av, ik)):
            lst.append(arr)
    return (xp, xs,
            jnp.stack(new_p[0]), jnp.stack(new_p[1]), jnp.stack(new_p[2]), jnp.stack(new_p[3]), jnp.stack(new_p[4]),
            jnp.stack(new_s[0]), jnp.stack(new_s[1]), jnp.stack(new_s[2]), jnp.stack(new_s[3]), jnp.stack(new_s[4]))
```

```python
import functools

import jax
import jax.numpy as jnp
from jax import lax
from jax.experimental import pallas as pl
from jax.experimental.pallas import tpu as pltpu

F32 = jnp.float32
BF16 = jnp.bfloat16
I32 = jnp.int32

D_MODEL = 1024
HEAD_DIM = 64
SB_HEADS = 8
SA_HEADS = 8
SA_KV_HEADS = 4
IDX_HEADS = 8
IDX_DIM = 64
SB_WIDTH = SB_HEADS * HEAD_DIM
SA_WIDTH = SA_HEADS * HEAD_DIM
SA_KV_WIDTH = SA_KV_HEADS * HEAD_DIM
IDX_WIDTH = IDX_HEADS * IDX_DIM
TOP_K_MAX = 256
ROT_DIM = HEAD_DIM // 4
ROPE_THETA = 500000.0
D_FF = 4 * D_MODEL
PAGE_SIZE = 128
RMS_EPS = 1e-6
QK_SCALE = HEAD_DIM ** -0.5
IDX_SCALE = IDX_DIM ** -0.5
IDX_W_SCALE = IDX_HEADS ** -0.5

LANES = 128
VMEM_LIMIT = 56 * 1024 * 1024

C_SQ, C_SK, C_SV = 0, 512, 1024
C_AQ, C_AK, C_AV = 1536, 2048, 2304
C_IQ, C_IK2, C_IW = 2560, 3072, 3200
C_GATE = 3328
C_END = C_GATE + 2 * D_MODEL
SA_Q_PERM = (0, 2, 1, 3, 4, 6, 5, 7)

SURV_FLOOR = -105.0
NEG_BIG = -1e30
INT_MIN = -2 ** 31


def _cparams(sem, vmem=VMEM_LIMIT):
    return pltpu.CompilerParams(dimension_semantics=sem, vmem_limit_bytes=vmem)


def _rope_chunk(y, cos, sin_a, sin_b):
    up = pltpu.roll(y, LANES - ROT_DIM // 2, 1)
    dn = pltpu.roll(y, ROT_DIM // 2, 1)
    return y * cos + up * sin_a + dn * sin_b


def _head_rms_chunk(y, e_mat, gain):
    y2 = y * y
    hi = y2.astype(BF16)
    lo = (y2 - hi.astype(F32)).astype(BF16)
    ms = (jnp.dot(hi, e_mat, preferred_element_type=F32)
          + jnp.dot(lo, e_mat, preferred_element_type=F32))
    return y * lax.rsqrt(ms + RMS_EPS) * gain


def _proj_kernel(x_ref, g1_ref, w_ref, qg_ref, kg_ref, ig_ref, bg_ref, cos_ref, sa_ref, sb_ref,
                 e_ref,
                 skf_ref, svf_ref, akf_ref, avf_ref, ikf_ref,
                 sbq_ref, sbk_ref, sbv_ref, saq_ref, sak_ref, sav_ref, iq_ref, ik2_ref, iw_ref,
                 gate_ref):
    x = x_ref[...]
    ms = jnp.mean(x * x, axis=-1, keepdims=True)
    h = (x * lax.rsqrt(ms + RMS_EPS) * g1_ref[...]).astype(BF16)

    def mm(lo, hi):
        return jnp.dot(h, w_ref[:, lo:hi], preferred_element_type=F32)

    cos, sin_a, sin_b = cos_ref[...], sa_ref[...], sb_ref[...]
    e_mat = e_ref[...]

    p = mm(C_SQ, C_SK)
    sbq_ref[...] = (p * QK_SCALE).astype(BF16)
    p = mm(C_SK, C_SV)
    skf_ref[...] = p
    sbk_ref[...] = p.astype(BF16)
    p = mm(C_SV, C_AQ)
    svf_ref[...] = p
    sbv_ref[...] = p.astype(BF16)

    p = mm(C_AQ, C_AK)
    for c in range(SA_WIDTH // LANES):
        sl = slice(c * LANES, (c + 1) * LANES)
        y = _rope_chunk(_head_rms_chunk(p[:, sl], e_mat, qg_ref[...]), cos, sin_a, sin_b)
        saq_ref[:, sl] = (y * QK_SCALE).astype(BF16)
    p = mm(C_AK, C_AV)
    for c in range(SA_KV_WIDTH // LANES):
        sl = slice(c * LANES, (c + 1) * LANES)
        y = _rope_chunk(_head_rms_chunk(p[:, sl], e_mat, kg_ref[...]), cos, sin_a, sin_b)
        akf_ref[:, sl] = y
        sak_ref[:, sl] = y.astype(BF16)
    p = mm(C_AV, C_IQ)
    avf_ref[...] = p
    sav_ref[...] = p.astype(BF16)

    p = mm(C_IQ, C_IK2)
    for c in range(IDX_WIDTH // LANES):
        sl = slice(c * LANES, (c + 1) * LANES)
        iq_ref[:, sl] = (_rope_chunk(p[:, sl], cos, sin_a, sin_b) * IDX_SCALE).astype(BF16)
    p = mm(C_IK2, C_GATE)
    y = _rope_chunk(_head_rms_chunk(p[:, :LANES], e_mat, ig_ref[...]), cos, sin_a, sin_b)
    ikf_ref[...] = y[:, :IDX_DIM]
    ik2_ref[...] = y.astype(BF16)
    iw_ref[...] = p[:, LANES:LANES + IDX_HEADS] * IDX_W_SCALE

    p = mm(C_GATE, C_END) + bg_ref[...]
    gate_ref[...] = jax.nn.sigmoid(p)


def _rope_tables(pos):
    half = ROT_DIM // 2
    inv = 1.0 / (ROPE_THETA ** (jnp.arange(0, ROT_DIM, 2, dtype=F32) / ROT_DIM))
    ang = pos[:, None] * inv[None, :]
    cos, sin = jnp.cos(ang), jnp.sin(ang)
    t = pos.shape[0]
    zeros = jnp.zeros((t, HEAD_DIM - ROT_DIM), F32)
    zh = jnp.zeros((t, half), F32)
    cos_h = jnp.concatenate([cos, cos, zeros + 1.0], axis=1)
    sa_h = jnp.concatenate([-sin, zh, zeros], axis=1)
    sb_h = jnp.concatenate([zh, sin, zeros], axis=1)
    return tuple(jnp.concatenate([a, a], axis=1) for a in (cos_h, sa_h, sb_h))


def _pack_w_in(w_in):
    o = 0
    parts = []
    for n in (SB_WIDTH, SB_WIDTH, SB_WIDTH, SA_WIDTH, SA_KV_WIDTH, SA_KV_WIDTH, IDX_WIDTH,
              IDX_DIM, IDX_HEADS, 2 * D_MODEL):
        parts.append(w_in[:, o:o + n])
        o += n
    sq, sk, sv, aq, ak, av, iq, ik, iw, gates = parts
    aq = aq.reshape(D_MODEL, SA_HEADS, HEAD_DIM)[:, jnp.array(SA_Q_PERM)].reshape(D_MODEL, SA_WIDTH)
    pad = jnp.zeros((D_MODEL, C_GATE - C_IW - IDX_HEADS), w_in.dtype)
    return jnp.concatenate([sq, sk, sv, aq, ak, av, iq, ik, ik, iw, pad, gates], axis=1).astype(BF16)


def _project(x2d, pos, n_rep, w_packed, g1, qg, kg, ig, bg, tm):
    r = x2d.shape[0]
    t = pos.shape[0]
    assert r == n_rep * t and r % tm == 0 and (t % tm == 0 or tm % t == 0)
    cos, sin_a, sin_b = _rope_tables(pos)
    if tm > t:
        cos, sin_a, sin_b = (jnp.tile(a, (tm // t, 1)) for a in (cos, sin_a, sin_b))
    nt = cos.shape[0] // tm
    e_mat = (jnp.arange(LANES)[:, None] // HEAD_DIM == jnp.arange(LANES)[None, :] // HEAD_DIM)
    e_mat = (e_mat.astype(F32) / HEAD_DIM).astype(BF16)

    def row(width):
        return pl.BlockSpec((tm, width), lambda i: (i, 0))

    def const(shape):
        return pl.BlockSpec(shape, lambda i: (0, 0))

    tab = pl.BlockSpec((tm, LANES), lambda i: (i % nt, 0))
    tile2 = lambda g: jnp.concatenate([g, g]).reshape(1, LANES)
    out_widths = [(SB_WIDTH, F32), (SB_WIDTH, F32), (SA_KV_WIDTH, F32), (SA_KV_WIDTH, F32),
                  (IDX_DIM, F32),
                  (SB_WIDTH, BF16), (SB_WIDTH, BF16), (SB_WIDTH, BF16),
                  (SA_WIDTH, BF16), (SA_KV_WIDTH, BF16), (SA_KV_WIDTH, BF16),
                  (IDX_WIDTH, BF16), (LANES, BF16), (IDX_HEADS, F32), (2 * D_MODEL, F32)]
    return pl.pallas_call(
        _proj_kernel,
        grid=(r // tm,),
        in_specs=[row(D_MODEL), const((1, D_MODEL)),
                  pl.BlockSpec((D_MODEL, C_END), lambda i: (0, 0), pipeline_mode=pl.Buffered(1)),
                  const((1, LANES)), const((1, LANES)), const((1, LANES)), const((1, 2 * D_MODEL)),
                  tab, tab, tab, const((LANES, LANES))],
        out_specs=[row(w) for w, _ in out_widths],
        out_shape=[jax.ShapeDtypeStruct((r, w), dt) for w, dt in out_widths],
        compiler_params=_cparams(("parallel",)),
        name="proj",
    )(x2d, g1.reshape(1, D_MODEL), w_packed, tile2(qg), tile2(kg), tile2(ig),
      bg.reshape(1, 2 * D_MODEL), cos, sin_a, sin_b, e_mat)


def _stick_break_tile(z, mask, surv, tri):
    t = jnp.log1p(jnp.exp(-jnp.abs(z)))
    l = -(jnp.maximum(z, 0.0) + t)
    if mask is not None:
        l = jnp.where(mask, l, 0.0)
    ls = jnp.minimum(z, 0.0) - t
    hi = l.astype(BF16)
    lo = (l - hi.astype(F32)).astype(BF16)
    ex = jnp.dot(hi, tri, preferred_element_type=F32) + jnp.dot(lo, tri, preferred_element_type=F32)
    a = jnp.exp(ls + ex + surv)
    if mask is not None:
        a = jnp.where(mask, a, 0.0)
    return a, surv + jnp.sum(l, axis=1, keepdims=True)


def _newer_tri(n):
    j = lax.broadcasted_iota(I32, (n, n), 0)
    s = lax.broadcasted_iota(I32, (n, n), 1)
    return (j > s).astype(BF16)


def _sb_prompt_kernel(q_ref, k_ref, v_ref, o_ref, acc_ref, surv_ref, *, tq):
    i = pl.program_id(2)
    t0 = i * tq
    lane = lax.broadcasted_iota(I32, (tq, LANES), 1)
    row_t = t0 + lax.broadcasted_iota(I32, (tq, tq), 0)
    col0 = lax.broadcasted_iota(I32, (tq, tq), 1)
    tri = _newer_tri(tq)
    q = q_ref[...]
    outs = []
    for half in range(2):
        qm = jnp.where((lane >= HEAD_DIM) == bool(half), q, jnp.zeros_like(q))
        acc_ref[...] = jnp.zeros_like(acc_ref)
        surv_ref[...] = jnp.zeros_like(surv_ref)

        def body(carry):
            kt, _ = carry
            start = pl.multiple_of(kt * tq, tq)
            kb = k_ref[pl.ds(start, tq), :]
            vb = v_ref[pl.ds(start, tq), :]
            z = lax.dot_general(qm, kb, (((1,), (1,)), ((), ())), preferred_element_type=F32)
            mask = (start + col0) < row_t
            a, surv = _stick_break_tile(z, mask, surv_ref[...], tri)
            acc_ref[...] += jnp.dot(a.astype(BF16), vb, preferred_element_type=F32)
            surv_ref[...] = surv
            return kt - 1, jnp.max(surv)

        lax.while_loop(lambda c: (c[0] >= 0) & (c[1] > SURV_FLOOR), body, (i, jnp.float32(0.0)))
        outs.append(acc_ref[...])
    o_ref[...] = jnp.where(lane < HEAD_DIM, outs[0], outs[1]).astype(o_ref.dtype)


def _sb_prompt(q, k, v, batch, seq, tq):
    nq = seq // tq
    n_chunks = SB_WIDTH // LANES
    qspec = pl.BlockSpec((tq, LANES), lambda b, c, i: (b * nq + i, c))
    kvspec = pl.BlockSpec((seq, LANES), lambda b, c, i: (b, c))
    return pl.pallas_call(
        functools.partial(_sb_prompt_kernel, tq=tq),
        grid=(batch, n_chunks, nq),
        in_specs=[qspec, kvspec, kvspec],
        out_specs=qspec,
        out_shape=jax.ShapeDtypeStruct((batch * seq, SB_WIDTH), BF16),
        scratch_shapes=[pltpu.VMEM((tq, LANES), F32), pltpu.VMEM((tq, 1), F32)],
        compiler_params=_cparams(("parallel", "parallel", "arbitrary")),
        name="sb_prompt",
    )(q, k, v)


def _sortable(x):
    bits = lax.bitcast_convert_type(x, I32)
    key = jnp.where(bits < 0, bits ^ jnp.int32(0x7FFFFFFF), bits)
    return jnp.where(x == 0.0, jnp.int32(0), key)


def _select_topk(keys_ref, n_tiles, tk, k_row, n_index_bits):
    qn = keys_ref.shape[0]
    sign = jnp.int32(INT_MIN)

    def count(pred):
        def tile_body(kt, acc):
            start = pl.multiple_of(kt * tk, tk)
            m = pred(keys_ref[:, pl.ds(start, tk)], start).astype(I32)
            for c in range(tk // LANES):
                acc = acc + m[:, c * LANES:(c + 1) * LANES]
            return acc
        acc = lax.fori_loop(0, n_tiles, tile_body, jnp.zeros((qn, LANES), I32))
        return jnp.sum(acc.astype(F32), axis=1, keepdims=True).astype(I32)

    def bit_body(carry):
        b, prefix, thr, done, _ = carry
        cand_u = prefix | lax.shift_left(jnp.int32(1), b)
        cand = cand_u ^ sign
        cnt = count(lambda kk, start: kk >= cand)
        live = done == 0
        prefix = jnp.where(live & (cnt >= k_row), cand_u, prefix)
        hit = live & (cnt == k_row)
        thr = jnp.where(hit, cand, thr)
        done = jnp.where(hit, 1, done)
        return b - 1, prefix, thr, done, jnp.min(done.astype(F32)).astype(I32)

    zeros = jnp.zeros((qn, 1), I32)
    _, prefix, thr, done, all_done = lax.while_loop(
        lambda c: (c[0] >= 0) & (c[4] == 0), bit_body,
        (jnp.int32(31), zeros, zeros, zeros, jnp.int32(0)))
    thr = jnp.where(done == 0, prefix ^ sign, thr)

    @pl.when(all_done == 0)
    def _():
        tied = done == 0
        want = k_row - count(lambda kk, start: kk > thr)
        col0 = lax.broadcasted_iota(I32, (qn, tk), 1)

        def ties_before(x):
            return count(lambda kk, start: (kk == thr) & ((start + col0) < x))

        def idx_body(j, x):
            cand = x | lax.shift_left(jnp.int32(1), n_index_bits - 1 - j)
            return jnp.where(ties_before(cand) < want, cand, x)

        x = lax.fori_loop(0, n_index_bits, idx_body, zeros)

        def fix_body(kt, c):
            start = pl.multiple_of(kt * tk, tk)
            kk = keys_ref[:, pl.ds(start, tk)]
            drop = tied & (kk == thr) & ((start + col0) > x)
            keys_ref[:, pl.ds(start, tk)] = jnp.where(drop, thr - 1, kk)
            return c
        lax.fori_loop(0, n_tiles, fix_body, 0)

    return thr


def _dsa_prompt_kernel(iq_ref, iw_ref, aq_ref, ik_ref, ak_ref, av_ref, o_ref,
                       keys_ref, m_ref, l_ref, acc_ref, *, tq, tk, k_top, n_index_bits):
    i = pl.program_id(1)
    t0 = i * tq
    n_tiles = (t0 + tq + tk - 1) // tk
    lane = lax.broadcasted_iota(I32, (tq, LANES), 1)
    halfmask = [(lane >= HEAD_DIM) == bool(e) for e in range(2)]
    row_t = t0 + lax.broadcasted_iota(I32, (tq, tk), 0)
    col0 = lax.broadcasted_iota(I32, (tq, tk), 1)
    nt_dims = (((1,), (1,)), ((), ()))

    iq = iq_ref[...]
    iw = iw_ref[...]
    q_heads = []
    for h in range(IDX_HEADS):
        qc = iq[:, (h // 2) * LANES:(h // 2 + 1) * LANES]
        q_heads.append(jnp.where(halfmask[h % 2], qc, jnp.zeros_like(qc)))

    def score_body(kt, c):
        start = pl.multiple_of(kt * tk, tk)
        kb = ik_ref[pl.ds(start, tk), :]
        score = jnp.zeros((tq, tk), F32)
        for h in range(IDX_HEADS):
            s = lax.dot_general(q_heads[h], kb, nt_dims, preferred_element_type=F32)
            score = score + jnp.maximum(s, 0.0) * iw[:, h:h + 1]
        key = jnp.where((start + col0) <= row_t, _sortable(score), jnp.int32(INT_MIN))
        keys_ref[:, pl.ds(start, tk)] = key
        return c
    lax.fori_loop(0, n_tiles, score_body, 0)

    t_col = t0 + lax.broadcasted_iota(I32, (tq, 1), 0)
    k_row = jnp.minimum(t_col + 1, k_top)
    thr = _select_topk(keys_ref, n_tiles, tk, k_row, n_index_bits)

    aq = aq_ref[...]
    o_chunks = []
    for p in range(SA_KV_WIDTH // LANES):
        accs = []
        for e in range(2):
            for j in range(2):
                qc = aq[:, (2 * p + j) * LANES:(2 * p + j + 1) * LANES]
                qm = jnp.where(halfmask[e], qc, jnp.zeros_like(qc))
                m_ref[...] = jnp.full_like(m_ref, NEG_BIG)
                l_ref[...] = jnp.zeros_like(l_ref)
                acc_ref[...] = jnp.zeros_like(acc_ref)

                def att_body(kt, c, qm=qm, p=p):
                    start = pl.multiple_of(kt * tk, tk)
                    kb = ak_ref[pl.ds(start, tk), p * LANES:(p + 1) * LANES]
                    vb = av_ref[pl.ds(start, tk), p * LANES:(p + 1) * LANES]
                    s = lax.dot_general(qm, kb, nt_dims, preferred_element_type=F32)
                    s = jnp.where(keys_ref[:, pl.ds(start, tk)] >= thr, s, NEG_BIG)
                    m_old = m_ref[...]
                    m_new = jnp.maximum(m_old, jnp.max(s, axis=1, keepdims=True))
                    alpha = jnp.exp(m_old - m_new)
                    pr = jnp.exp(s - m_new)
                    l_ref[...] = alpha * l_ref[...] + jnp.sum(pr, axis=1, keepdims=True)
                    acc_ref[...] = alpha * acc_ref[...] + jnp.dot(pr.astype(BF16), vb,
                                                                  preferred_element_type=F32)
                    m_ref[...] = m_new
                    return c
                lax.fori_loop(0, n_tiles, att_body, 0)
                accs.append(acc_ref[...] / l_ref[...])
        for j in range(2):
            o_chunks.append((2 * p + j, jnp.where(lane < HEAD_DIM, accs[j], accs[2 + j])))
    for c, val in o_chunks:
        o_ref[:, c * LANES:(c + 1) * LANES] = val.astype(o_ref.dtype)


def _dsa_prompt(iq, iw, aq, ik2, ak, av, batch, seq, tq, tk):
    nq = seq // tq
    k_top = min(TOP_K_MAX, seq // 4)
    n_index_bits = max(1, seq.bit_length())
    qspec = lambda w: pl.BlockSpec((tq, w), lambda b, i: (b * nq + i, 0))
    kspec = lambda w: pl.BlockSpec((seq, w), lambda b, i: (b, 0), pipeline_mode=pl.Buffered(1))
    return pl.pallas_call(
        functools.partial(_dsa_prompt_kernel, tq=tq, tk=tk, k_top=k_top, n_index_bits=n_index_bits),
        grid=(batch, nq),
        in_specs=[qspec(IDX_WIDTH), qspec(IDX_HEADS), qspec(SA_WIDTH),
                  kspec(LANES), kspec(SA_KV_WIDTH), kspec(SA_KV_WIDTH)],
        out_specs=qspec(SA_WIDTH),
        out_shape=jax.ShapeDtypeStruct((batch * seq, SA_WIDTH), BF16),
        scratch_shapes=[pltpu.VMEM((tq, seq), I32), pltpu.VMEM((tq, 1), F32),
                        pltpu.VMEM((tq, 1), F32), pltpu.VMEM((tq, LANES), F32)],
        compiler_params=_cparams(("parallel", "arbitrary")),
        name="dsa_prompt",
    )(iq, iw, aq, ik2, ak, av)


def _sb_sample_kernel(pt_ref, q_ref, kn_ref, vn_ref, kc_ref, vc_ref, o_ref, surv_ref, *, n_tok):
    j = pl.program_id(1)
    rows = q_ref.shape[1]
    q = q_ref[0]
    nt_dims = (((1,), (1,)), ((), ()))
    tri = _newer_tri(PAGE_SIZE)

    @pl.when(j == 0)
    def _():
        kb, vb = kn_ref[0], vn_ref[0]
        z = lax.dot_general(q, kb, nt_dims, preferred_element_type=F32)
        t = lax.broadcasted_iota(I32, (rows, PAGE_SIZE), 0) % n_tok
        mask = lax.broadcasted_iota(I32, (rows, PAGE_SIZE), 1) < t
        a, surv = _stick_break_tile(z, mask, jnp.zeros((rows, 1), F32), tri)
        o_ref[0] = jnp.dot(a.astype(BF16), vb, preferred_element_type=F32)
        surv_ref[...] = surv

    @pl.when(j > 0)
    def _():
        @pl.when(jnp.max(surv_ref[...]) > SURV_FLOOR)
        def _():
            kt = kc_ref[0].astype(BF16)
            vt = vc_ref[0].astype(BF16)
            z = jnp.dot(q, kt, preferred_element_type=F32)
            a, surv = _stick_break_tile(z, None, surv_ref[...], tri)
            o_ref[0] += lax.dot_general(a.astype(BF16), vt, nt_dims, preferred_element_type=F32)
            surv_ref[...] = surv


def _block_diag_queries(q, n_heads_q, n_heads_kv):
    group = n_heads_q // n_heads_kv
    blocks = []
    for h in range(n_heads_q):
        kv = h // group
        qh = q[:, :, h * HEAD_DIM:(h + 1) * HEAD_DIM]
        blocks.append(jnp.pad(qh, ((0, 0), (0, 0), (kv * HEAD_DIM, (n_heads_kv - 1 - kv) * HEAD_DIM))))
    return jnp.concatenate(blocks, axis=1)


def _take_block_diag(acc, n_tok, n_heads_q, n_heads_kv):
    group = n_heads_q // n_heads_kv
    return jnp.concatenate(
        [acc[:, h * n_tok:(h + 1) * n_tok, (h // group) * HEAD_DIM:(h // group + 1) * HEAD_DIM]
         for h in range(n_heads_q)], axis=2)


def _pages_t(cache):
    n_phys, page = cache.shape[:2]
    return jnp.moveaxis(cache, 1, -1).reshape(n_phys, -1, page)


def _pad_rows(a, rows):
    return jnp.pad(a, ((0, 0), (0, rows - a.shape[1]), (0, 0)))


def _sb_sample(q, k, v, cache_k, cache_v, page_table):
    b, n_tok, _ = q.shape
    n_pages = page_table.shape[1]
    rows = SB_HEADS * n_tok
    qbd = _block_diag_queries(q, SB_HEADS, SB_HEADS)
    kn, vn = _pad_rows(k, PAGE_SIZE), _pad_rows(v, PAGE_SIZE)
    page = lambda bi, j, pt: (pt[bi, n_pages - jnp.maximum(j, 1)], 0, 0)
    per_b = lambda bi, j, pt: (bi, 0, 0)
    acc = pl.pallas_call(
        functools.partial(_sb_sample_kernel, n_tok=n_tok),
        grid_spec=pltpu.PrefetchScalarGridSpec(
            num_scalar_prefetch=1,
            grid=(b, n_pages + 1),
            in_specs=[pl.BlockSpec((1, rows, SB_WIDTH), per_b),
                      pl.BlockSpec((1, PAGE_SIZE, SB_WIDTH), per_b),
                      pl.BlockSpec((1, PAGE_SIZE, SB_WIDTH), per_b),
                      pl.BlockSpec((1, SB_WIDTH, PAGE_SIZE), page),
                      pl.BlockSpec((1, SB_WIDTH, PAGE_SIZE), page)],
            out_specs=pl.BlockSpec((1, rows, SB_WIDTH), per_b),
            scratch_shapes=[pltpu.VMEM((rows, 1), F32)]),
        out_shape=jax.ShapeDtypeStruct((b, rows, SB_WIDTH), F32),
        compiler_params=_cparams(("parallel", "arbitrary")),
        name="sb_sample",
    )(page_table, qbd, kn, vn, cache_k, cache_v)
    return _take_block_diag(acc, n_tok, SB_HEADS, SB_HEADS)


def _dsa_sel_kernel(pt_ref, iq_ref, iw_ref, ikn_ref, ikc_ref, sel_ref, keys_ref,
                    *, n_tok, n_pages, k_top, n_index_bits):
    j = pl.program_id(1)
    rows = IDX_HEADS * n_tok
    nt_dims = (((1,), (1,)), ((), ()))

    def scores(s):
        s = jnp.maximum(s, 0.0) * iw_ref[0]
        return jnp.sum(s.reshape(IDX_HEADS, n_tok, PAGE_SIZE), axis=0)

    @pl.when(j < n_pages)
    def _():
        key = _sortable(scores(jnp.dot(iq_ref[0], ikc_ref[0].astype(BF16), preferred_element_type=F32)))
        keys_ref[:, pl.ds(pl.multiple_of(j * PAGE_SIZE, PAGE_SIZE), PAGE_SIZE)] = key

    @pl.when(j == n_pages)
    def _():
        t = lax.broadcasted_iota(I32, (n_tok, PAGE_SIZE), 0)
        col = lax.broadcasted_iota(I32, (n_tok, PAGE_SIZE), 1)
        s_new = lax.dot_general(iq_ref[0], ikn_ref[0], nt_dims, preferred_element_type=F32)
        key = jnp.where(col <= t, _sortable(scores(s_new)), jnp.int32(INT_MIN))
        keys_ref[:, n_pages * PAGE_SIZE:] = key
        t_abs = n_pages * PAGE_SIZE + lax.broadcasted_iota(I32, (n_tok, 1), 0)
        k_row = jnp.minimum(t_abs + 1, k_top)
        thr = _select_topk(keys_ref, n_pages + 1, PAGE_SIZE, k_row, n_index_bits)
        sel_ref[0] = (keys_ref[...] >= thr).astype(F32)


def _dsa_sample_kernel(pt_ref, q_ref, sel_ref, kn_ref, vn_ref, kc_ref, vc_ref, o_ref,
                       m_ref, l_ref, *, n_tok, n_pages):
    j = pl.program_id(1)
    rows = SA_HEADS * n_tok
    nt_dims = (((1,), (1,)), ((), ()))

    @pl.when(j == 0)
    def _():
        m_ref[...] = jnp.full_like(m_ref, NEG_BIG)
        l_ref[...] = jnp.zeros_like(l_ref)
        o_ref[...] = jnp.zeros_like(o_ref)

    def step(s, pv):
        sel = jnp.broadcast_to(sel_ref[0][None], (SA_HEADS, n_tok, PAGE_SIZE)).reshape(rows, PAGE_SIZE)
        s = jnp.where(sel > 0.5, s, NEG_BIG)
        m_old = m_ref[...]
        m_new = jnp.maximum(m_old, jnp.max(s, axis=1, keepdims=True))
        alpha = jnp.exp(m_old - m_new)
        pr = jnp.exp(s - m_new)
        l_ref[...] = alpha * l_ref[...] + jnp.sum(pr, axis=1, keepdims=True)
        o_ref[0] = alpha * o_ref[0] + pv(pr.astype(BF16))
        m_ref[...] = m_new

    @pl.when(j < n_pages)
    def _():
        vt = vc_ref[0].astype(BF16)
        step(jnp.dot(q_ref[0], kc_ref[0].astype(BF16), preferred_element_type=F32),
             lambda pr: lax.dot_general(pr, vt, nt_dims, preferred_element_type=F32))

    @pl.when(j == n_pages)
    def _():
        step(lax.dot_general(q_ref[0], kn_ref[0], nt_dims, preferred_element_type=F32),
             lambda pr: jnp.dot(pr, vn_ref[0], preferred_element_type=F32))
        o_ref[0] = o_ref[0] / l_ref[...]


def _dsa_sample(aq, ak, av, iq, iw, ik, cache_k, cache_v, cache_ik, page_table):
    b, n_tok, _ = aq.shape
    n_pages = page_table.shape[1]
    n_keys = (n_pages + 1) * PAGE_SIZE
    k_top = min(TOP_K_MAX, (n_pages * PAGE_SIZE + n_tok) // 4)
    rows = SA_HEADS * n_tok
    per_b = lambda bi, j, pt: (bi, 0, 0)
    page = lambda bi, j, pt: (pt[bi, jnp.minimum(j, n_pages - 1)], 0, 0)

    iq_rows = iq.reshape(b, n_tok, IDX_HEADS, IDX_DIM).transpose(0, 2, 1, 3).reshape(b, rows, IDX_DIM)
    iw_rows = iw.transpose(0, 2, 1).reshape(b, rows, 1)
    sel = pl.pallas_call(
        functools.partial(_dsa_sel_kernel, n_tok=n_tok, n_pages=n_pages, k_top=k_top,
                          n_index_bits=max(1, n_keys.bit_length())),
        grid_spec=pltpu.PrefetchScalarGridSpec(
            num_scalar_prefetch=1,
            grid=(b, n_pages + 1),
            in_specs=[pl.BlockSpec((1, rows, IDX_DIM), per_b),
                      pl.BlockSpec((1, rows, 1), per_b),
                      pl.BlockSpec((1, PAGE_SIZE, IDX_DIM), per_b),
                      pl.BlockSpec((1, IDX_DIM, PAGE_SIZE), page)],
            out_specs=pl.BlockSpec((1, n_tok, n_keys), per_b),
            scratch_shapes=[pltpu.VMEM((n_tok, n_keys), I32)]),
        out_shape=jax.ShapeDtypeStruct((b, n_tok, n_keys), F32),
        compiler_params=_cparams(("parallel", "arbitrary")),
        name="dsa_sel",
    )(page_table, iq_rows, iw_rows, _pad_rows(ik, PAGE_SIZE), cache_ik)

    qbd = _block_diag_queries(aq, SA_HEADS, SA_KV_HEADS)
    acc = pl.pallas_call(
        functools.partial(_dsa_sample_kernel, n_tok=n_tok, n_pages=n_pages),
        grid_spec=pltpu.PrefetchScalarGridSpec(
            num_scalar_prefetch=1,
            grid=(b, n_pages + 1),
            in_specs=[pl.BlockSpec((1, rows, SA_KV_WIDTH), per_b),
                      pl.BlockSpec((1, n_tok, PAGE_SIZE), lambda bi, j, pt: (bi, 0, j)),
                      pl.BlockSpec((1, PAGE_SIZE, SA_KV_WIDTH), per_b),
                      pl.BlockSpec((1, PAGE_SIZE, SA_KV_WIDTH), per_b),
                      pl.BlockSpec((1, SA_KV_WIDTH, PAGE_SIZE), page),
                      pl.BlockSpec((1, SA_KV_WIDTH, PAGE_SIZE), page)],
            out_specs=pl.BlockSpec((1, rows, SA_KV_WIDTH), per_b),
            scratch_shapes=[pltpu.VMEM((rows, 1), F32), pltpu.VMEM((rows, 1), F32)]),
        out_shape=jax.ShapeDtypeStruct((b, rows, SA_KV_WIDTH), F32),
        compiler_params=_cparams(("parallel", "arbitrary")),
        name="dsa_sample",
    )(page_table, qbd, sel, _pad_rows(ak, PAGE_SIZE), _pad_rows(av, PAGE_SIZE), cache_k, cache_v)
    return _take_block_diag(acc, n_tok, SA_HEADS, SA_KV_HEADS)


def _tail_kernel(x_ref, osb_ref, osa_ref, gate_ref, wsb_ref, wsa_ref, wo_ref, g2_ref, wup_ref,
                 wdn_ref, y_ref, *, ff_chunk):
    gate = gate_ref[...]
    merged = (gate[:, :D_MODEL] * jnp.dot(osb_ref[...], wsb_ref[...], preferred_element_type=F32)
              + gate[:, D_MODEL:] * jnp.dot(osa_ref[...], wsa_ref[...], preferred_element_type=F32))
    x1 = x_ref[...] + jnp.dot(merged.astype(BF16), wo_ref[...], preferred_element_type=F32)
    ms = jnp.mean(x1 * x1, axis=-1, keepdims=True)
    h = (x1 * lax.rsqrt(ms + RMS_EPS) * g2_ref[...]).astype(BF16)
    y = x1
    for c in range(D_FF // ff_chunk):
        sl = slice(c * ff_chunk, (c + 1) * ff_chunk)
        u = jnp.maximum(jnp.dot(h, wup_ref[:, sl], preferred_element_type=F32), 0.0)
        y = y + jnp.dot((u * u).astype(BF16), wdn_ref[sl, :], preferred_element_type=F32)
    y_ref[...] = y


def _tail(x2d, o_sb, o_sa, gates, w_sb_out, w_sa_out, w_o, g2, w_up, w_down, tm):
    r = x2d.shape[0]
    row = lambda w: pl.BlockSpec((tm, w), lambda i: (i, 0))
    res = lambda shape: pl.BlockSpec(shape, lambda i: (0, 0), pipeline_mode=pl.Buffered(1))
    return pl.pallas_call(
        functools.partial(_tail_kernel, ff_chunk=1024),
        grid=(r // tm,),
        in_specs=[row(D_MODEL), row(SB_WIDTH), row(SA_WIDTH), row(2 * D_MODEL),
                  res((SB_WIDTH, D_MODEL)), res((SA_WIDTH, D_MODEL)), res((D_MODEL, D_MODEL)),
                  res((1, D_MODEL)), res((D_MODEL, D_FF)), res((D_FF, D_MODEL))],
        out_specs=row(D_MODEL),
        out_shape=jax.ShapeDtypeStruct((r, D_MODEL), F32),
        compiler_params=_cparams(("parallel",)),
        name="tail",
    )(x2d, o_sb, o_sa, gates, w_sb_out, w_sa_out, w_o, g2.reshape(1, D_MODEL), w_up, w_down)


def _unpermute_heads(a):
    shp = a.shape
    a = a.reshape(shp[:-1] + (SA_HEADS, HEAD_DIM))
    return a[..., jnp.array(SA_Q_PERM), :].reshape(shp)


def kernel(x_prompt, x_sample, cache_sb_k, cache_sb_v, cache_sa_k, cache_sa_v, cache_idx_k, page_table,
           norm1_g, w_in, b_gate, q_norm_g, k_norm_g, idx_k_norm_g, w_sb_out, w_sa_out, w_o,
           norm2_g, w_up, w_down):
    depth = w_in.shape[0]
    batch, seq, _ = x_prompt.shape
    dec_batch, dec_seq, _ = x_sample.shape
    n_phys = cache_sb_k.shape[1]
    past_len = page_table.shape[1] * PAGE_SIZE
    assert seq % 512 == 0 and cache_sb_k.shape[2] == PAGE_SIZE
    pos_p = jnp.arange(seq, dtype=F32)
    pos_s = past_len + jnp.arange(dec_seq, dtype=F32)
    tm_p = 256
    tm_s = 256 if (dec_batch * dec_seq) % 256 == 0 else dec_seq

    xp = x_prompt.reshape(batch * seq, D_MODEL)
    xs = x_sample.reshape(dec_batch * dec_seq, D_MODEL)
    new_p = [[] for _ in range(5)]
    new_s = [[] for _ in range(5)]
    for layer in range(depth):
        w_packed = _pack_w_in(w_in[layer])
        w_sa_perm = _unpermute_heads(w_sa_out[layer].T).T.astype(BF16)
        tail_w = (w_sb_out[layer].astype(BF16), w_sa_perm, w_o[layer].astype(BF16), norm2_g[layer],
                  w_up[layer].astype(BF16), w_down[layer].astype(BF16))
        norm_args = (norm1_g[layer], q_norm_g[layer], k_norm_g[layer], idx_k_norm_g[layer], b_gate[layer])

        (skf, svf, akf, avf, ikf, sbq, sbk, sbv, saq, sak, sav, iq, ik2, iw, gates) = _project(
            xp, pos_p, batch, w_packed, *norm_args, tm=tm_p)
        o_sb = _sb_prompt(sbq, sbk, sbv, batch, seq, tq=256)
        o_sa = _dsa_prompt(iq, iw, saq, ik2, sak, sav, batch, seq, tq=128, tk=512)
        xp = _tail(xp, o_sb, o_sa, gates, *tail_w, tm=256)
        for lst, arr, hh in zip(new_p, (skf, svf, akf, avf, ikf), (SB_HEADS, SB_HEADS, SA_KV_HEADS, SA_KV_HEADS, 0)):
            lst.append(arr.reshape((batch, seq, hh, HEAD_DIM) if hh else (batch, seq, IDX_DIM)))

        (skf, svf, akf, avf, ikf, sbq, sbk, sbv, saq, sak, sav, iq, ik2, iw, gates) = _project(
            xs, pos_s, dec_batch, w_packed, *norm_args, tm=tm_s)
        b3 = lambda a: a.reshape(dec_batch, dec_seq, a.shape[-1])
        o_sb = _sb_sample(b3(sbq), b3(sbk), b3(sbv),
                          _pages_t(cache_sb_k[layer]), _pages_t(cache_sb_v[layer]), page_table)
        o_sa = _dsa_sample(_unpermute_heads(b3(saq)), b3(sak), b3(sav), b3(iq), b3(iw),
                           b3(ik2)[..., :IDX_DIM],
                           _pages_t(cache_sa_k[layer]), _pages_t(cache_sa_v[layer]),
                           _pages_t(cache_idx_k[layer]), page_table)
        o_sb = o_sb.reshape(dec_batch * dec_seq, SB_WIDTH).astype(BF16)
        o_sa = _unpermute_heads(o_sa).reshape(dec_batch * dec_seq, SA_WIDTH).astype(BF16)
        xs = _tail(xs, o_sb, o_sa, gates, *tail_w, tm=tm_s)
        for lst, arr, hh in zip(new_s, (skf, svf, akf, avf, ikf), (SB_HEADS, SB_HEADS, SA_KV_HEADS, SA_KV_HEADS, 0)):
            lst.append(arr.reshape((dec_batch, dec_seq, hh, HEAD_DIM) if hh else (dec_batch, dec_seq, IDX_DIM)))

    return (xp.reshape(batch, seq, D_MODEL), xs.reshape(dec_batch, dec_seq, D_MODEL),
            *(jnp.stack(l) for l in new_p), *(jnp.stack(l) for l in new_s))
```

```python
import functools

import jax
import jax.numpy as jnp
from jax import lax
from jax.experimental import pallas as pl
from jax.experimental.pallas import tpu as pltpu

F32 = jnp.float32
BF16 = jnp.bfloat16
I32 = jnp.int32

D_MODEL = 1024
HEAD_DIM = 64
SB_HEADS = 8
SA_HEADS = 8
SA_KV_HEADS = 4
IDX_HEADS = 8
IDX_DIM = 64
SB_WIDTH = SB_HEADS * HEAD_DIM
SA_WIDTH = SA_HEADS * HEAD_DIM
SA_KV_WIDTH = SA_KV_HEADS * HEAD_DIM
IDX_WIDTH = IDX_HEADS * IDX_DIM
TOP_K_MAX = 256
ROT_DIM = HEAD_DIM // 4
ROPE_THETA = 500000.0
D_FF = 4 * D_MODEL
PAGE_SIZE = 128
RMS_EPS = 1e-6
QK_SCALE = HEAD_DIM ** -0.5
SA_Q_SCALE = QK_SCALE * 1.4426950408889634
IDX_SCALE = IDX_DIM ** -0.5
IDX_W_SCALE = IDX_HEADS ** -0.5

LANES = 128
VMEM_LIMIT = 56 * 1024 * 1024

C_SQ, C_SK, C_SV = 0, 512, 1024
C_AQ, C_AK, C_AV = 1536, 2048, 2304
C_IQ, C_IK2, C_IW = 2560, 3072, 3200
C_GATE = 3328
C_END = C_GATE + 2 * D_MODEL
WT_ROWS = SA_KV_WIDTH + 16
SA_Q_PERM = (0, 2, 1, 3, 4, 6, 5, 7)

SURV_FLOOR = -105.0
NEG_BIG = -1e30
INT_MIN = -2 ** 31

SB_SAMPLE_HEAD_PAGES = 4
DSA_SEL_GROUP = 16
DSA_SAMPLE_GROUP = 8


def _largest_divisor(n, cap):
    return max(d for d in range(1, min(n, cap) + 1) if n % d == 0)


def _cparams(sem, vmem=VMEM_LIMIT):
    return pltpu.CompilerParams(dimension_semantics=sem, vmem_limit_bytes=vmem)


def _rope_chunk(y, cos, sin_a, sin_b):
    up = pltpu.roll(y, LANES - ROT_DIM // 2, 1)
    dn = pltpu.roll(y, ROT_DIM // 2, 1)
    return y * cos + up * sin_a + dn * sin_b


def _head_rms_chunk(y, e_mat, gain):
    y2 = y * y
    hi = y2.astype(BF16)
    lo = (y2 - hi.astype(F32)).astype(BF16)
    ms = (jnp.dot(hi, e_mat, preferred_element_type=F32)
          + jnp.dot(lo, e_mat, preferred_element_type=F32))
    return y * lax.rsqrt(ms + RMS_EPS) * gain


def _proj_kernel(x_ref, g1_ref, w_ref, wt_ref, qg_ref, kg_ref, ig_ref, bg_ref, cos_ref, sa_ref, sb_ref,
                 e_ref,
                 skf_ref, svf_ref, akf_ref, avf_ref, ikf_ref,
                 sbq_ref, sbk_ref, sbv_ref, saq_ref, sak_ref, sav_ref, iq_ref, ik2_ref, gate_ref,
                 avt_ref, iwt_ref):
    x = x_ref[...]
    ms = jnp.mean(x * x, axis=-1, keepdims=True)
    h = (x * lax.rsqrt(ms + RMS_EPS) * g1_ref[...]).astype(BF16)

    def mm(lo, hi):
        return jnp.dot(h, w_ref[:, lo:hi], preferred_element_type=F32)

    cos, sin_a, sin_b = cos_ref[...], sa_ref[...], sb_ref[...]
    e_mat = e_ref[...]

    p = mm(C_SQ, C_SK)
    sbq_ref[...] = (p * QK_SCALE).astype(BF16)
    p = mm(C_SK, C_SV)
    skf_ref[...] = p
    sbk_ref[...] = p.astype(BF16)
    p = mm(C_SV, C_AQ)
    svf_ref[...] = p
    sbv_ref[...] = p.astype(BF16)

    p = mm(C_AQ, C_AK)
    for c in range(SA_WIDTH // LANES):
        sl = slice(c * LANES, (c + 1) * LANES)
        y = _rope_chunk(_head_rms_chunk(p[:, sl], e_mat, qg_ref[...]), cos, sin_a, sin_b)
        saq_ref[:, sl] = (y * SA_Q_SCALE).astype(BF16)
    p = mm(C_AK, C_AV)
    for c in range(SA_KV_WIDTH // LANES):
        sl = slice(c * LANES, (c + 1) * LANES)
        y = _rope_chunk(_head_rms_chunk(p[:, sl], e_mat, kg_ref[...]), cos, sin_a, sin_b)
        akf_ref[:, sl] = y
        sak_ref[:, sl] = y.astype(BF16)
    p = mm(C_AV, C_IQ)
    avf_ref[...] = p
    sav_ref[...] = p.astype(BF16)

    p = mm(C_IQ, C_IK2)
    for c in range(IDX_WIDTH // LANES):
        sl = slice(c * LANES, (c + 1) * LANES)
        iq_ref[:, sl] = (_rope_chunk(p[:, sl], cos, sin_a, sin_b) * IDX_SCALE).astype(BF16)
    p = mm(C_IK2, C_GATE)
    y = _rope_chunk(_head_rms_chunk(p[:, :LANES], e_mat, ig_ref[...]), cos, sin_a, sin_b)
    ikf_ref[...] = y[:, :IDX_DIM]
    ik2_ref[...] = y.astype(BF16)

    p = mm(C_GATE, C_END) + bg_ref[...]
    gate_ref[...] = jax.nn.sigmoid(p)

    pt = lax.dot_general(wt_ref[...], h, (((1,), (1,)), ((), ())), preferred_element_type=F32)
    avt_ref[...] = pt[:SA_KV_WIDTH].astype(BF16)
    iwt_ref[...] = pt[SA_KV_WIDTH:SA_KV_WIDTH + IDX_HEADS] * IDX_W_SCALE


def _rope_tables(pos):
    half = ROT_DIM // 2
    inv = 1.0 / (ROPE_THETA ** (jnp.arange(0, ROT_DIM, 2, dtype=F32) / ROT_DIM))
    ang = pos[:, None] * inv[None, :]
    cos, sin = jnp.cos(ang), jnp.sin(ang)
    t = pos.shape[0]
    zeros = jnp.zeros((t, HEAD_DIM - ROT_DIM), F32)
    zh = jnp.zeros((t, half), F32)
    cos_h = jnp.concatenate([cos, cos, zeros + 1.0], axis=1)
    sa_h = jnp.concatenate([-sin, zh, zeros], axis=1)
    sb_h = jnp.concatenate([zh, sin, zeros], axis=1)
    return tuple(jnp.concatenate([a, a], axis=1) for a in (cos_h, sa_h, sb_h))


def _pack_w_in(w_in):
    o = 0
    parts = []
    for n in (SB_WIDTH, SB_WIDTH, SB_WIDTH, SA_WIDTH, SA_KV_WIDTH, SA_KV_WIDTH, IDX_WIDTH,
              IDX_DIM, IDX_HEADS, 2 * D_MODEL):
        parts.append(w_in[:, o:o + n])
        o += n
    sq, sk, sv, aq, ak, av, iq, ik, iw, gates = parts
    aq = aq.reshape(D_MODEL, SA_HEADS, HEAD_DIM)[:, jnp.array(SA_Q_PERM)].reshape(D_MODEL, SA_WIDTH)
    pad = jnp.zeros((D_MODEL, C_GATE - C_IW), w_in.dtype)
    w = jnp.concatenate([sq, sk, sv, aq, ak, av, iq, ik, ik, pad, gates], axis=1).astype(BF16)
    wt = jnp.concatenate([av, iw, jnp.zeros((D_MODEL, WT_ROWS - SA_KV_WIDTH - IDX_HEADS), w_in.dtype)],
                         axis=1).T.astype(BF16)
    return w, wt


def _project(x2d, pos, n_rep, w_packed, g1, qg, kg, ig, bg, tm):
    r = x2d.shape[0]
    t = pos.shape[0]
    assert r == n_rep * t and r % tm == 0 and (t % tm == 0 or tm % t == 0)
    cos, sin_a, sin_b = _rope_tables(pos)
    if tm > t:
        cos, sin_a, sin_b = (jnp.tile(a, (tm // t, 1)) for a in (cos, sin_a, sin_b))
    nt = cos.shape[0] // tm
    e_mat = (jnp.arange(LANES)[:, None] // HEAD_DIM == jnp.arange(LANES)[None, :] // HEAD_DIM)
    e_mat = (e_mat.astype(F32) / HEAD_DIM).astype(BF16)

    def row(width):
        return pl.BlockSpec((tm, width), lambda i: (i, 0))

    def const(shape):
        return pl.BlockSpec(shape, lambda i: (0, 0))

    tab = pl.BlockSpec((tm, LANES), lambda i: (i % nt, 0))
    tile2 = lambda g: jnp.concatenate([g, g]).reshape(1, LANES)
    out_widths = [(SB_WIDTH, F32), (SB_WIDTH, F32), (SA_KV_WIDTH, F32), (SA_KV_WIDTH, F32),
                  (IDX_DIM, F32),
                  (SB_WIDTH, BF16), (SB_WIDTH, BF16), (SB_WIDTH, BF16),
                  (SA_WIDTH, BF16), (SA_KV_WIDTH, BF16), (SA_KV_WIDTH, BF16),
                  (IDX_WIDTH, BF16), (LANES, BF16), (2 * D_MODEL, F32)]
    out_heights = [(SA_KV_WIDTH, BF16), (IDX_HEADS, F32)]
    w, wt = w_packed
    return pl.pallas_call(
        _proj_kernel,
        grid=(r // tm,),
        in_specs=[row(D_MODEL), const((1, D_MODEL)),
                  pl.BlockSpec((D_MODEL, C_END), lambda i: (0, 0), pipeline_mode=pl.Buffered(1)),
                  const((WT_ROWS, D_MODEL)),
                  const((1, LANES)), const((1, LANES)), const((1, LANES)), const((1, 2 * D_MODEL)),
                  tab, tab, tab, const((LANES, LANES))],
        out_specs=[row(w_) for w_, _ in out_widths]
                  + [pl.BlockSpec((h_, tm), lambda i: (0, i)) for h_, _ in out_heights],
        out_shape=[jax.ShapeDtypeStruct((r, w_), dt) for w_, dt in out_widths]
                  + [jax.ShapeDtypeStruct((h_, r), dt) for h_, dt in out_heights],
        compiler_params=_cparams(("parallel",)),
        name="proj",
    )(x2d, g1.reshape(1, D_MODEL), w, wt, tile2(qg), tile2(kg), tile2(ig),
      bg.reshape(1, 2 * D_MODEL), cos, sin_a, sin_b, e_mat)


def _stick_break_tile(z, mask, surv, tri):
    t = jnp.log1p(jnp.exp(-jnp.abs(z)))
    l = -(jnp.maximum(z, 0.0) + t)
    if mask is not None:
        l = jnp.where(mask, l, 0.0)
    ls = jnp.minimum(z, 0.0) - t
    hi = l.astype(BF16)
    lo = (l - hi.astype(F32)).astype(BF16)
    ex = jnp.dot(hi, tri, preferred_element_type=F32) + jnp.dot(lo, tri, preferred_element_type=F32)
    a = jnp.exp(ls + ex + surv)
    if mask is not None:
        a = jnp.where(mask, a, 0.0)
    return a, surv + jnp.sum(l, axis=1, keepdims=True)


def _newer_tri(n):
    j = lax.broadcasted_iota(I32, (n, n), 0)
    s = lax.broadcasted_iota(I32, (n, n), 1)
    return (j > s).astype(BF16)


def _sb_prompt_kernel(q_ref, k_ref, v_ref, o_ref, acc_ref, surv_ref, *, tq):
    i = pl.program_id(2)
    t0 = i * tq
    lane = lax.broadcasted_iota(I32, (tq, LANES), 1)
    row_t = t0 + lax.broadcasted_iota(I32, (tq, tq), 0)
    col0 = lax.broadcasted_iota(I32, (tq, tq), 1)
    tri = _newer_tri(tq)
    q = q_ref[...]
    outs = []
    for half in range(2):
        qm = jnp.where((lane >= HEAD_DIM) == bool(half), q, jnp.zeros_like(q))
        acc_ref[...] = jnp.zeros_like(acc_ref)
        surv_ref[...] = jnp.zeros_like(surv_ref)

        def body(carry):
            kt, _ = carry
            start = pl.multiple_of(kt * tq, tq)
            kb = k_ref[pl.ds(start, tq), :]
            vb = v_ref[pl.ds(start, tq), :]
            z = lax.dot_general(qm, kb, (((1,), (1,)), ((), ())), preferred_element_type=F32)
            mask = (start + col0) < row_t
            a, surv = _stick_break_tile(z, mask, surv_ref[...], tri)
            acc_ref[...] += jnp.dot(a.astype(BF16), vb, preferred_element_type=F32)
            surv_ref[...] = surv
            return kt - 1, jnp.max(surv)

        lax.while_loop(lambda c: (c[0] >= 0) & (c[1] > SURV_FLOOR), body, (i, jnp.float32(0.0)))
        outs.append(acc_ref[...])
    o_ref[...] = jnp.where(lane < HEAD_DIM, outs[0], outs[1]).astype(o_ref.dtype)


def _sb_prompt(q, k, v, batch, seq, tq):
    nq = seq // tq
    n_chunks = SB_WIDTH // LANES
    qspec = pl.BlockSpec((tq, LANES), lambda b, c, i: (b * nq + i, c))
    kvspec = pl.BlockSpec((seq, LANES), lambda b, c, i: (b, c))
    return pl.pallas_call(
        functools.partial(_sb_prompt_kernel, tq=tq),
        grid=(batch, n_chunks, nq),
        in_specs=[qspec, kvspec, kvspec],
        out_specs=qspec,
        out_shape=jax.ShapeDtypeStruct((batch * seq, SB_WIDTH), BF16),
        scratch_shapes=[pltpu.VMEM((tq, LANES), F32), pltpu.VMEM((tq, 1), F32)],
        compiler_params=_cparams(("parallel", "parallel", "arbitrary")),
        name="sb_prompt",
    )(q, k, v)


def _sortable(x):
    bits = lax.bitcast_convert_type(x, I32)
    key = jnp.where(bits < 0, bits ^ jnp.int32(0x7FFFFFFF), bits)
    return jnp.where(x == 0.0, jnp.int32(0), key)


def _select_topk(keys_ref, n_tiles, tk, k_row, n_index_bits, key_axis=1):
    qn = keys_ref.shape[1 - key_axis]
    sign = jnp.int32(INT_MIN)
    tile_shape = (qn, tk) if key_axis == 1 else (tk, qn)
    fold = LANES if key_axis == 1 else 8
    vec = (qn, 1) if key_axis == 1 else (1, qn)

    def tile_at(kt):
        start = pl.multiple_of(kt * tk, tk)
        idx = (slice(None), pl.ds(start, tk)) if key_axis == 1 else (pl.ds(start, tk), slice(None))
        return start, idx

    def count(pred):
        def tile_body(kt, acc):
            start, idx = tile_at(kt)
            m = pred(keys_ref[idx], start).astype(I32)
            if key_axis == 1:
                for c in range(tk // fold):
                    acc = acc + m[:, c * fold:(c + 1) * fold]
                return acc
            return acc + jnp.sum(m.reshape(tk // fold, fold, qn), axis=0)
        acc0 = jnp.zeros((qn, fold) if key_axis == 1 else (fold, qn), I32)
        acc = lax.fori_loop(0, n_tiles, tile_body, acc0)
        return jnp.sum(acc.astype(F32), axis=key_axis, keepdims=True).astype(I32)

    def bit_body(carry):
        b, prefix, thr, done, _ = carry
        cand_u = prefix | lax.shift_left(jnp.int32(1), b)
        cand = cand_u ^ sign
        cnt = count(lambda kk, start: kk >= cand)
        live = done == 0
        prefix = jnp.where(live & (cnt >= k_row), cand_u, prefix)
        hit = live & (cnt == k_row)
        thr = jnp.where(hit, cand, thr)
        done = jnp.where(hit, 1, done)
        return b - 1, prefix, thr, done, jnp.min(done.astype(F32)).astype(I32)

    zeros = jnp.zeros(vec, I32)
    _, prefix, thr, done, all_done = lax.while_loop(
        lambda c: (c[0] >= 0) & (c[4] == 0), bit_body,
        (jnp.int32(31), zeros, zeros, zeros, jnp.int32(0)))
    thr = jnp.where(done == 0, prefix ^ sign, thr)

    @pl.when(all_done == 0)
    def _():
        tied = done == 0
        want = k_row - count(lambda kk, start: kk > thr)
        col0 = lax.broadcasted_iota(I32, tile_shape, key_axis)

        def ties_before(x):
            return count(lambda kk, start: (kk == thr) & ((start + col0) < x))

        def idx_body(j, x):
            cand = x | lax.shift_left(jnp.int32(1), n_index_bits - 1 - j)
            return jnp.where(ties_before(cand) < want, cand, x)

        x = lax.fori_loop(0, n_index_bits, idx_body, zeros)

        def fix_body(kt, c):
            start, idx = tile_at(kt)
            kk = keys_ref[idx]
            drop = tied & (kk == thr) & ((start + col0) > x)
            keys_ref[idx] = jnp.where(drop, thr - 1, kk)
            return c
        lax.fori_loop(0, n_tiles, fix_body, 0)

    return thr


def _dsa_prompt_kernel(iq_ref, iwt_ref, aq_ref, ik_ref, ak_ref, avt_ref, o_ref,
                       keys_ref, iqt_ref, aqt_ref, m_ref, l_ref, acc_ref,
                       *, tq, tk, tka, k_top, n_index_bits):
    assert tq == LANES
    i = pl.program_id(1)
    t0 = i * tq
    n_tiles = (t0 + tq + tk - 1) // tk
    dim_row = lax.broadcasted_iota(I32, (LANES, tq), 0)
    halfmask = [(dim_row >= HEAD_DIM) == bool(e) for e in range(2)]
    t_query = t0 + lax.broadcasted_iota(I32, (1, tq), 1)

    def chunk_t(ref, c):
        return ref[:, c * LANES:(c + 1) * LANES].astype(F32).T

    for c in range(IDX_WIDTH // LANES):
        qt = chunk_t(iq_ref, c)
        for e in range(2):
            iqt_ref[c, :, e * tq:(e + 1) * tq] = jnp.where(halfmask[e], qt, 0.0).astype(BF16)

    def score_body(kt, c):
        start = pl.multiple_of(kt * tk, tk)
        kb = ik_ref[pl.ds(start, tk), :]
        score = jnp.zeros((tk, tq), F32)
        for c2 in range(IDX_WIDTH // LANES):
            s2 = jnp.dot(kb, iqt_ref[c2], preferred_element_type=F32)
            for e in range(2):
                h = 2 * c2 + e
                score = score + jnp.maximum(s2[:, e * tq:(e + 1) * tq], 0.0) * iwt_ref[h:h + 1, :]
        pos = start + lax.broadcasted_iota(I32, (tk, tq), 0)
        keys_ref[pl.ds(start, tk), :] = jnp.where(pos <= t_query, _sortable(score), jnp.int32(INT_MIN))
        return c
    lax.fori_loop(0, n_tiles, score_body, 0)

    k_row = jnp.minimum(t_query + 1, k_top)
    thr = _select_topk(keys_ref, n_tiles, tk, k_row, n_index_bits, key_axis=0)

    for c in range(SA_WIDTH // LANES):
        qt = chunk_t(aq_ref, c)
        p, j = c // 2, c % 2
        for e in range(2):
            aqt_ref[2 * p + e, :, j * tq:(j + 1) * tq] = jnp.where(halfmask[e], qt, 0.0).astype(BF16)
    m_ref[...] = jnp.full_like(m_ref, NEG_BIG)
    l_ref[...] = jnp.zeros_like(l_ref)
    acc_ref[...] = jnp.zeros_like(acc_ref)

    def att_body(kt, c):
        start = pl.multiple_of(kt * tka, tka)
        sel = keys_ref[pl.ds(start, tka), :] >= thr
        m_all, l_all = m_ref[...], l_ref[...]
        new_m, new_l, new_acc = [], [], []
        logits = []
        for g in range(SA_KV_HEADS):
            kb = ak_ref[pl.ds(start, tka), (g // 2) * LANES:(g // 2 + 1) * LANES]
            logits.append(jnp.dot(kb, aqt_ref[g], preferred_element_type=F32))
        probs, alphas = [], []
        for hh in range(SA_HEADS):
            g, j = hh // 2, hh % 2
            s = jnp.where(sel, logits[g][:, j * tq:(j + 1) * tq], NEG_BIG)
            m_old = m_all[hh:hh + 1]
            m_new = jnp.maximum(m_old, jnp.max(s, axis=0, keepdims=True))
            alpha = jnp.exp2(m_old - m_new)
            pr = jnp.exp2(s - m_new)
            new_m.append(m_new)
            new_l.append(alpha * l_all[hh:hh + 1] + jnp.sum(pr, axis=0, keepdims=True))
            probs.append(pr.astype(BF16))
            alphas.append(alpha)
        for hh in range(SA_HEADS):
            g = hh // 2
            vt = avt_ref[g * HEAD_DIM:(g + 1) * HEAD_DIM, pl.ds(start, tka)]
            new_acc.append(alphas[hh] * acc_ref[hh * HEAD_DIM:(hh + 1) * HEAD_DIM]
                           + jnp.dot(vt, probs[hh], preferred_element_type=F32))
        m_ref[...] = jnp.concatenate(new_m, axis=0)
        l_ref[...] = jnp.concatenate(new_l, axis=0)
        acc_ref[...] = jnp.concatenate(new_acc, axis=0)
        return c
    lax.fori_loop(0, (t0 + tq + tka - 1) // tka, att_body, 0)

    def head_out(hh):
        return acc_ref[hh * HEAD_DIM:(hh + 1) * HEAD_DIM] / l_ref[hh:hh + 1]

    for p in range(SA_KV_WIDTH // LANES):
        for j in range(2):
            ot = jnp.concatenate([head_out(2 * (2 * p) + j), head_out(2 * (2 * p + 1) + j)], axis=0)
            o_ref[:, (2 * p + j) * LANES:(2 * p + j + 1) * LANES] = ot.T.astype(o_ref.dtype)


def _dsa_prompt(iq, iwt, aq, ik2, ak, avt, batch, seq, tq, tk, tka):
    nq = seq // tq
    k_top = min(TOP_K_MAX, seq // 4)
    n_index_bits = max(1, seq.bit_length())
    qspec = lambda w: pl.BlockSpec((tq, w), lambda b, i: (b * nq + i, 0))
    kspec = lambda w: pl.BlockSpec((seq, w), lambda b, i: (b, 0), pipeline_mode=pl.Buffered(1))
    return pl.pallas_call(
        functools.partial(_dsa_prompt_kernel, tq=tq, tk=tk, tka=tka, k_top=k_top,
                          n_index_bits=n_index_bits),
        grid=(batch, nq),
        in_specs=[qspec(IDX_WIDTH),
                  pl.BlockSpec((IDX_HEADS, tq), lambda b, i: (0, b * nq + i)),
                  qspec(SA_WIDTH), kspec(LANES), kspec(SA_KV_WIDTH),
                  pl.BlockSpec((SA_KV_WIDTH, seq), lambda b, i: (0, b), pipeline_mode=pl.Buffered(1))],
        out_specs=qspec(SA_WIDTH),
        out_shape=jax.ShapeDtypeStruct((batch * seq, SA_WIDTH), BF16),
        scratch_shapes=[pltpu.VMEM((seq, tq), I32),
                        pltpu.VMEM((IDX_HEADS // 2, LANES, 2 * tq), BF16),
                        pltpu.VMEM((SA_KV_HEADS, LANES, 2 * tq), BF16),
                        pltpu.VMEM((SA_HEADS, tq), F32), pltpu.VMEM((SA_HEADS, tq), F32),
                        pltpu.VMEM((SA_HEADS * HEAD_DIM, tq), F32)],
        compiler_params=_cparams(("parallel", "arbitrary")),
        name="dsa_prompt",
    )(iq, iwt, aq, ik2, ak, avt)


def _sb_page_step(q, kc_ref, vc_ref, o_ref, surv_ref, tri):
    @pl.when(jnp.max(surv_ref[0]) > SURV_FLOOR)
    def _():
        kt = kc_ref[0].astype(BF16)
        vt = vc_ref[0].astype(BF16)
        z = jnp.dot(q, kt, preferred_element_type=F32)
        a, surv = _stick_break_tile(z, None, surv_ref[0][:, :1], tri)
        o_ref[0] += lax.dot_general(a.astype(BF16), vt, (((1,), (1,)), ((), ())),
                                    preferred_element_type=F32)
        surv_ref[0] = jnp.broadcast_to(surv, surv_ref.shape[1:])


def _sb_sample_head_kernel(pt_ref, q_ref, kn_ref, vn_ref, *rest, n_tok, n_head_pages):
    pages, (o_ref, surv_ref) = rest[:2 * n_head_pages], rest[2 * n_head_pages:]
    rows = q_ref.shape[1]
    q = q_ref[0]
    tri = _newer_tri(PAGE_SIZE)
    z = lax.dot_general(q, kn_ref[0], (((1,), (1,)), ((), ())), preferred_element_type=F32)
    t = lax.broadcasted_iota(I32, (rows, PAGE_SIZE), 0) % n_tok
    mask = lax.broadcasted_iota(I32, (rows, PAGE_SIZE), 1) < t
    a, surv = _stick_break_tile(z, mask, jnp.zeros((rows, 1), F32), tri)
    o_ref[0] = jnp.dot(a.astype(BF16), vn_ref[0], preferred_element_type=F32)
    surv_ref[0] = jnp.broadcast_to(surv, surv_ref.shape[1:])
    for i in range(n_head_pages):
        _sb_page_step(q, pages[2 * i], pages[2 * i + 1], o_ref, surv_ref, tri)


def _sb_sample_tail_kernel(pt_ref, q_ref, acc_ref, sin_ref, kc_ref, vc_ref, o_ref, surv_ref):
    @pl.when(pl.program_id(1) == 0)
    def _():
        o_ref[...] = acc_ref[...]
        surv_ref[...] = sin_ref[...]
    _sb_page_step(q_ref[0], kc_ref, vc_ref, o_ref, surv_ref, _newer_tri(PAGE_SIZE))


def _block_diag_queries(q, n_heads_q, n_heads_kv):
    group = n_heads_q // n_heads_kv
    blocks = []
    for h in range(n_heads_q):
        kv = h // group
        qh = q[:, :, h * HEAD_DIM:(h + 1) * HEAD_DIM]
        blocks.append(jnp.pad(qh, ((0, 0), (0, 0), (kv * HEAD_DIM, (n_heads_kv - 1 - kv) * HEAD_DIM))))
    return jnp.concatenate(blocks, axis=1)


def _take_block_diag(acc, n_tok, n_heads_q, n_heads_kv):
    group = n_heads_q // n_heads_kv
    return jnp.concatenate(
        [acc[:, h * n_tok:(h + 1) * n_tok, (h // group) * HEAD_DIM:(h // group + 1) * HEAD_DIM]
         for h in range(n_heads_q)], axis=2)


def _pages_t(cache):
    n_phys, page = cache.shape[:2]
    return jnp.moveaxis(cache, 1, -1).reshape(n_phys, -1, page)


def _pad_rows(a, rows):
    return jnp.pad(a, ((0, 0), (0, rows - a.shape[1]), (0, 0)))


def _sb_sample(q, k, v, cache_k, cache_v, page_table):
    b, n_tok, _ = q.shape
    n_pages = page_table.shape[1]
    rows = SB_HEADS * n_tok
    qbd = _block_diag_queries(q, SB_HEADS, SB_HEADS)
    kn, vn = _pad_rows(k, PAGE_SIZE), _pad_rows(v, PAGE_SIZE)
    n_head = min(SB_SAMPLE_HEAD_PAGES, n_pages)
    out_shape = [jax.ShapeDtypeStruct((b, rows, SB_WIDTH), F32),
                 jax.ShapeDtypeStruct((b, rows, LANES), F32)]

    per_b1 = lambda bi, pt: (bi, 0, 0)
    page_specs = []
    for i in range(n_head):
        spec = pl.BlockSpec((1, SB_WIDTH, PAGE_SIZE), lambda bi, pt, i=i: (pt[bi, n_pages - 1 - i], 0, 0))
        page_specs += [spec, spec]
    acc, surv = pl.pallas_call(
        functools.partial(_sb_sample_head_kernel, n_tok=n_tok, n_head_pages=n_head),
        grid_spec=pltpu.PrefetchScalarGridSpec(
            num_scalar_prefetch=1,
            grid=(b,),
            in_specs=[pl.BlockSpec((1, rows, SB_WIDTH), per_b1),
                      pl.BlockSpec((1, PAGE_SIZE, SB_WIDTH), per_b1),
                      pl.BlockSpec((1, PAGE_SIZE, SB_WIDTH), per_b1)] + page_specs,
            out_specs=[pl.BlockSpec((1, rows, SB_WIDTH), per_b1), pl.BlockSpec((1, rows, LANES), per_b1)]),
        out_shape=out_shape,
        compiler_params=_cparams(("parallel",)),
        name="sb_sample_head",
    )(page_table, qbd, kn, vn, *([cache_k, cache_v] * n_head))

    n_tail = n_pages - n_head
    if n_tail > 0:
        per_b = lambda bi, j, pt: (bi, 0, 0)
        page = lambda bi, j, pt: (pt[bi, n_tail - 1 - j], 0, 0)

        def older_pages(acc, surv):
            return pl.pallas_call(
                _sb_sample_tail_kernel,
                grid_spec=pltpu.PrefetchScalarGridSpec(
                    num_scalar_prefetch=1,
                    grid=(b, n_tail),
                    in_specs=[pl.BlockSpec((1, rows, SB_WIDTH), per_b),
                              pl.BlockSpec((1, rows, SB_WIDTH), per_b),
                              pl.BlockSpec((1, rows, LANES), per_b),
                              pl.BlockSpec((1, SB_WIDTH, PAGE_SIZE), page),
                              pl.BlockSpec((1, SB_WIDTH, PAGE_SIZE), page)],
                    out_specs=[pl.BlockSpec((1, rows, SB_WIDTH), per_b),
                               pl.BlockSpec((1, rows, LANES), per_b)]),
                out_shape=out_shape,
                compiler_params=_cparams(("parallel", "arbitrary")),
                name="sb_sample_tail",
            )(page_table, qbd, acc, surv, cache_k, cache_v)[0]

        acc = lax.cond(jnp.max(surv) > SURV_FLOOR, older_pages, lambda acc, surv: acc, acc, surv)
    return _take_block_diag(acc, n_tok, SB_HEADS, SB_HEADS)


def _dsa_sel_kernel(pt_ref, iq_ref, iw_ref, ikn_ref, *rest, n_tok, n_pages, group, k_top, n_index_bits):
    pages, (sel_ref, keys_ref) = rest[:group], rest[group:]
    j = pl.program_id(1)
    width = group * PAGE_SIZE

    def scores(s):
        s = jnp.maximum(s, 0.0) * iw_ref[0]
        return jnp.sum(s.reshape(IDX_HEADS, n_tok, s.shape[-1]), axis=0)

    kt = jnp.concatenate([p_ref[0].astype(BF16) for p_ref in pages], axis=1)
    key = _sortable(scores(jnp.dot(iq_ref[0], kt, preferred_element_type=F32)))
    keys_ref[:, pl.ds(pl.multiple_of(j * width, width), width)] = key

    @pl.when(j == n_pages // group - 1)
    def _():
        t = lax.broadcasted_iota(I32, (n_tok, PAGE_SIZE), 0)
        col = lax.broadcasted_iota(I32, (n_tok, PAGE_SIZE), 1)
        s_new = lax.dot_general(iq_ref[0], ikn_ref[0], (((1,), (1,)), ((), ())),
                                preferred_element_type=F32)
        key_new = jnp.where(col <= t, _sortable(scores(s_new)), jnp.int32(INT_MIN))
        keys_ref[:, n_pages * PAGE_SIZE:] = key_new
        t_abs = n_pages * PAGE_SIZE + lax.broadcasted_iota(I32, (n_tok, 1), 0)
        k_row = jnp.minimum(t_abs + 1, k_top)
        thr = _select_topk(keys_ref, n_pages + 1, PAGE_SIZE, k_row, n_index_bits)
        sel_ref[0] = (keys_ref[...] >= thr).astype(F32)


def _dsa_sample_kernel(pt_ref, q_ref, sel_ref, seln_ref, kn_ref, vn_ref, *rest, n_tok, n_pages, group):
    k_pages, v_pages, (o_ref, m_ref, l_ref) = rest[:group], rest[group:2 * group], rest[2 * group:]
    j = pl.program_id(1)
    rows = SA_HEADS * n_tok
    nt_dims = (((1,), (1,)), ((), ()))

    @pl.when(j == 0)
    def _():
        m_ref[...] = jnp.full_like(m_ref, NEG_BIG)
        l_ref[...] = jnp.zeros_like(l_ref)
        o_ref[...] = jnp.zeros_like(o_ref)

    def step(s, sel, pv):
        n = s.shape[-1]
        sel = jnp.broadcast_to(sel[None], (SA_HEADS, n_tok, n)).reshape(rows, n)
        s = jnp.where(sel > 0.5, s, NEG_BIG)
        m_old = m_ref[...]
        m_new = jnp.maximum(m_old, jnp.max(s, axis=1, keepdims=True))
        alpha = jnp.exp2(m_old - m_new)
        pr = jnp.exp2(s - m_new)
        l_ref[...] = alpha * l_ref[...] + jnp.sum(pr, axis=1, keepdims=True)
        o_ref[0] = alpha * o_ref[0] + pv(pr.astype(BF16))
        m_ref[...] = m_new

    kt = jnp.concatenate([p_ref[0].astype(BF16) for p_ref in k_pages], axis=1)
    vt = jnp.concatenate([p_ref[0].astype(BF16) for p_ref in v_pages], axis=1)
    step(jnp.dot(q_ref[0], kt, preferred_element_type=F32), sel_ref[0],
         lambda pr: lax.dot_general(pr, vt, nt_dims, preferred_element_type=F32))

    @pl.when(j == n_pages // group - 1)
    def _():
        step(lax.dot_general(q_ref[0], kn_ref[0], nt_dims, preferred_element_type=F32), seln_ref[0],
             lambda pr: jnp.dot(pr, vn_ref[0], preferred_element_type=F32))
        o_ref[0] = o_ref[0] / l_ref[...]


def _dsa_sample(aq, ak, av, iq, iw, ik, cache_k, cache_v, cache_ik, page_table):
    b, n_tok, _ = aq.shape
    n_pages = page_table.shape[1]
    n_keys = (n_pages + 1) * PAGE_SIZE
    k_top = min(TOP_K_MAX, (n_pages * PAGE_SIZE + n_tok) // 4)
    rows = SA_HEADS * n_tok
    per_b = lambda bi, j, pt: (bi, 0, 0)

    def page_specs(width, group):
        return [pl.BlockSpec((1, width, PAGE_SIZE), lambda bi, j, pt, i=i: (pt[bi, j * group + i], 0, 0))
                for i in range(group)]

    g_sel = _largest_divisor(n_pages, DSA_SEL_GROUP)
    iq_rows = iq.reshape(b, n_tok, IDX_HEADS, IDX_DIM).transpose(0, 2, 1, 3).reshape(b, rows, IDX_DIM)
    iw_rows = iw.transpose(0, 2, 1).reshape(b, rows, 1)
    sel = pl.pallas_call(
        functools.partial(_dsa_sel_kernel, n_tok=n_tok, n_pages=n_pages, group=g_sel, k_top=k_top,
                          n_index_bits=max(1, n_keys.bit_length())),
        grid_spec=pltpu.PrefetchScalarGridSpec(
            num_scalar_prefetch=1,
            grid=(b, n_pages // g_sel),
            in_specs=[pl.BlockSpec((1, rows, IDX_DIM), per_b),
                      pl.BlockSpec((1, rows, 1), per_b),
                      pl.BlockSpec((1, PAGE_SIZE, IDX_DIM), per_b)] + page_specs(IDX_DIM, g_sel),
            out_specs=pl.BlockSpec((1, n_tok, n_keys), per_b),
            scratch_shapes=[pltpu.VMEM((n_tok, n_keys), I32)]),
        out_shape=jax.ShapeDtypeStruct((b, n_tok, n_keys), F32),
        compiler_params=_cparams(("parallel", "arbitrary")),
        name="dsa_sel",
    )(page_table, iq_rows, iw_rows, _pad_rows(ik, PAGE_SIZE), *([cache_ik] * g_sel))

    g_att = _largest_divisor(n_pages, DSA_SAMPLE_GROUP)
    qbd = _block_diag_queries(aq, SA_HEADS, SA_KV_HEADS)
    acc = pl.pallas_call(
        functools.partial(_dsa_sample_kernel, n_tok=n_tok, n_pages=n_pages, group=g_att),
        grid_spec=pltpu.PrefetchScalarGridSpec(
            num_scalar_prefetch=1,
            grid=(b, n_pages // g_att),
            in_specs=[pl.BlockSpec((1, rows, SA_KV_WIDTH), per_b),
                      pl.BlockSpec((1, n_tok, g_att * PAGE_SIZE), lambda bi, j, pt: (bi, 0, j)),
                      pl.BlockSpec((1, n_tok, PAGE_SIZE), lambda bi, j, pt: (bi, 0, n_pages)),
                      pl.BlockSpec((1, PAGE_SIZE, SA_KV_WIDTH), per_b),
                      pl.BlockSpec((1, PAGE_SIZE, SA_KV_WIDTH), per_b)]
                     + page_specs(SA_KV_WIDTH, g_att) + page_specs(SA_KV_WIDTH, g_att),
            out_specs=pl.BlockSpec((1, rows, SA_KV_WIDTH), per_b),
            scratch_shapes=[pltpu.VMEM((rows, 1), F32), pltpu.VMEM((rows, 1), F32)]),
        out_shape=jax.ShapeDtypeStruct((b, rows, SA_KV_WIDTH), F32),
        compiler_params=_cparams(("parallel", "arbitrary")),
        name="dsa_sample",
    )(page_table, qbd, sel, sel, _pad_rows(ak, PAGE_SIZE), _pad_rows(av, PAGE_SIZE),
      *([cache_k] * g_att), *([cache_v] * g_att))
    return _take_block_diag(acc, n_tok, SA_HEADS, SA_KV_HEADS)


def _tail_kernel(x_ref, osb_ref, osa_ref, gate_ref, wsb_ref, wsa_ref, wo_ref, g2_ref, wup_ref,
                 wdn_ref, y_ref, *, ff_chunk):
    gate = gate_ref[...]
    merged = (gate[:, :D_MODEL] * jnp.dot(osb_ref[...], wsb_ref[...], preferred_element_type=F32)
              + gate[:, D_MODEL:] * jnp.dot(osa_ref[...], wsa_ref[...], preferred_element_type=F32))
    x1 = x_ref[...] + jnp.dot(merged.astype(BF16), wo_ref[...], preferred_element_type=F32)
    ms = jnp.mean(x1 * x1, axis=-1, keepdims=True)
    h = (x1 * lax.rsqrt(ms + RMS_EPS) * g2_ref[...]).astype(BF16)
    y = x1
    for c in range(D_FF // ff_chunk):
        sl = slice(c * ff_chunk, (c + 1) * ff_chunk)
        u = jnp.maximum(jnp.dot(h, wup_ref[:, sl], preferred_element_type=F32), 0.0)
        y = y + jnp.dot((u * u).astype(BF16), wdn_ref[sl, :], preferred_element_type=F32)
    y_ref[...] = y


def _tail(x2d, o_sb, o_sa, gates, w_sb_out, w_sa_out, w_o, g2, w_up, w_down, tm):
    r = x2d.shape[0]
    row = lambda w: pl.BlockSpec((tm, w), lambda i: (i, 0))
    res = lambda shape: pl.BlockSpec(shape, lambda i: (0, 0), pipeline_mode=pl.Buffered(1))
    return pl.pallas_call(
        functools.partial(_tail_kernel, ff_chunk=1024),
        grid=(r // tm,),
        in_specs=[row(D_MODEL), row(SB_WIDTH), row(SA_WIDTH), row(2 * D_MODEL),
                  res((SB_WIDTH, D_MODEL)), res((SA_WIDTH, D_MODEL)), res((D_MODEL, D_MODEL)),
                  res((1, D_MODEL)), res((D_MODEL, D_FF)), res((D_FF, D_MODEL))],
        out_specs=row(D_MODEL),
        out_shape=jax.ShapeDtypeStruct((r, D_MODEL), F32),
        compiler_params=_cparams(("parallel",)),
        name="tail",
    )(x2d, o_sb, o_sa, gates, w_sb_out, w_sa_out, w_o, g2.reshape(1, D_MODEL), w_up, w_down)


def _unpermute_heads(a):
    shp = a.shape
    a = a.reshape(shp[:-1] + (SA_HEADS, HEAD_DIM))
    return a[..., jnp.array(SA_Q_PERM), :].reshape(shp)


def kernel(x_prompt, x_sample, cache_sb_k, cache_sb_v, cache_sa_k, cache_sa_v, cache_idx_k, page_table,
           norm1_g, w_in, b_gate, q_norm_g, k_norm_g, idx_k_norm_g, w_sb_out, w_sa_out, w_o,
           norm2_g, w_up, w_down):
    depth = w_in.shape[0]
    batch, seq, _ = x_prompt.shape
    dec_batch, dec_seq, _ = x_sample.shape
    n_phys = cache_sb_k.shape[1]
    past_len = page_table.shape[1] * PAGE_SIZE
    assert seq % 512 == 0 and cache_sb_k.shape[2] == PAGE_SIZE
    pos_p = jnp.arange(seq, dtype=F32)
    pos_s = past_len + jnp.arange(dec_seq, dtype=F32)
    tm_p = 256
    tm_s = 256 if (dec_batch * dec_seq) % 256 == 0 else dec_batch * dec_seq

    xp = x_prompt.reshape(batch * seq, D_MODEL)
    xs = x_sample.reshape(dec_batch * dec_seq, D_MODEL)
    new_p = [[] for _ in range(5)]
    new_s = [[] for _ in range(5)]
    for layer in range(depth):
        w_packed = _pack_w_in(w_in[layer])
        w_sa_perm = _unpermute_heads(w_sa_out[layer].T).T.astype(BF16)
        tail_w = (w_sb_out[layer].astype(BF16), w_sa_perm, w_o[layer].astype(BF16), norm2_g[layer],
                  w_up[layer].astype(BF16), w_down[layer].astype(BF16))
        norm_args = (norm1_g[layer], q_norm_g[layer], k_norm_g[layer], idx_k_norm_g[layer], b_gate[layer])

        (skf, svf, akf, avf, ikf, sbq, sbk, sbv, saq, sak, sav, iq, ik2, gates, avt, iwt) = _project(
            xp, pos_p, batch, w_packed, *norm_args, tm=tm_p)
        o_sb = _sb_prompt(sbq, sbk, sbv, batch, seq, tq=256)
        o_sa = _dsa_prompt(iq, iwt, saq, ik2, sak, avt, batch, seq, tq=128, tk=512, tka=512)
        xp = _tail(xp, o_sb, o_sa, gates, *tail_w, tm=256)
        for lst, arr, hh in zip(new_p, (skf, svf, akf, avf, ikf), (SB_HEADS, SB_HEADS, SA_KV_HEADS, SA_KV_HEADS, 0)):
            lst.append(arr.reshape((batch, seq, hh, HEAD_DIM) if hh else (batch, seq, IDX_DIM)))

        (skf, svf, akf, avf, ikf, sbq, sbk, sbv, saq, sak, sav, iq, ik2, gates, avt, iwt) = _project(
            xs, pos_s, dec_batch, w_packed, *norm_args, tm=tm_s)
        b3 = lambda a: a.reshape(dec_batch, dec_seq, a.shape[-1])
        iw = iwt.T
        o_sb = _sb_sample(b3(sbq), b3(sbk), b3(sbv),
                          _pages_t(cache_sb_k[layer]), _pages_t(cache_sb_v[layer]), page_table)
        o_sa = _dsa_sample(_unpermute_heads(b3(saq)), b3(sak), b3(sav), b3(iq), b3(iw),
                           b3(ik2)[..., :IDX_DIM],
                           _pages_t(cache_sa_k[layer]), _pages_t(cache_sa_v[layer]),
                           _pages_t(cache_idx_k[layer]), page_table)
        o_sb = o_sb.reshape(dec_batch * dec_seq, SB_WIDTH).astype(BF16)
        o_sa = _unpermute_heads(o_sa).reshape(dec_batch * dec_seq, SA_WIDTH).astype(BF16)
        xs = _tail(xs, o_sb, o_sa, gates, *tail_w, tm=tm_s)
        for lst, arr, hh in zip(new_s, (skf, svf, akf, avf, ikf), (SB_HEADS, SB_HEADS, SA_KV_HEADS, SA_KV_HEADS, 0)):
            lst.append(arr.reshape((dec_batch, dec_seq, hh, HEAD_DIM) if hh else (dec_batch, dec_seq, IDX_DIM)))

    return (xp.reshape(batch, seq, D_MODEL), xs.reshape(dec_batch, dec_seq, D_MODEL),
            *(jnp.stack(l) for l in new_p), *(jnp.stack(l) for l in new_s))
```

```python
import functools

import jax
import jax.numpy as jnp
from jax import lax
from jax.experimental import pallas as pl
from jax.experimental.pallas import tpu as pltpu

F32 = jnp.float32
BF16 = jnp.bfloat16
I32 = jnp.int32

D_MODEL = 1024
HEAD_DIM = 64
SB_HEADS = 8
SA_HEADS = 8
SA_KV_HEADS = 4
IDX_HEADS = 8
IDX_DIM = 64
SB_WIDTH = SB_HEADS * HEAD_DIM
SA_WIDTH = SA_HEADS * HEAD_DIM
SA_KV_WIDTH = SA_KV_HEADS * HEAD_DIM
IDX_WIDTH = IDX_HEADS * IDX_DIM
TOP_K_MAX = 256
ROT_DIM = HEAD_DIM // 4
ROPE_THETA = 500000.0
D_FF = 4 * D_MODEL
PAGE_SIZE = 128
RMS_EPS = 1e-6
QK_SCALE = HEAD_DIM ** -0.5
SA_Q_SCALE = QK_SCALE * 1.4426950408889634
IDX_SCALE = IDX_DIM ** -0.5
IDX_W_SCALE = IDX_HEADS ** -0.5

LANES = 128
VMEM_LIMIT = 56 * 1024 * 1024

C_SQ, C_SK, C_SV = 0, 512, 1024
C_AQ, C_AK, C_AV = 1536, 2048, 2304
C_IQ, C_IK2, C_IW = 2560, 3072, 3200
C_GATE = 3328
C_END = C_GATE + 2 * D_MODEL
SA_Q_PERM = (0, 2, 1, 3, 4, 6, 5, 7)

SURV_FLOOR = -105.0
NEG_BIG = -1e30
INT_MIN = -2 ** 31

SB_SAMPLE_HEAD_PAGES = 4
DSA_SEL_GROUP = 32
DSA_SAMPLE_GROUP = 16


TOPK_GROUPS = 256


def _largest_divisor(n, cap):
    return max(d for d in range(1, min(n, cap) + 1) if n % d == 0)


def _cparams(sem, vmem=VMEM_LIMIT):
    return pltpu.CompilerParams(dimension_semantics=sem, vmem_limit_bytes=vmem)


def _rope_chunk(y, cos, sin_a, sin_b):
    up = pltpu.roll(y, LANES - ROT_DIM // 2, 1)
    dn = pltpu.roll(y, ROT_DIM // 2, 1)
    return y * cos + up * sin_a + dn * sin_b


def _head_rms_chunk(y, e_mat, gain):
    y2 = y * y
    hi = y2.astype(BF16)
    lo = (y2 - hi.astype(F32)).astype(BF16)
    ms = (jnp.dot(hi, e_mat, preferred_element_type=F32)
          + jnp.dot(lo, e_mat, preferred_element_type=F32))
    return y * lax.rsqrt(ms + RMS_EPS) * gain


def _proj_kernel(x_ref, g1_ref, w_ref, qg_ref, kg_ref, ig_ref, bg_ref, cos_ref, sa_ref, sb_ref,
                 e_ref,
                 skt_ref, svt_ref, akt_ref, avt_ref, ikt_ref, avtb_ref, iwt_ref,
                 sbq_ref, sbk_ref, sbv_ref, saq_ref, sak_ref, sav_ref, iq_ref, ik2_ref, gate_ref):
    x = x_ref[...]
    ms = jnp.mean(x * x, axis=-1, keepdims=True)
    h = (x * lax.rsqrt(ms + RMS_EPS) * g1_ref[...]).astype(BF16)

    def mm(lo, hi):
        return jnp.dot(h, w_ref[:, lo:hi], preferred_element_type=F32)

    cos, sin_a, sin_b = cos_ref[...], sa_ref[...], sb_ref[...]
    e_mat = e_ref[...]

    p = mm(C_SQ, C_SK)
    sbq_ref[...] = (p * QK_SCALE).astype(BF16)
    p = mm(C_SK, C_SV)
    skt_ref[0] = p.T
    sbk_ref[...] = p.astype(BF16)
    p = mm(C_SV, C_AQ)
    svt_ref[0] = p.T
    sbv_ref[...] = p.astype(BF16)

    p = mm(C_AQ, C_AK)
    for c in range(SA_WIDTH // LANES):
        sl = slice(c * LANES, (c + 1) * LANES)
        y = _rope_chunk(_head_rms_chunk(p[:, sl], e_mat, qg_ref[...]), cos, sin_a, sin_b)
        saq_ref[:, sl] = (y * SA_Q_SCALE).astype(BF16)
    p = mm(C_AK, C_AV)
    for c in range(SA_KV_WIDTH // LANES):
        sl = slice(c * LANES, (c + 1) * LANES)
        y = _rope_chunk(_head_rms_chunk(p[:, sl], e_mat, kg_ref[...]), cos, sin_a, sin_b)
        akt_ref[0, sl, :] = y.T
        sak_ref[:, sl] = y.astype(BF16)
    p = mm(C_AV, C_IQ)
    pt = p.T
    avt_ref[0] = pt
    avtb_ref[0] = pt.astype(BF16)
    sav_ref[...] = p.astype(BF16)

    p = mm(C_IQ, C_IK2)
    for c in range(IDX_WIDTH // LANES):
        sl = slice(c * LANES, (c + 1) * LANES)
        iq_ref[:, sl] = (_rope_chunk(p[:, sl], cos, sin_a, sin_b) * IDX_SCALE).astype(BF16)
    p = mm(C_IK2, C_GATE)
    y = _rope_chunk(_head_rms_chunk(p[:, :LANES], e_mat, ig_ref[...]), cos, sin_a, sin_b)
    ikt_ref[0] = y.T[:IDX_DIM]
    ik2_ref[...] = y.astype(BF16)
    iwt_ref[0] = p[:, LANES:].T[:IDX_HEADS] * IDX_W_SCALE

    p = mm(C_GATE, C_END) + bg_ref[...]
    gate_ref[...] = jax.nn.sigmoid(p)


def _rope_tables(pos):
    half = ROT_DIM // 2
    inv = 1.0 / (ROPE_THETA ** (jnp.arange(0, ROT_DIM, 2, dtype=F32) / ROT_DIM))
    ang = pos[:, None] * inv[None, :]
    cos, sin = jnp.cos(ang), jnp.sin(ang)
    t = pos.shape[0]
    zeros = jnp.zeros((t, HEAD_DIM - ROT_DIM), F32)
    zh = jnp.zeros((t, half), F32)
    cos_h = jnp.concatenate([cos, cos, zeros + 1.0], axis=1)
    sa_h = jnp.concatenate([-sin, zh, zeros], axis=1)
    sb_h = jnp.concatenate([zh, sin, zeros], axis=1)
    return tuple(jnp.concatenate([a, a], axis=1) for a in (cos_h, sa_h, sb_h))


def _pack_w_in(w_in):
    o = 0
    parts = []
    for n in (SB_WIDTH, SB_WIDTH, SB_WIDTH, SA_WIDTH, SA_KV_WIDTH, SA_KV_WIDTH, IDX_WIDTH,
              IDX_DIM, IDX_HEADS, 2 * D_MODEL):
        parts.append(w_in[:, o:o + n])
        o += n
    sq, sk, sv, aq, ak, av, iq, ik, iw, gates = parts
    aq = aq.reshape(D_MODEL, SA_HEADS, HEAD_DIM)[:, jnp.array(SA_Q_PERM)].reshape(D_MODEL, SA_WIDTH)
    pad = jnp.zeros((D_MODEL, C_GATE - C_IW - IDX_HEADS), w_in.dtype)
    return jnp.concatenate([sq, sk, sv, aq, ak, av, iq, ik, ik, iw, pad, gates], axis=1).astype(BF16)


def _project(x2d, pos, n_rep, w_packed, g1, qg, kg, ig, bg, tm, group_cols):
    r = x2d.shape[0]
    t = pos.shape[0]
    assert r == n_rep * t and r % tm == 0 and (t % tm == 0 or tm % t == 0)
    assert r % group_cols == 0 and group_cols % tm == 0
    ntc = group_cols // tm
    cos, sin_a, sin_b = _rope_tables(pos)
    if tm > t:
        cos, sin_a, sin_b = (jnp.tile(a, (tm // t, 1)) for a in (cos, sin_a, sin_b))
    nt = cos.shape[0] // tm
    e_mat = (jnp.arange(LANES)[:, None] // HEAD_DIM == jnp.arange(LANES)[None, :] // HEAD_DIM)
    e_mat = (e_mat.astype(F32) / HEAD_DIM).astype(BF16)

    def row(width):
        return pl.BlockSpec((tm, width), lambda i: (i, 0))

    def const(shape):
        return pl.BlockSpec(shape, lambda i: (0, 0))

    tab = pl.BlockSpec((tm, LANES), lambda i: (i % nt, 0))
    tile2 = lambda g: jnp.concatenate([g, g]).reshape(1, LANES)
    out_heights = [(SB_WIDTH, F32), (SB_WIDTH, F32), (SA_KV_WIDTH, F32), (SA_KV_WIDTH, F32),
                   (IDX_DIM, F32), (SA_KV_WIDTH, BF16), (IDX_HEADS, F32)]
    out_widths = [(SB_WIDTH, BF16), (SB_WIDTH, BF16), (SB_WIDTH, BF16),
                  (SA_WIDTH, BF16), (SA_KV_WIDTH, BF16), (SA_KV_WIDTH, BF16),
                  (IDX_WIDTH, BF16), (LANES, BF16), (2 * D_MODEL, F32)]
    return pl.pallas_call(
        _proj_kernel,
        grid=(r // tm,),
        in_specs=[row(D_MODEL), const((1, D_MODEL)),
                  pl.BlockSpec((D_MODEL, C_END), lambda i: (0, 0), pipeline_mode=pl.Buffered(1)),
                  const((1, LANES)), const((1, LANES)), const((1, LANES)), const((1, 2 * D_MODEL)),
                  tab, tab, tab, const((LANES, LANES))],
        out_specs=[pl.BlockSpec((1, h_, tm), lambda i: (i // ntc, 0, i % ntc)) for h_, _ in out_heights]
                  + [row(w_) for w_, _ in out_widths],
        out_shape=[jax.ShapeDtypeStruct((r // group_cols, h_, group_cols), dt) for h_, dt in out_heights]
                  + [jax.ShapeDtypeStruct((r, w_), dt) for w_, dt in out_widths],
        compiler_params=_cparams(("parallel",)),
        name="proj",
    )(x2d, g1.reshape(1, D_MODEL), w_packed, tile2(qg), tile2(kg), tile2(ig),
      bg.reshape(1, 2 * D_MODEL), cos, sin_a, sin_b, e_mat)


def _stick_break_tile(z, mask, surv, tri):
    t = jnp.log1p(jnp.exp(-jnp.abs(z)))
    l = -(jnp.maximum(z, 0.0) + t)
    if mask is not None:
        l = jnp.where(mask, l, 0.0)
    ls = jnp.minimum(z, 0.0) - t
    hi = l.astype(BF16)
    lo = (l - hi.astype(F32)).astype(BF16)
    ex = jnp.dot(hi, tri, preferred_element_type=F32) + jnp.dot(lo, tri, preferred_element_type=F32)
    a = jnp.exp(ls + ex + surv)
    if mask is not None:
        a = jnp.where(mask, a, 0.0)
    return a, surv + jnp.sum(l, axis=1, keepdims=True)


def _newer_tri(n):
    j = lax.broadcasted_iota(I32, (n, n), 0)
    s = lax.broadcasted_iota(I32, (n, n), 1)
    return (j > s).astype(BF16)


def _sb_prompt_kernel(q_ref, k_ref, v_ref, o_ref, acc_ref, surv_ref, *, tq):
    i = pl.program_id(2)
    t0 = i * tq
    lane = lax.broadcasted_iota(I32, (tq, LANES), 1)
    row_t = t0 + lax.broadcasted_iota(I32, (tq, tq), 0)
    col0 = lax.broadcasted_iota(I32, (tq, tq), 1)
    tri = _newer_tri(tq)
    q = q_ref[...]
    heads = range(2)
    qm = [jnp.where((lane >= HEAD_DIM) == bool(h), q, jnp.zeros_like(q)) for h in heads]
    acc_ref[...] = jnp.zeros_like(acc_ref)
    surv_ref[...] = jnp.zeros_like(surv_ref)

    def body(carry):
        kt, _ = carry
        start = pl.multiple_of(kt * tq, tq)
        kb = k_ref[pl.ds(start, tq), :]
        vb = v_ref[pl.ds(start, tq), :]
        mask = (start + col0) < row_t
        z = [lax.dot_general(qm[h], kb, (((1,), (1,)), ((), ())), preferred_element_type=F32)
             for h in heads]
        l, ls, ex = [], [], []
        for h in heads:
            t = jnp.log1p(jnp.exp(-jnp.abs(z[h])))
            l.append(jnp.where(mask, -(jnp.maximum(z[h], 0.0) + t), 0.0))
            ls.append(jnp.minimum(z[h], 0.0) - t)
            hi = l[h].astype(BF16)
            lo = (l[h] - hi.astype(F32)).astype(BF16)
            ex.append(jnp.dot(hi, tri, preferred_element_type=F32)
                      + jnp.dot(lo, tri, preferred_element_type=F32))
        acc, surv = [], []
        for h in heads:
            a = jnp.where(mask, jnp.exp(ls[h] + ex[h] + surv_ref[h]), 0.0)
            acc.append(acc_ref[h] + jnp.dot(a.astype(BF16), vb, preferred_element_type=F32))
            surv.append(surv_ref[h] + jnp.sum(l[h], axis=1, keepdims=True))
        acc_ref[...] = jnp.stack(acc)
        surv_ref[...] = jnp.stack(surv)
        return kt - 1, jnp.maximum(jnp.max(surv[0]), jnp.max(surv[1]))

    lax.while_loop(lambda c: (c[0] >= 0) & (c[1] > SURV_FLOOR), body, (i, jnp.float32(0.0)))
    o_ref[...] = jnp.where(lane < HEAD_DIM, acc_ref[0], acc_ref[1]).astype(o_ref.dtype)


def _sb_prompt(q, k, v, batch, seq, tq):
    nq = seq // tq
    n_chunks = SB_WIDTH // LANES
    qspec = pl.BlockSpec((tq, LANES), lambda b, c, i: (b * nq + i, c))
    kvspec = pl.BlockSpec((seq, LANES), lambda b, c, i: (b, c))
    return pl.pallas_call(
        functools.partial(_sb_prompt_kernel, tq=tq),
        grid=(batch, n_chunks, nq),
        in_specs=[qspec, kvspec, kvspec],
        out_specs=qspec,
        out_shape=jax.ShapeDtypeStruct((batch * seq, SB_WIDTH), BF16),
        scratch_shapes=[pltpu.VMEM((2, tq, LANES), F32), pltpu.VMEM((2, tq, 1), F32)],
        compiler_params=_cparams(("parallel", "parallel", "arbitrary")),
        name="sb_prompt",
    )(q, k, v)


def _sortable(x):
    bits = lax.bitcast_convert_type(x, I32)
    key = jnp.where(bits < 0, bits ^ jnp.int32(0x7FFFFFFF), bits)
    return jnp.where(x == 0.0, jnp.int32(0), key)


def _select_topk(keys_ref, n_tiles, tk, k_row, n_index_bits, key_axis=1):
    qn = keys_ref.shape[1 - key_axis]
    sign = jnp.int32(INT_MIN)
    tile_shape = (qn, tk) if key_axis == 1 else (tk, qn)
    fold = LANES if key_axis == 1 else 8
    vec = (qn, 1) if key_axis == 1 else (1, qn)

    def tile_at(kt):
        start = pl.multiple_of(kt * tk, tk)
        idx = (slice(None), pl.ds(start, tk)) if key_axis == 1 else (pl.ds(start, tk), slice(None))
        return start, idx

    def count(pred):
        def tile_body(kt, acc):
            start, idx = tile_at(kt)
            m = pred(keys_ref[idx], start).astype(I32)
            if key_axis == 1:
                for c in range(tk // fold):
                    acc = acc + m[:, c * fold:(c + 1) * fold]
                return acc
            return acc + jnp.sum(m.reshape(tk // fold, fold, qn), axis=0)
        acc0 = jnp.zeros((qn, fold) if key_axis == 1 else (fold, qn), I32)
        acc = lax.fori_loop(0, n_tiles, tile_body, acc0)
        return jnp.sum(acc.astype(F32), axis=key_axis, keepdims=True).astype(I32)

    bracket = None
    if key_axis == 0 and tk % TOPK_GROUPS == 0:
        def gmax_body(kt, acc):
            _, idx = tile_at(kt)
            return jnp.maximum(acc, jnp.max(keys_ref[idx].reshape(tk // TOPK_GROUPS, TOPK_GROUPS, qn), axis=0))
        gmax = lax.fori_loop(0, n_tiles, gmax_body, jnp.full((TOPK_GROUPS, qn), INT_MIN, I32))
        gmax = gmax.reshape(TOPK_GROUPS // fold, fold, qn)
        hi_b, lo_b = jnp.max(gmax, axis=0), jnp.min(gmax, axis=0)
        for shift in (4, 2, 1):
            hi_b = jnp.maximum(hi_b, pltpu.roll(hi_b, shift, 0))
            lo_b = jnp.minimum(lo_b, pltpu.roll(lo_b, shift, 0))
        bracket = (lo_b[:1], hi_b[:1])

    def bit_body(carry):
        b, prefix, thr, done, _ = carry
        cand_u = prefix | lax.shift_left(jnp.int32(1), b)
        cand = cand_u ^ sign
        if bracket is None:
            cnt = count(lambda kk, start: kk >= cand)
        else:
            take = cand <= bracket[0]
            decided = take | (cand > bracket[1]) | (done != 0)
            cnt = lax.cond(jnp.min(decided.astype(F32)) > 0.5,
                           lambda: jnp.where(take, k_row + 1, 0),
                           lambda: count(lambda kk, start: kk >= cand))
        live = done == 0
        prefix = jnp.where(live & (cnt >= k_row), cand_u, prefix)
        hit = live & (cnt == k_row)
        thr = jnp.where(hit, cand, thr)
        done = jnp.where(hit, 1, done)
        return b - 1, prefix, thr, done, jnp.min(done.astype(F32)).astype(I32)

    zeros = jnp.zeros(vec, I32)
    _, prefix, thr, done, all_done = lax.while_loop(
        lambda c: (c[0] >= 0) & (c[4] == 0), bit_body,
        (jnp.int32(31), zeros, zeros, zeros, jnp.int32(0)))
    thr = jnp.where(done == 0, prefix ^ sign, thr)

    @pl.when(all_done == 0)
    def _():
        tied = done == 0
        want = k_row - count(lambda kk, start: kk > thr)
        col0 = lax.broadcasted_iota(I32, tile_shape, key_axis)

        def ties_before(x):
            return count(lambda kk, start: (kk == thr) & ((start + col0) < x))

        def idx_body(j, x):
            cand = x | lax.shift_left(jnp.int32(1), n_index_bits - 1 - j)
            return jnp.where(ties_before(cand) < want, cand, x)

        x = lax.fori_loop(0, n_index_bits, idx_body, zeros)

        def fix_body(kt, c):
            start, idx = tile_at(kt)
            kk = keys_ref[idx]
            drop = tied & (kk == thr) & ((start + col0) > x)
            keys_ref[idx] = jnp.where(drop, thr - 1, kk)
            return c
        lax.fori_loop(0, n_tiles, fix_body, 0)

    return thr


def _dsa_prompt_kernel(iq_ref, iwt_ref, aq_ref, ik_ref, ak_ref, avt_ref, o_ref,
                       keys_ref, iqt_ref, aqt_ref, m_ref, l_ref, acc_ref,
                       *, tq, tk, nsub, k_top, n_index_bits):
    assert tq == LANES
    i = pl.program_id(1)
    t0 = i * tq
    n_super = (t0 + tq + nsub * tk - 1) // (nsub * tk)
    n_tiles = nsub * n_super
    dim_row = lax.broadcasted_iota(I32, (LANES, tq), 0)
    halfmask = [(dim_row >= HEAD_DIM) == bool(e) for e in range(2)]
    t_query = t0 + lax.broadcasted_iota(I32, (1, tq), 1)

    def chunk_t(ref, c):
        return ref[:, c * LANES:(c + 1) * LANES].astype(F32).T

    for c in range(IDX_WIDTH // LANES):
        qt = chunk_t(iq_ref, c)
        for e in range(2):
            iqt_ref[c, :, e * tq:(e + 1) * tq] = jnp.where(halfmask[e], qt, 0.0).astype(BF16)

    def score_body(ks, c):
        starts = [pl.multiple_of((ks * nsub + u) * tk, tk) for u in range(nsub)]
        dots = [[jnp.dot(ik_ref[pl.ds(st, tk), :], iqt_ref[c2], preferred_element_type=F32)
                 for c2 in range(IDX_WIDTH // LANES)] for st in starts]
        for u, st in enumerate(starts):
            score = jnp.zeros((tk, tq), F32)
            for c2 in range(IDX_WIDTH // LANES):
                for e in range(2):
                    h = 2 * c2 + e
                    s = dots[u][c2][:, e * tq:(e + 1) * tq]
                    score = score + jnp.maximum(s, 0.0) * iwt_ref[0, h:h + 1, :]
            pos = st + lax.broadcasted_iota(I32, (tk, tq), 0)
            keys_ref[pl.ds(st, tk), :] = jnp.where(pos <= t_query, _sortable(score), jnp.int32(INT_MIN))
        return c
    lax.fori_loop(0, n_super, score_body, 0)

    k_row = jnp.minimum(t_query + 1, k_top)
    thr = _select_topk(keys_ref, n_tiles, tk, k_row, n_index_bits, key_axis=0)

    for c in range(SA_WIDTH // LANES):
        qt = chunk_t(aq_ref, c)
        p, j = c // 2, c % 2
        for e in range(2):
            aqt_ref[2 * p + e, :, j * tq:(j + 1) * tq] = jnp.where(halfmask[e], qt, 0.0).astype(BF16)
    m_ref[...] = jnp.full_like(m_ref, NEG_BIG)
    l_ref[...] = jnp.zeros_like(l_ref)
    acc_ref[...] = jnp.zeros_like(acc_ref)

    def att_body(ks, c):
        starts = [pl.multiple_of((ks * nsub + u) * tk, tk) for u in range(nsub)]
        logits = [[jnp.dot(ak_ref[pl.ds(st, tk), (g // 2) * LANES:(g // 2 + 1) * LANES], aqt_ref[g],
                           preferred_element_type=F32) for g in range(SA_KV_HEADS)]
                  for st in starts]
        m_all, l_all = m_ref[...], l_ref[...]
        m_cur = [m_all[hh:hh + 1] for hh in range(SA_HEADS)]
        l_cur = [l_all[hh:hh + 1] for hh in range(SA_HEADS)]
        acc_cur = [acc_ref[hh * HEAD_DIM:(hh + 1) * HEAD_DIM] for hh in range(SA_HEADS)]
        for u, st in enumerate(starts):
            sel = keys_ref[pl.ds(st, tk), :] >= thr
            probs, alphas = [], []
            for hh in range(SA_HEADS):
                g, j = hh // 2, hh % 2
                s = jnp.where(sel, logits[u][g][:, j * tq:(j + 1) * tq], NEG_BIG)
                m_new = jnp.maximum(m_cur[hh], jnp.max(s, axis=0, keepdims=True))
                alpha = jnp.exp2(m_cur[hh] - m_new)
                pr = jnp.exp2(s - m_new)
                m_cur[hh] = m_new
                l_cur[hh] = alpha * l_cur[hh] + jnp.sum(pr, axis=0, keepdims=True)
                probs.append(pr.astype(BF16))
                alphas.append(alpha)
            for hh in range(SA_HEADS):
                g = hh // 2
                vt = avt_ref[0, g * HEAD_DIM:(g + 1) * HEAD_DIM, pl.ds(st, tk)]
                acc_cur[hh] = alphas[hh] * acc_cur[hh] + jnp.dot(vt, probs[hh],
                                                                 preferred_element_type=F32)
        m_ref[...] = jnp.concatenate(m_cur, axis=0)
        l_ref[...] = jnp.concatenate(l_cur, axis=0)
        acc_ref[...] = jnp.concatenate(acc_cur, axis=0)
        return c
    lax.fori_loop(0, n_super, att_body, 0)

    def head_out(hh):
        return acc_ref[hh * HEAD_DIM:(hh + 1) * HEAD_DIM] / l_ref[hh:hh + 1]

    for p in range(SA_KV_WIDTH // LANES):
        for j in range(2):
            ot = jnp.concatenate([head_out(2 * (2 * p) + j), head_out(2 * (2 * p + 1) + j)], axis=0)
            o_ref[:, (2 * p + j) * LANES:(2 * p + j + 1) * LANES] = ot.T.astype(o_ref.dtype)


def _dsa_prompt(iq, iwt, aq, ik2, ak, avt, batch, seq, tq, tk, nsub):
    assert seq % (nsub * tk) == 0
    nq = seq // tq
    k_top = min(TOP_K_MAX, seq // 4)
    n_index_bits = max(1, seq.bit_length())
    qspec = lambda w: pl.BlockSpec((tq, w), lambda b, i: (b * nq + i, 0))
    kspec = lambda w: pl.BlockSpec((seq, w), lambda b, i: (b, 0), pipeline_mode=pl.Buffered(1))
    return pl.pallas_call(
        functools.partial(_dsa_prompt_kernel, tq=tq, tk=tk, nsub=nsub, k_top=k_top,
                          n_index_bits=n_index_bits),
        grid=(batch, nq),
        in_specs=[qspec(IDX_WIDTH),
                  pl.BlockSpec((1, IDX_HEADS, tq), lambda b, i: (b, 0, i)),
                  qspec(SA_WIDTH), kspec(LANES), kspec(SA_KV_WIDTH),
                  pl.BlockSpec((1, SA_KV_WIDTH, seq), lambda b, i: (b, 0, 0),
                               pipeline_mode=pl.Buffered(1))],
        out_specs=qspec(SA_WIDTH),
        out_shape=jax.ShapeDtypeStruct((batch * seq, SA_WIDTH), BF16),
        scratch_shapes=[pltpu.VMEM((seq, tq), I32),
                        pltpu.VMEM((IDX_HEADS // 2, LANES, 2 * tq), BF16),
                        pltpu.VMEM((SA_KV_HEADS, LANES, 2 * tq), BF16),
                        pltpu.VMEM((SA_HEADS, tq), F32), pltpu.VMEM((SA_HEADS, tq), F32),
                        pltpu.VMEM((SA_HEADS * HEAD_DIM, tq), F32)],
        compiler_params=_cparams(("parallel", "arbitrary")),
        name="dsa_prompt",
    )(iq, iwt, aq, ik2, ak, avt)


def _sb_page_step(q, kc_ref, vc_ref, o_ref, surv_ref, tri):
    @pl.when(jnp.max(surv_ref[0]) > SURV_FLOOR)
    def _():
        kt = kc_ref[0].astype(BF16)
        vt = vc_ref[0].astype(BF16)
        z = jnp.dot(q, kt, preferred_element_type=F32)
        a, surv = _stick_break_tile(z, None, surv_ref[0][:, :1], tri)
        o_ref[0] += lax.dot_general(a.astype(BF16), vt, (((1,), (1,)), ((), ())),
                                    preferred_element_type=F32)
        surv_ref[0] = jnp.broadcast_to(surv, surv_ref.shape[1:])


def _sb_sample_head_kernel(pt_ref, q_ref, kn_ref, vn_ref, *rest, n_tok, n_head_pages):
    pages, (o_ref, surv_ref) = rest[:2 * n_head_pages], rest[2 * n_head_pages:]
    rows = q_ref.shape[1]
    q = q_ref[0]
    tri = _newer_tri(PAGE_SIZE)
    z = lax.dot_general(q, kn_ref[0], (((1,), (1,)), ((), ())), preferred_element_type=F32)
    t = lax.broadcasted_iota(I32, (rows, PAGE_SIZE), 0) % n_tok
    mask = lax.broadcasted_iota(I32, (rows, PAGE_SIZE), 1) < t
    a, surv = _stick_break_tile(z, mask, jnp.zeros((rows, 1), F32), tri)
    o_ref[0] = jnp.dot(a.astype(BF16), vn_ref[0], preferred_element_type=F32)
    surv_ref[0] = jnp.broadcast_to(surv, surv_ref.shape[1:])
    for i in range(n_head_pages):
        _sb_page_step(q, pages[2 * i], pages[2 * i + 1], o_ref, surv_ref, tri)


def _sb_sample_tail_kernel(pt_ref, q_ref, acc_ref, sin_ref, kc_ref, vc_ref, o_ref, surv_ref):
    @pl.when(pl.program_id(1) == 0)
    def _():
        o_ref[...] = acc_ref[...]
        surv_ref[...] = sin_ref[...]
    _sb_page_step(q_ref[0], kc_ref, vc_ref, o_ref, surv_ref, _newer_tri(PAGE_SIZE))


def _block_diag_queries(q, n_heads_q, n_heads_kv):
    group = n_heads_q // n_heads_kv
    blocks = []
    for h in range(n_heads_q):
        kv = h // group
        qh = q[:, :, h * HEAD_DIM:(h + 1) * HEAD_DIM]
        blocks.append(jnp.pad(qh, ((0, 0), (0, 0), (kv * HEAD_DIM, (n_heads_kv - 1 - kv) * HEAD_DIM))))
    return jnp.concatenate(blocks, axis=1)


def _take_block_diag(acc, n_tok, n_heads_q, n_heads_kv):
    group = n_heads_q // n_heads_kv
    return jnp.concatenate(
        [acc[:, h * n_tok:(h + 1) * n_tok, (h // group) * HEAD_DIM:(h // group + 1) * HEAD_DIM]
         for h in range(n_heads_q)], axis=2)


def _pages_t(cache):
    n_phys, page = cache.shape[:2]
    return jnp.moveaxis(cache, 1, -1).reshape(n_phys, -1, page)


def _pad_rows(a, rows):
    return jnp.pad(a, ((0, 0), (0, rows - a.shape[1]), (0, 0)))


def _sb_sample(q, k, v, cache_k, cache_v, page_table):
    b, n_tok, _ = q.shape
    n_pages = page_table.shape[1]
    rows = SB_HEADS * n_tok
    qbd = _block_diag_queries(q, SB_HEADS, SB_HEADS)
    kn, vn = _pad_rows(k, PAGE_SIZE), _pad_rows(v, PAGE_SIZE)
    n_head = min(SB_SAMPLE_HEAD_PAGES, n_pages)
    out_shape = [jax.ShapeDtypeStruct((b, rows, SB_WIDTH), F32),
                 jax.ShapeDtypeStruct((b, rows, LANES), F32)]

    per_b1 = lambda bi, pt: (bi, 0, 0)
    page_specs = []
    for i in range(n_head):
        spec = pl.BlockSpec((1, SB_WIDTH, PAGE_SIZE), lambda bi, pt, i=i: (pt[bi, n_pages - 1 - i], 0, 0))
        page_specs += [spec, spec]
    acc, surv = pl.pallas_call(
        functools.partial(_sb_sample_head_kernel, n_tok=n_tok, n_head_pages=n_head),
        grid_spec=pltpu.PrefetchScalarGridSpec(
            num_scalar_prefetch=1,
            grid=(b,),
            in_specs=[pl.BlockSpec((1, rows, SB_WIDTH), per_b1),
                      pl.BlockSpec((1, PAGE_SIZE, SB_WIDTH), per_b1),
                      pl.BlockSpec((1, PAGE_SIZE, SB_WIDTH), per_b1)] + page_specs,
            out_specs=[pl.BlockSpec((1, rows, SB_WIDTH), per_b1), pl.BlockSpec((1, rows, LANES), per_b1)]),
        out_shape=out_shape,
        compiler_params=_cparams(("parallel",)),
        name="sb_sample_head",
    )(page_table, qbd, kn, vn, *([cache_k, cache_v] * n_head))

    n_tail = n_pages - n_head
    if n_tail > 0:
        per_b = lambda bi, j, pt: (bi, 0, 0)
        page = lambda bi, j, pt: (pt[bi, n_tail - 1 - j], 0, 0)

        def older_pages(acc, surv):
            return pl.pallas_call(
                _sb_sample_tail_kernel,
                grid_spec=pltpu.PrefetchScalarGridSpec(
                    num_scalar_prefetch=1,
                    grid=(b, n_tail),
                    in_specs=[pl.BlockSpec((1, rows, SB_WIDTH), per_b),
                              pl.BlockSpec((1, rows, SB_WIDTH), per_b),
                              pl.BlockSpec((1, rows, LANES), per_b),
                              pl.BlockSpec((1, SB_WIDTH, PAGE_SIZE), page),
                              pl.BlockSpec((1, SB_WIDTH, PAGE_SIZE), page)],
                    out_specs=[pl.BlockSpec((1, rows, SB_WIDTH), per_b),
                               pl.BlockSpec((1, rows, LANES), per_b)]),
                out_shape=out_shape,
                compiler_params=_cparams(("parallel", "arbitrary")),
                name="sb_sample_tail",
            )(page_table, qbd, acc, surv, cache_k, cache_v)[0]

        acc = lax.cond(jnp.max(surv) > SURV_FLOOR, older_pages, lambda acc, surv: acc, acc, surv)
    return _take_block_diag(acc, n_tok, SB_HEADS, SB_HEADS)


def _dsa_sel_kernel(pt_ref, iq_ref, iw_ref, ikn_ref, *rest, n_tok, n_pages, group, k_top, n_index_bits):
    pages, (sel_ref, keys_ref) = rest[:group], rest[group:]
    j = pl.program_id(1)
    width = group * PAGE_SIZE

    def scores(s):
        s = jnp.maximum(s, 0.0) * iw_ref[0]
        return jnp.sum(s.reshape(IDX_HEADS, n_tok, s.shape[-1]), axis=0)

    kt = jnp.concatenate([p_ref[0].astype(BF16) for p_ref in pages], axis=1)
    key = _sortable(scores(jnp.dot(iq_ref[0], kt, preferred_element_type=F32)))
    keys_ref[:, pl.ds(pl.multiple_of(j * width, width), width)] = key

    @pl.when(j == n_pages // group - 1)
    def _():
        t = lax.broadcasted_iota(I32, (n_tok, PAGE_SIZE), 0)
        col = lax.broadcasted_iota(I32, (n_tok, PAGE_SIZE), 1)
        s_new = lax.dot_general(iq_ref[0], ikn_ref[0], (((1,), (1,)), ((), ())),
                                preferred_element_type=F32)
        key_new = jnp.where(col <= t, _sortable(scores(s_new)), jnp.int32(INT_MIN))
        keys_ref[:, n_pages * PAGE_SIZE:] = key_new
        t_abs = n_pages * PAGE_SIZE + lax.broadcasted_iota(I32, (n_tok, 1), 0)
        k_row = jnp.minimum(t_abs + 1, k_top)
        thr = _select_topk(keys_ref, 1, keys_ref.shape[1], k_row, n_index_bits)
        sel_ref[0] = (keys_ref[...] >= thr).astype(F32)


def _dsa_sample_kernel(pt_ref, q_ref, sel_ref, seln_ref, kn_ref, vn_ref, *rest, n_tok, n_pages, group):
    k_pages, v_pages, (o_ref, m_ref, l_ref) = rest[:group], rest[group:2 * group], rest[2 * group:]
    j = pl.program_id(1)
    rows = SA_HEADS * n_tok
    nt_dims = (((1,), (1,)), ((), ()))

    @pl.when(j == 0)
    def _():
        m_ref[...] = jnp.full_like(m_ref, NEG_BIG)
        l_ref[...] = jnp.zeros_like(l_ref)
        o_ref[...] = jnp.zeros_like(o_ref)

    def step(s, sel, pv):
        n = s.shape[-1]
        sel = jnp.broadcast_to(sel[None], (SA_HEADS, n_tok, n)).reshape(rows, n)
        s = jnp.where(sel > 0.5, s, NEG_BIG)
        m_old = m_ref[...]
        m_new = jnp.maximum(m_old, jnp.max(s, axis=1, keepdims=True))
        alpha = jnp.exp2(m_old - m_new)
        pr = jnp.exp2(s - m_new)
        l_ref[...] = alpha * l_ref[...] + jnp.sum(pr, axis=1, keepdims=True)
        o_ref[0] = alpha * o_ref[0] + pv(pr.astype(BF16))
        m_ref[...] = m_new

    kt = jnp.concatenate([p_ref[0].astype(BF16) for p_ref in k_pages], axis=1)
    vt = jnp.concatenate([p_ref[0].astype(BF16) for p_ref in v_pages], axis=1)
    step(jnp.dot(q_ref[0], kt, preferred_element_type=F32), sel_ref[0],
         lambda pr: lax.dot_general(pr, vt, nt_dims, preferred_element_type=F32))

    @pl.when(j == n_pages // group - 1)
    def _():
        step(lax.dot_general(q_ref[0], kn_ref[0], nt_dims, preferred_element_type=F32), seln_ref[0],
             lambda pr: jnp.dot(pr, vn_ref[0], preferred_element_type=F32))
        o_ref[0] = o_ref[0] / l_ref[...]


def _dsa_sample(aq, ak, av, iq, iw, ik, cache_k, cache_v, cache_ik, page_table):
    b, n_tok, _ = aq.shape
    n_pages = page_table.shape[1]
    n_keys = (n_pages + 1) * PAGE_SIZE
    k_top = min(TOP_K_MAX, (n_pages * PAGE_SIZE + n_tok) // 4)
    rows = SA_HEADS * n_tok
    per_b = lambda bi, j, pt: (bi, 0, 0)

    def page_specs(width, group):
        return [pl.BlockSpec((1, width, PAGE_SIZE), lambda bi, j, pt, i=i: (pt[bi, j * group + i], 0, 0))
                for i in range(group)]

    g_sel = _largest_divisor(n_pages, DSA_SEL_GROUP)
    iq_rows = iq.reshape(b, n_tok, IDX_HEADS, IDX_DIM).transpose(0, 2, 1, 3).reshape(b, rows, IDX_DIM)
    iw_rows = iw.transpose(0, 2, 1).reshape(b, rows, 1)
    sel = pl.pallas_call(
        functools.partial(_dsa_sel_kernel, n_tok=n_tok, n_pages=n_pages, group=g_sel, k_top=k_top,
                          n_index_bits=max(1, n_keys.bit_length())),
        grid_spec=pltpu.PrefetchScalarGridSpec(
            num_scalar_prefetch=1,
            grid=(b, n_pages // g_sel),
            in_specs=[pl.BlockSpec((1, rows, IDX_DIM), per_b),
                      pl.BlockSpec((1, rows, 1), per_b),
                      pl.BlockSpec((1, PAGE_SIZE, IDX_DIM), per_b)] + page_specs(IDX_DIM, g_sel),
            out_specs=pl.BlockSpec((1, n_tok, n_keys), per_b),
            scratch_shapes=[pltpu.VMEM((n_tok, n_keys), I32)]),
        out_shape=jax.ShapeDtypeStruct((b, n_tok, n_keys), F32),
        compiler_params=_cparams(("parallel", "arbitrary")),
        name="dsa_sel",
    )(page_table, iq_rows, iw_rows, _pad_rows(ik, PAGE_SIZE), *([cache_ik] * g_sel))

    g_att = _largest_divisor(n_pages, DSA_SAMPLE_GROUP)
    qbd = _block_diag_queries(aq, SA_HEADS, SA_KV_HEADS)
    acc = pl.pallas_call(
        functools.partial(_dsa_sample_kernel, n_tok=n_tok, n_pages=n_pages, group=g_att),
        grid_spec=pltpu.PrefetchScalarGridSpec(
            num_scalar_prefetch=1,
            grid=(b, n_pages // g_att),
            in_specs=[pl.BlockSpec((1, rows, SA_KV_WIDTH), per_b),
                      pl.BlockSpec((1, n_tok, g_att * PAGE_SIZE), lambda bi, j, pt: (bi, 0, j)),
                      pl.BlockSpec((1, n_tok, PAGE_SIZE), lambda bi, j, pt: (bi, 0, n_pages)),
                      pl.BlockSpec((1, PAGE_SIZE, SA_KV_WIDTH), per_b),
                      pl.BlockSpec((1, PAGE_SIZE, SA_KV_WIDTH), per_b)]
                     + page_specs(SA_KV_WIDTH, g_att) + page_specs(SA_KV_WIDTH, g_att),
            out_specs=pl.BlockSpec((1, rows, SA_KV_WIDTH), per_b),
            scratch_shapes=[pltpu.VMEM((rows, 1), F32), pltpu.VMEM((rows, 1), F32)]),
        out_shape=jax.ShapeDtypeStruct((b, rows, SA_KV_WIDTH), F32),
        compiler_params=_cparams(("parallel", "arbitrary")),
        name="dsa_sample",
    )(page_table, qbd, sel, sel, _pad_rows(ak, PAGE_SIZE), _pad_rows(av, PAGE_SIZE),
      *([cache_k] * g_att), *([cache_v] * g_att))
    return _take_block_diag(acc, n_tok, SA_HEADS, SA_KV_HEADS)


def _tail_kernel(x_ref, osb_ref, osa_ref, gate_ref, wsb_ref, wsa_ref, wo_ref, g2_ref, wup_ref,
                 wdn_ref, y_ref, *, ff_chunk):
    gate = gate_ref[...]
    merged = (gate[:, :D_MODEL] * jnp.dot(osb_ref[...], wsb_ref[...], preferred_element_type=F32)
              + gate[:, D_MODEL:] * jnp.dot(osa_ref[...], wsa_ref[...], preferred_element_type=F32))
    x1 = x_ref[...] + jnp.dot(merged.astype(BF16), wo_ref[...], preferred_element_type=F32)
    ms = jnp.mean(x1 * x1, axis=-1, keepdims=True)
    h = (x1 * lax.rsqrt(ms + RMS_EPS) * g2_ref[...]).astype(BF16)
    y = x1
    for c in range(D_FF // ff_chunk):
        sl = slice(c * ff_chunk, (c + 1) * ff_chunk)
        u = jnp.maximum(jnp.dot(h, wup_ref[:, sl], preferred_element_type=F32), 0.0)
        y = y + jnp.dot((u * u).astype(BF16), wdn_ref[sl, :], preferred_element_type=F32)
    y_ref[...] = y


def _tail(x2d, o_sb, o_sa, gates, w_sb_out, w_sa_out, w_o, g2, w_up, w_down, tm):
    r = x2d.shape[0]
    row = lambda w: pl.BlockSpec((tm, w), lambda i: (i, 0))
    res = lambda shape: pl.BlockSpec(shape, lambda i: (0, 0), pipeline_mode=pl.Buffered(1))
    return pl.pallas_call(
        functools.partial(_tail_kernel, ff_chunk=1024),
        grid=(r // tm,),
        in_specs=[row(D_MODEL), row(SB_WIDTH), row(SA_WIDTH), row(2 * D_MODEL),
                  res((SB_WIDTH, D_MODEL)), res((SA_WIDTH, D_MODEL)), res((D_MODEL, D_MODEL)),
                  res((1, D_MODEL)), res((D_MODEL, D_FF)), res((D_FF, D_MODEL))],
        out_specs=row(D_MODEL),
        out_shape=jax.ShapeDtypeStruct((r, D_MODEL), F32),
        compiler_params=_cparams(("parallel",)),
        name="tail",
    )(x2d, o_sb, o_sa, gates, w_sb_out, w_sa_out, w_o, g2.reshape(1, D_MODEL), w_up, w_down)


def _unpermute_heads(a):
    shp = a.shape
    a = a.reshape(shp[:-1] + (SA_HEADS, HEAD_DIM))
    return a[..., jnp.array(SA_Q_PERM), :].reshape(shp)


def kernel(x_prompt, x_sample, cache_sb_k, cache_sb_v, cache_sa_k, cache_sa_v, cache_idx_k, page_table,
           norm1_g, w_in, b_gate, q_norm_g, k_norm_g, idx_k_norm_g, w_sb_out, w_sa_out, w_o,
           norm2_g, w_up, w_down):
    depth = w_in.shape[0]
    batch, seq, _ = x_prompt.shape
    dec_batch, dec_seq, _ = x_sample.shape
    n_phys = cache_sb_k.shape[1]
    past_len = page_table.shape[1] * PAGE_SIZE
    assert seq % 512 == 0 and cache_sb_k.shape[2] == PAGE_SIZE
    pos_p = jnp.arange(seq, dtype=F32)
    pos_s = past_len + jnp.arange(dec_seq, dtype=F32)
    tm_p = 256
    tm_s = 256 if (dec_batch * dec_seq) % 256 == 0 else dec_batch * dec_seq

    xp = x_prompt.reshape(batch * seq, D_MODEL)
    xs = x_sample.reshape(dec_batch * dec_seq, D_MODEL)
    new_p = [[] for _ in range(5)]
    new_s = [[] for _ in range(5)]
    for layer in range(depth):
        w_packed = _pack_w_in(w_in[layer])
        w_sa_perm = _unpermute_heads(w_sa_out[layer].T).T.astype(BF16)
        tail_w = (w_sb_out[layer].astype(BF16), w_sa_perm, w_o[layer].astype(BF16), norm2_g[layer],
                  w_up[layer].astype(BF16), w_down[layer].astype(BF16))
        norm_args = (norm1_g[layer], q_norm_g[layer], k_norm_g[layer], idx_k_norm_g[layer], b_gate[layer])

        (*new_kv, avtb, iwt, sbq, sbk, sbv, saq, sak, sav, iq, ik2, gates) = _project(
            xp, pos_p, batch, w_packed, *norm_args, tm=tm_p, group_cols=seq)
        o_sb = _sb_prompt(sbq, sbk, sbv, batch, seq, tq=256)
        o_sa = _dsa_prompt(iq, iwt, saq, ik2, sak, avtb, batch, seq, tq=128, tk=512, nsub=2)
        xp = _tail(xp, o_sb, o_sa, gates, *tail_w, tm=256)
        for lst, arr in zip(new_p, new_kv):
            arr = jnp.moveaxis(arr.reshape(batch, -1, min(arr.shape[1], HEAD_DIM), seq), 3, 1)
            lst.append(arr if arr.shape[2] > 1 else arr[:, :, 0])

        n_s = dec_batch * dec_seq
        (*new_kv, avtb, iwt, sbq, sbk, sbv, saq, sak, sav, iq, ik2, gates) = _project(
            xs, pos_s, dec_batch, w_packed, *norm_args, tm=tm_s, group_cols=n_s)
        b3 = lambda a: a.reshape(dec_batch, dec_seq, a.shape[-1])
        iw = iwt[0].T
        o_sb = _sb_sample(b3(sbq), b3(sbk), b3(sbv),
                          _pages_t(cache_sb_k[layer]), _pages_t(cache_sb_v[layer]), page_table)
        o_sa = _dsa_sample(_unpermute_heads(b3(saq)), b3(sak), b3(sav), b3(iq), b3(iw),
                           b3(ik2)[..., :IDX_DIM],
                           _pages_t(cache_sa_k[layer]), _pages_t(cache_sa_v[layer]),
                           _pages_t(cache_idx_k[layer]), page_table)
        o_sb = o_sb.reshape(dec_batch * dec_seq, SB_WIDTH).astype(BF16)
        o_sa = _unpermute_heads(o_sa).reshape(dec_batch * dec_seq, SA_WIDTH).astype(BF16)
        xs = _tail(xs, o_sb, o_sa, gates, *tail_w, tm=tm_s)
        for lst, arr in zip(new_s, new_kv):
            arr = jnp.moveaxis(arr.reshape(-1, min(arr.shape[1], HEAD_DIM), dec_batch, dec_seq), (2, 3), (0, 1))
            lst.append(arr if arr.shape[2] > 1 else arr[:, :, 0])

    return (xp.reshape(batch, seq, D_MODEL), xs.reshape(dec_batch, dec_seq, D_MODEL),
            *(jnp.stack(l) for l in new_p), *(jnp.stack(l) for l in new_s))
```

```python
import functools

import jax
import jax.numpy as jnp
from jax import lax
from jax.experimental import pallas as pl
from jax.experimental.pallas import tpu as pltpu

F32 = jnp.float32
BF16 = jnp.bfloat16
I32 = jnp.int32

D_MODEL = 1024
HEAD_DIM = 64
SB_HEADS = 8
SA_HEADS = 8
SA_KV_HEADS = 4
IDX_HEADS = 8
IDX_DIM = 64
SB_WIDTH = SB_HEADS * HEAD_DIM
SA_WIDTH = SA_HEADS * HEAD_DIM
SA_KV_WIDTH = SA_KV_HEADS * HEAD_DIM
IDX_WIDTH = IDX_HEADS * IDX_DIM
TOP_K_MAX = 256
ROT_DIM = HEAD_DIM // 4
ROPE_THETA = 500000.0
D_FF = 4 * D_MODEL
PAGE_SIZE = 128
RMS_EPS = 1e-6
QK_SCALE = HEAD_DIM ** -0.5
SA_Q_SCALE = QK_SCALE * 1.4426950408889634
IDX_SCALE = IDX_DIM ** -0.5
IDX_W_SCALE = IDX_HEADS ** -0.5

LANES = 128
VMEM_LIMIT = 56 * 1024 * 1024

C_SQ, C_SK, C_SV = 0, 512, 1024
C_AQ, C_AK, C_AV = 1536, 2048, 2304
C_IQ, C_IK2, C_IW = 2560, 3072, 3200
C_GATE = 3328
C_END = C_GATE + 2 * D_MODEL
SA_Q_PERM = (0, 2, 1, 3, 4, 6, 5, 7)

SURV_FLOOR = -105.0
NEG_BIG = -1e30
INT_MIN = -2 ** 31

SB_SAMPLE_HEAD_PAGES = 4
DSA_SEL_GROUP = 32
DSA_SAMPLE_GROUP = 16


TOPK_GROUPS = 256


def _largest_divisor(n, cap):
    return max(d for d in range(1, min(n, cap) + 1) if n % d == 0)


def _cparams(sem, vmem=VMEM_LIMIT):
    return pltpu.CompilerParams(dimension_semantics=sem, vmem_limit_bytes=vmem)


def _rope_chunk(y, cos, sin_a, sin_b):
    up = pltpu.roll(y, LANES - ROT_DIM // 2, 1)
    dn = pltpu.roll(y, ROT_DIM // 2, 1)
    return y * cos + up * sin_a + dn * sin_b


def _head_rms_chunk(y, e_mat, gain):
    y2 = y * y
    hi = y2.astype(BF16)
    lo = (y2 - hi.astype(F32)).astype(BF16)
    ms = (jnp.dot(hi, e_mat, preferred_element_type=F32)
          + jnp.dot(lo, e_mat, preferred_element_type=F32))
    return y * lax.rsqrt(ms + RMS_EPS) * gain


def _proj_kernel(x_ref, g1_ref, w_ref, qg_ref, kg_ref, ig_ref, bg_ref, cos_ref, sa_ref, sb_ref,
                 e_ref,
                 skt_ref, svt_ref, akt_ref, avt_ref, ikt_ref, avtb_ref, iwt_ref,
                 sbq_ref, sbk_ref, sbv_ref, saq_ref, sak_ref, sav_ref, iq_ref, ik2_ref, gate_ref):
    x = x_ref[...]
    ms = jnp.mean(x * x, axis=-1, keepdims=True)
    h = (x * lax.rsqrt(ms + RMS_EPS) * g1_ref[...]).astype(BF16)

    def mm(lo, hi):
        return jnp.dot(h, w_ref[:, lo:hi], preferred_element_type=F32)

    cos, sin_a, sin_b = cos_ref[...], sa_ref[...], sb_ref[...]
    e_mat = e_ref[...]

    p = mm(C_SQ, C_SK)
    sbq_ref[...] = (p * QK_SCALE).astype(BF16)
    p = mm(C_SK, C_SV)
    skt_ref[0] = p.T
    sbk_ref[...] = p.astype(BF16)
    p = mm(C_SV, C_AQ)
    svt_ref[0] = p.T
    sbv_ref[...] = p.astype(BF16)

    p = mm(C_AQ, C_AK)
    for c in range(SA_WIDTH // LANES):
        sl = slice(c * LANES, (c + 1) * LANES)
        y = _rope_chunk(_head_rms_chunk(p[:, sl], e_mat, qg_ref[...]), cos, sin_a, sin_b)
        saq_ref[:, sl] = (y * SA_Q_SCALE).astype(BF16)
    p = mm(C_AK, C_AV)
    for c in range(SA_KV_WIDTH // LANES):
        sl = slice(c * LANES, (c + 1) * LANES)
        y = _rope_chunk(_head_rms_chunk(p[:, sl], e_mat, kg_ref[...]), cos, sin_a, sin_b)
        akt_ref[0, sl, :] = y.T
        sak_ref[:, sl] = y.astype(BF16)
    p = mm(C_AV, C_IQ)
    pt = p.T
    avt_ref[0] = pt
    avtb_ref[0] = pt.astype(BF16)
    sav_ref[...] = p.astype(BF16)

    p = mm(C_IQ, C_IK2)
    for c in range(IDX_WIDTH // LANES):
        sl = slice(c * LANES, (c + 1) * LANES)
        iq_ref[:, sl] = (_rope_chunk(p[:, sl], cos, sin_a, sin_b) * IDX_SCALE).astype(BF16)
    p = mm(C_IK2, C_GATE)
    y = _rope_chunk(_head_rms_chunk(p[:, :LANES], e_mat, ig_ref[...]), cos, sin_a, sin_b)
    ikt_ref[0] = y.T[:IDX_DIM]
    ik2_ref[...] = y.astype(BF16)
    iwt_ref[0] = p[:, LANES:].T[:IDX_HEADS] * IDX_W_SCALE

    p = mm(C_GATE, C_END) + bg_ref[...]
    gate_ref[...] = jax.nn.sigmoid(p)


def _rope_tables(pos):
    half = ROT_DIM // 2
    inv = 1.0 / (ROPE_THETA ** (jnp.arange(0, ROT_DIM, 2, dtype=F32) / ROT_DIM))
    ang = pos[:, None] * inv[None, :]
    cos, sin = jnp.cos(ang), jnp.sin(ang)
    t = pos.shape[0]
    zeros = jnp.zeros((t, HEAD_DIM - ROT_DIM), F32)
    zh = jnp.zeros((t, half), F32)
    cos_h = jnp.concatenate([cos, cos, zeros + 1.0], axis=1)
    sa_h = jnp.concatenate([-sin, zh, zeros], axis=1)
    sb_h = jnp.concatenate([zh, sin, zeros], axis=1)
    return tuple(jnp.concatenate([a, a], axis=1) for a in (cos_h, sa_h, sb_h))


def _pack_w_in(w_in):
    o = 0
    parts = []
    for n in (SB_WIDTH, SB_WIDTH, SB_WIDTH, SA_WIDTH, SA_KV_WIDTH, SA_KV_WIDTH, IDX_WIDTH,
              IDX_DIM, IDX_HEADS, 2 * D_MODEL):
        parts.append(w_in[:, o:o + n])
        o += n
    sq, sk, sv, aq, ak, av, iq, ik, iw, gates = parts
    aq = aq.reshape(D_MODEL, SA_HEADS, HEAD_DIM)[:, jnp.array(SA_Q_PERM)].reshape(D_MODEL, SA_WIDTH)
    pad = jnp.zeros((D_MODEL, C_GATE - C_IW - IDX_HEADS), w_in.dtype)
    return jnp.concatenate([sq, sk, sv, aq, ak, av, iq, ik, ik, iw, pad, gates], axis=1).astype(BF16)


def _project(x2d, pos, n_rep, w_packed, g1, qg, kg, ig, bg, tm, group_cols):
    r = x2d.shape[0]
    t = pos.shape[0]
    assert r == n_rep * t and r % tm == 0 and (t % tm == 0 or tm % t == 0)
    assert r % group_cols == 0 and group_cols % tm == 0
    ntc = group_cols // tm
    cos, sin_a, sin_b = _rope_tables(pos)
    if tm > t:
        cos, sin_a, sin_b = (jnp.tile(a, (tm // t, 1)) for a in (cos, sin_a, sin_b))
    nt = cos.shape[0] // tm
    e_mat = (jnp.arange(LANES)[:, None] // HEAD_DIM == jnp.arange(LANES)[None, :] // HEAD_DIM)
    e_mat = (e_mat.astype(F32) / HEAD_DIM).astype(BF16)

    def row(width):
        return pl.BlockSpec((tm, width), lambda i: (i, 0))

    def const(shape):
        return pl.BlockSpec(shape, lambda i: (0, 0))

    tab = pl.BlockSpec((tm, LANES), lambda i: (i % nt, 0))
    tile2 = lambda g: jnp.concatenate([g, g]).reshape(1, LANES)
    out_heights = [(SB_WIDTH, F32), (SB_WIDTH, F32), (SA_KV_WIDTH, F32), (SA_KV_WIDTH, F32),
                   (IDX_DIM, F32), (SA_KV_WIDTH, BF16), (IDX_HEADS, F32)]
    out_widths = [(SB_WIDTH, BF16), (SB_WIDTH, BF16), (SB_WIDTH, BF16),
                  (SA_WIDTH, BF16), (SA_KV_WIDTH, BF16), (SA_KV_WIDTH, BF16),
                  (IDX_WIDTH, BF16), (LANES, BF16), (2 * D_MODEL, F32)]
    return pl.pallas_call(
        _proj_kernel,
        grid=(r // tm,),
        in_specs=[row(D_MODEL), const((1, D_MODEL)),
                  pl.BlockSpec((D_MODEL, C_END), lambda i: (0, 0), pipeline_mode=pl.Buffered(1)),
                  const((1, LANES)), const((1, LANES)), const((1, LANES)), const((1, 2 * D_MODEL)),
                  tab, tab, tab, const((LANES, LANES))],
        out_specs=[pl.BlockSpec((1, h_, tm), lambda i: (i // ntc, 0, i % ntc)) for h_, _ in out_heights]
                  + [row(w_) for w_, _ in out_widths],
        out_shape=[jax.ShapeDtypeStruct((r // group_cols, h_, group_cols), dt) for h_, dt in out_heights]
                  + [jax.ShapeDtypeStruct((r, w_), dt) for w_, dt in out_widths],
        compiler_params=_cparams(("parallel",)),
        name="proj",
    )(x2d, g1.reshape(1, D_MODEL), w_packed, tile2(qg), tile2(kg), tile2(ig),
      bg.reshape(1, 2 * D_MODEL), cos, sin_a, sin_b, e_mat)


def _stick_break_tile(z, mask, surv, tri):
    t = jnp.log1p(jnp.exp(-jnp.abs(z)))
    l = -(jnp.maximum(z, 0.0) + t)
    if mask is not None:
        l = jnp.where(mask, l, 0.0)
    ls = jnp.minimum(z, 0.0) - t
    hi = l.astype(BF16)
    lo = (l - hi.astype(F32)).astype(BF16)
    ex = jnp.dot(hi, tri, preferred_element_type=F32) + jnp.dot(lo, tri, preferred_element_type=F32)
    a = jnp.exp(ls + ex + surv)
    if mask is not None:
        a = jnp.where(mask, a, 0.0)
    return a, surv + jnp.sum(l, axis=1, keepdims=True)


def _newer_tri(n):
    j = lax.broadcasted_iota(I32, (n, n), 0)
    s = lax.broadcasted_iota(I32, (n, n), 1)
    return (j > s).astype(BF16)


def _sb_prompt_kernel(q_ref, k_ref, v_ref, o_ref, acc_ref, surv_ref, *, tq):
    i = pl.program_id(2)
    t0 = i * tq
    lane = lax.broadcasted_iota(I32, (tq, LANES), 1)
    row_t = t0 + lax.broadcasted_iota(I32, (tq, tq), 0)
    col0 = lax.broadcasted_iota(I32, (tq, tq), 1)
    tri = _newer_tri(tq)
    q = q_ref[...]
    heads = range(2)
    qm = [jnp.where((lane >= HEAD_DIM) == bool(h), q, jnp.zeros_like(q)) for h in heads]
    acc_ref[...] = jnp.zeros_like(acc_ref)
    surv_ref[...] = jnp.zeros_like(surv_ref)

    def body(carry):
        kt, _ = carry
        start = pl.multiple_of(kt * tq, tq)
        kb = k_ref[pl.ds(start, tq), :]
        vb = v_ref[pl.ds(start, tq), :]
        mask = (start + col0) < row_t
        z = [lax.dot_general(qm[h], kb, (((1,), (1,)), ((), ())), preferred_element_type=F32)
             for h in heads]
        l, ls, ex = [], [], []
        for h in heads:
            t = jnp.log1p(jnp.exp(-jnp.abs(z[h])))
            l.append(jnp.where(mask, -(jnp.maximum(z[h], 0.0) + t), 0.0))
            ls.append(jnp.minimum(z[h], 0.0) - t)
            hi = l[h].astype(BF16)
            lo = (l[h] - hi.astype(F32)).astype(BF16)
            ex.append(jnp.dot(hi, tri, preferred_element_type=F32)
                      + jnp.dot(lo, tri, preferred_element_type=F32))
        acc, surv = [], []
        for h in heads:
            a = jnp.where(mask, jnp.exp(ls[h] + ex[h] + surv_ref[h]), 0.0)
            acc.append(acc_ref[h] + jnp.dot(a.astype(BF16), vb, preferred_element_type=F32))
            surv.append(surv_ref[h] + jnp.sum(l[h], axis=1, keepdims=True))
        acc_ref[...] = jnp.stack(acc)
        surv_ref[...] = jnp.stack(surv)
        return kt - 1, jnp.maximum(jnp.max(surv[0]), jnp.max(surv[1]))

    lax.while_loop(lambda c: (c[0] >= 0) & (c[1] > SURV_FLOOR), body, (i, jnp.float32(0.0)))
    o_ref[...] = jnp.where(lane < HEAD_DIM, acc_ref[0], acc_ref[1]).astype(o_ref.dtype)


def _sb_prompt(q, k, v, batch, seq, tq):
    nq = seq // tq
    n_chunks = SB_WIDTH // LANES
    qspec = pl.BlockSpec((tq, LANES), lambda b, c, i: (b * nq + i, c))
    kvspec = pl.BlockSpec((seq, LANES), lambda b, c, i: (b, c))
    return pl.pallas_call(
        functools.partial(_sb_prompt_kernel, tq=tq),
        grid=(batch, n_chunks, nq),
        in_specs=[qspec, kvspec, kvspec],
        out_specs=qspec,
        out_shape=jax.ShapeDtypeStruct((batch * seq, SB_WIDTH), BF16),
        scratch_shapes=[pltpu.VMEM((2, tq, LANES), F32), pltpu.VMEM((2, tq, 1), F32)],
        compiler_params=_cparams(("parallel", "parallel", "arbitrary")),
        name="sb_prompt",
    )(q, k, v)


def _sortable(x):
    bits = lax.bitcast_convert_type(x, I32)
    key = jnp.where(bits < 0, bits ^ jnp.int32(0x7FFFFFFF), bits)
    return jnp.where(x == 0.0, jnp.int32(0), key)


def _select_topk(keys_ref, n_tiles, tk, k_row, n_index_bits, key_axis=1):
    qn = keys_ref.shape[1 - key_axis]
    sign = jnp.int32(INT_MIN)
    tile_shape = (qn, tk) if key_axis == 1 else (tk, qn)
    fold = LANES if key_axis == 1 else 8
    vec = (qn, 1) if key_axis == 1 else (1, qn)

    def tile_at(kt):
        start = pl.multiple_of(kt * tk, tk)
        idx = (slice(None), pl.ds(start, tk)) if key_axis == 1 else (pl.ds(start, tk), slice(None))
        return start, idx

    def count(pred):
        def tile_body(kt, acc):
            start, idx = tile_at(kt)
            m = pred(keys_ref[idx], start).astype(I32)
            if key_axis == 1:
                for c in range(tk // fold):
                    acc = acc + m[:, c * fold:(c + 1) * fold]
                return acc
            return acc + jnp.sum(m.reshape(tk // fold, fold, qn), axis=0)
        acc0 = jnp.zeros((qn, fold) if key_axis == 1 else (fold, qn), I32)
        acc = lax.fori_loop(0, n_tiles, tile_body, acc0)
        if key_axis == 0:
            for shift in (4, 2, 1):
                acc = acc + pltpu.roll(acc, shift, 0)
            return acc[:1]
        return jnp.sum(acc.astype(F32), axis=1, keepdims=True).astype(I32)

    bracket = None
    if key_axis == 0 and tk % TOPK_GROUPS == 0:
        def gmax_body(kt, acc):
            _, idx = tile_at(kt)
            return jnp.maximum(acc, jnp.max(keys_ref[idx].reshape(tk // TOPK_GROUPS, TOPK_GROUPS, qn), axis=0))
        gmax = lax.fori_loop(0, n_tiles, gmax_body, jnp.full((TOPK_GROUPS, qn), INT_MIN, I32))
        gmax = gmax.reshape(TOPK_GROUPS // fold, fold, qn)
        hi_b, lo_b = jnp.max(gmax, axis=0), jnp.min(gmax, axis=0)
        for shift in (4, 2, 1):
            hi_b = jnp.maximum(hi_b, pltpu.roll(hi_b, shift, 0))
            lo_b = jnp.minimum(lo_b, pltpu.roll(lo_b, shift, 0))
        bracket = (lo_b[:1], hi_b[:1])

    zeros = jnp.zeros(vec, I32)
    b_first, prefix0 = jnp.int32(31), zeros
    if bracket is not None:
        lo_u, hi_u = bracket[0] ^ sign, bracket[1] ^ sign
        diff = lo_u ^ hi_u
        expo = lax.shift_right_logical(lax.bitcast_convert_type(jnp.maximum(diff, 1).astype(F32), I32),
                                       jnp.int32(23)) - 127
        top = jnp.where(diff < 0, 31, expo)
        b_first = jnp.minimum(jnp.max(top.astype(F32)).astype(I32), 31)
        keep = jnp.where(b_first >= 31, 0, lax.shift_left(jnp.int32(-1), jnp.minimum(b_first + 1, 31)))
        prefix0 = hi_u & keep

    def one_bit(b, prefix, thr, done):
        bit = jnp.where(b >= 0, lax.shift_left(jnp.int32(1), jnp.maximum(b, 0)), 0)
        cand_u = prefix | bit
        cand = cand_u ^ sign
        cnt = count(lambda kk, start: kk >= cand)
        live = done == 0
        prefix = jnp.where(live & (cnt >= k_row), cand_u, prefix)
        hit = live & (cnt == k_row)
        return prefix, jnp.where(hit, cand, thr), jnp.where(hit, 1, done)

    def bit_body(carry):
        b, prefix, thr, done, _ = carry
        for u in range(2):
            prefix, thr, done = one_bit(b - u, prefix, thr, done)
        return b - 2, prefix, thr, done, jnp.min(done.astype(F32)).astype(I32)

    _, prefix, thr, done, all_done = lax.while_loop(
        lambda c: (c[0] >= 0) & (c[4] == 0), bit_body,
        (b_first, prefix0, zeros, zeros, jnp.int32(0)))
    thr = jnp.where(done == 0, prefix ^ sign, thr)

    if key_axis == 0:
        @pl.when(all_done == 0)
        def _():
            tied = done == 0
            want = (k_row - count(lambda kk, start: kk > thr)).astype(F32)
            half = tk // 2
            r = lax.broadcasted_iota(I32, (half, half), 0)
            c = lax.broadcasted_iota(I32, (half, half), 1)
            tri = (c <= r).astype(BF16)

            def fix_body(kt, seen):
                start, _ = tile_at(kt)
                for u in range(2):
                    idx = (pl.ds(pl.multiple_of(start + u * half, half), half), slice(None))
                    kk = keys_ref[idx]
                    tie = tied & (kk == thr)
                    rank = seen + jnp.dot(tri, tie.astype(BF16), preferred_element_type=F32)
                    keys_ref[idx] = jnp.where(tie & (rank > want), thr - 1, kk)
                    seen = rank[half - 1:half]
                return seen
            lax.fori_loop(0, n_tiles, fix_body, jnp.zeros(vec, F32))
        return thr

    @pl.when(all_done == 0)
    def _():
        tied = done == 0
        want = k_row - count(lambda kk, start: kk > thr)
        col0 = lax.broadcasted_iota(I32, tile_shape, key_axis)

        def ties_before(x):
            return count(lambda kk, start: (kk == thr) & ((start + col0) < x))

        def idx_body(j, x):
            cand = x | lax.shift_left(jnp.int32(1), n_index_bits - 1 - j)
            return jnp.where(ties_before(cand) < want, cand, x)

        x = lax.fori_loop(0, n_index_bits, idx_body, zeros)

        def fix_body(kt, c):
            start, idx = tile_at(kt)
            kk = keys_ref[idx]
            drop = tied & (kk == thr) & ((start + col0) > x)
            keys_ref[idx] = jnp.where(drop, thr - 1, kk)
            return c
        lax.fori_loop(0, n_tiles, fix_body, 0)

    return thr


def _dsa_prompt_kernel(iq_ref, iwt_ref, aq_ref, ik_ref, ak_ref, avt_ref, o_ref,
                       keys_ref, iqt_ref, aqt_ref, m_ref, l_ref, acc_ref,
                       *, tq, tk, nsub, k_top, n_index_bits):
    assert tq == LANES
    i = pl.program_id(1)
    t0 = i * tq
    n_super = (t0 + tq + nsub * tk - 1) // (nsub * tk)
    n_tiles = nsub * n_super
    dim_row = lax.broadcasted_iota(I32, (LANES, tq), 0)
    halfmask = [(dim_row >= HEAD_DIM) == bool(e) for e in range(2)]
    t_query = t0 + lax.broadcasted_iota(I32, (1, tq), 1)

    def chunk_t(ref, c):
        return ref[:, c * LANES:(c + 1) * LANES].astype(F32).T

    for c in range(IDX_WIDTH // LANES):
        qt = chunk_t(iq_ref, c)
        for e in range(2):
            iqt_ref[c, :, e * tq:(e + 1) * tq] = jnp.where(halfmask[e], qt, 0.0).astype(BF16)

    def score_body(ks, c):
        starts = [pl.multiple_of((ks * nsub + u) * tk, tk) for u in range(nsub)]
        dots = [[jnp.dot(ik_ref[pl.ds(st, tk), :], iqt_ref[c2], preferred_element_type=F32)
                 for c2 in range(IDX_WIDTH // LANES)] for st in starts]
        for u, st in enumerate(starts):
            score = jnp.zeros((tk, tq), F32)
            for c2 in range(IDX_WIDTH // LANES):
                for e in range(2):
                    h = 2 * c2 + e
                    s = dots[u][c2][:, e * tq:(e + 1) * tq]
                    score = score + jnp.maximum(s, 0.0) * iwt_ref[0, h:h + 1, :]
            pos = st + lax.broadcasted_iota(I32, (tk, tq), 0)
            keys_ref[pl.ds(st, tk), :] = jnp.where(pos <= t_query, _sortable(score), jnp.int32(INT_MIN))
        return c
    lax.fori_loop(0, n_super, score_body, 0)

    k_row = jnp.minimum(t_query + 1, k_top)
    thr = _select_topk(keys_ref, (t0 + tq + tk - 1) // tk, tk, k_row, n_index_bits, key_axis=0)

    for c in range(SA_WIDTH // LANES):
        qt = chunk_t(aq_ref, c)
        p, j = c // 2, c % 2
        for e in range(2):
            aqt_ref[2 * p + e, :, j * tq:(j + 1) * tq] = jnp.where(halfmask[e], qt, 0.0).astype(BF16)
    m_ref[...] = jnp.full_like(m_ref, NEG_BIG)
    l_ref[...] = jnp.zeros_like(l_ref)
    acc_ref[...] = jnp.zeros_like(acc_ref)

    def att_body(ks, c):
        starts = [pl.multiple_of((ks * nsub + u) * tk, tk) for u in range(nsub)]
        logits = [[jnp.dot(ak_ref[pl.ds(st, tk), (g // 2) * LANES:(g // 2 + 1) * LANES], aqt_ref[g],
                           preferred_element_type=F32) for g in range(SA_KV_HEADS)]
                  for st in starts]
        m_all, l_all = m_ref[...], l_ref[...]
        m_cur = [m_all[hh:hh + 1] for hh in range(SA_HEADS)]
        l_cur = [l_all[hh:hh + 1] for hh in range(SA_HEADS)]
        acc_cur = [acc_ref[hh * HEAD_DIM:(hh + 1) * HEAD_DIM] for hh in range(SA_HEADS)]
        for u, st in enumerate(starts):
            sel = keys_ref[pl.ds(st, tk), :] >= thr
            probs, alphas = [], []
            for hh in range(SA_HEADS):
                g, j = hh // 2, hh % 2
                s = jnp.where(sel, logits[u][g][:, j * tq:(j + 1) * tq], NEG_BIG)
                m_new = jnp.maximum(m_cur[hh], jnp.max(s, axis=0, keepdims=True))
                alpha = jnp.exp2(m_cur[hh] - m_new)
                pr = jnp.exp2(s - m_new)
                m_cur[hh] = m_new
                l_cur[hh] = alpha * l_cur[hh] + jnp.sum(pr, axis=0, keepdims=True)
                probs.append(pr.astype(BF16))
                alphas.append(alpha)
            for hh in range(SA_HEADS):
                g = hh // 2
                vt = avt_ref[0, g * HEAD_DIM:(g + 1) * HEAD_DIM, pl.ds(st, tk)]
                acc_cur[hh] = alphas[hh] * acc_cur[hh] + jnp.dot(vt, probs[hh],
                                                                 preferred_element_type=F32)
        m_ref[...] = jnp.concatenate(m_cur, axis=0)
        l_ref[...] = jnp.concatenate(l_cur, axis=0)
        acc_ref[...] = jnp.concatenate(acc_cur, axis=0)
        return c
    lax.fori_loop(0, n_super, att_body, 0)

    def head_out(hh):
        return acc_ref[hh * HEAD_DIM:(hh + 1) * HEAD_DIM] / l_ref[hh:hh + 1]

    for p in range(SA_KV_WIDTH // LANES):
        for j in range(2):
            ot = jnp.concatenate([head_out(2 * (2 * p) + j), head_out(2 * (2 * p + 1) + j)], axis=0)
            o_ref[:, (2 * p + j) * LANES:(2 * p + j + 1) * LANES] = ot.T.astype(o_ref.dtype)


def _dsa_prompt(iq, iwt, aq, ik2, ak, avt, batch, seq, tq, tk, nsub):
    assert seq % (nsub * tk) == 0
    nq = seq // tq
    k_top = min(TOP_K_MAX, seq // 4)
    n_index_bits = max(1, seq.bit_length())
    qspec = lambda w: pl.BlockSpec((tq, w), lambda b, i: (b * nq + i, 0))
    kspec = lambda w: pl.BlockSpec((seq, w), lambda b, i: (b, 0), pipeline_mode=pl.Buffered(1))
    return pl.pallas_call(
        functools.partial(_dsa_prompt_kernel, tq=tq, tk=tk, nsub=nsub, k_top=k_top,
                          n_index_bits=n_index_bits),
        grid=(batch, nq),
        in_specs=[qspec(IDX_WIDTH),
                  pl.BlockSpec((1, IDX_HEADS, tq), lambda b, i: (b, 0, i)),
                  qspec(SA_WIDTH), kspec(LANES), kspec(SA_KV_WIDTH),
                  pl.BlockSpec((1, SA_KV_WIDTH, seq), lambda b, i: (b, 0, 0),
                               pipeline_mode=pl.Buffered(1))],
        out_specs=qspec(SA_WIDTH),
        out_shape=jax.ShapeDtypeStruct((batch * seq, SA_WIDTH), BF16),
        scratch_shapes=[pltpu.VMEM((seq, tq), I32),
                        pltpu.VMEM((IDX_HEADS // 2, LANES, 2 * tq), BF16),
                        pltpu.VMEM((SA_KV_HEADS, LANES, 2 * tq), BF16),
                        pltpu.VMEM((SA_HEADS, tq), F32), pltpu.VMEM((SA_HEADS, tq), F32),
                        pltpu.VMEM((SA_HEADS * HEAD_DIM, tq), F32)],
        compiler_params=_cparams(("parallel", "arbitrary")),
        name="dsa_prompt",
    )(iq, iwt, aq, ik2, ak, avt)


def _sb_page_step(q, kc_ref, vc_ref, o_ref, surv_ref, tri):
    @pl.when(jnp.max(surv_ref[0]) > SURV_FLOOR)
    def _():
        kt = kc_ref[0].astype(BF16)
        vt = vc_ref[0].astype(BF16)
        z = jnp.dot(q, kt, preferred_element_type=F32)
        a, surv = _stick_break_tile(z, None, surv_ref[0][:, :1], tri)
        o_ref[0] += lax.dot_general(a.astype(BF16), vt, (((1,), (1,)), ((), ())),
                                    preferred_element_type=F32)
        surv_ref[0] = jnp.broadcast_to(surv, surv_ref.shape[1:])


def _sb_sample_head_kernel(pt_ref, q_ref, kn_ref, vn_ref, *rest, n_tok, n_head_pages):
    pages, (o_ref, surv_ref) = rest[:2 * n_head_pages], rest[2 * n_head_pages:]
    rows = q_ref.shape[1]
    q = q_ref[0]
    tri = _newer_tri(PAGE_SIZE)
    z = lax.dot_general(q, kn_ref[0], (((1,), (1,)), ((), ())), preferred_element_type=F32)
    t = lax.broadcasted_iota(I32, (rows, PAGE_SIZE), 0) % n_tok
    mask = lax.broadcasted_iota(I32, (rows, PAGE_SIZE), 1) < t
    a, surv = _stick_break_tile(z, mask, jnp.zeros((rows, 1), F32), tri)
    o_ref[0] = jnp.dot(a.astype(BF16), vn_ref[0], preferred_element_type=F32)
    surv_ref[0] = jnp.broadcast_to(surv, surv_ref.shape[1:])
    for i in range(n_head_pages):
        _sb_page_step(q, pages[2 * i], pages[2 * i + 1], o_ref, surv_ref, tri)


def _sb_sample_tail_kernel(pt_ref, q_ref, acc_ref, sin_ref, kc_ref, vc_ref, o_ref, surv_ref):
    @pl.when(pl.program_id(1) == 0)
    def _():
        o_ref[...] = acc_ref[...]
        surv_ref[...] = sin_ref[...]
    _sb_page_step(q_ref[0], kc_ref, vc_ref, o_ref, surv_ref, _newer_tri(PAGE_SIZE))


def _block_diag_queries(q, n_heads_q, n_heads_kv):
    group = n_heads_q // n_heads_kv
    blocks = []
    for h in range(n_heads_q):
        kv = h // group
        qh = q[:, :, h * HEAD_DIM:(h + 1) * HEAD_DIM]
        blocks.append(jnp.pad(qh, ((0, 0), (0, 0), (kv * HEAD_DIM, (n_heads_kv - 1 - kv) * HEAD_DIM))))
    return jnp.concatenate(blocks, axis=1)


def _take_block_diag(acc, n_tok, n_heads_q, n_heads_kv):
    group = n_heads_q // n_heads_kv
    return jnp.concatenate(
        [acc[:, h * n_tok:(h + 1) * n_tok, (h // group) * HEAD_DIM:(h // group + 1) * HEAD_DIM]
         for h in range(n_heads_q)], axis=2)


def _pages_t(cache):
    n_phys, page = cache.shape[:2]
    return jnp.moveaxis(cache, 1, -1).reshape(n_phys, -1, page)


def _pad_rows(a, rows):
    return jnp.pad(a, ((0, 0), (0, rows - a.shape[1]), (0, 0)))


def _sb_sample(q, k, v, cache_k, cache_v, page_table):
    b, n_tok, _ = q.shape
    n_pages = page_table.shape[1]
    rows = SB_HEADS * n_tok
    qbd = _block_diag_queries(q, SB_HEADS, SB_HEADS)
    kn, vn = _pad_rows(k, PAGE_SIZE), _pad_rows(v, PAGE_SIZE)
    n_head = min(SB_SAMPLE_HEAD_PAGES, n_pages)
    out_shape = [jax.ShapeDtypeStruct((b, rows, SB_WIDTH), F32),
                 jax.ShapeDtypeStruct((b, rows, LANES), F32)]

    per_b1 = lambda bi, pt: (bi, 0, 0)
    page_specs = []
    for i in range(n_head):
        spec = pl.BlockSpec((1, SB_WIDTH, PAGE_SIZE), lambda bi, pt, i=i: (pt[bi, n_pages - 1 - i], 0, 0))
        page_specs += [spec, spec]
    acc, surv = pl.pallas_call(
        functools.partial(_sb_sample_head_kernel, n_tok=n_tok, n_head_pages=n_head),
        grid_spec=pltpu.PrefetchScalarGridSpec(
            num_scalar_prefetch=1,
            grid=(b,),
            in_specs=[pl.BlockSpec((1, rows, SB_WIDTH), per_b1),
                      pl.BlockSpec((1, PAGE_SIZE, SB_WIDTH), per_b1),
                      pl.BlockSpec((1, PAGE_SIZE, SB_WIDTH), per_b1)] + page_specs,
            out_specs=[pl.BlockSpec((1, rows, SB_WIDTH), per_b1), pl.BlockSpec((1, rows, LANES), per_b1)]),
        out_shape=out_shape,
        compiler_params=_cparams(("parallel",)),
        name="sb_sample_head",
    )(page_table, qbd, kn, vn, *([cache_k, cache_v] * n_head))

    n_tail = n_pages - n_head
    if n_tail > 0:
        per_b = lambda bi, j, pt: (bi, 0, 0)
        page = lambda bi, j, pt: (pt[bi, n_tail - 1 - j], 0, 0)

        def older_pages(acc, surv):
            return pl.pallas_call(
                _sb_sample_tail_kernel,
                grid_spec=pltpu.PrefetchScalarGridSpec(
                    num_scalar_prefetch=1,
                    grid=(b, n_tail),
                    in_specs=[pl.BlockSpec((1, rows, SB_WIDTH), per_b),
                              pl.BlockSpec((1, rows, SB_WIDTH), per_b),
                              pl.BlockSpec((1, rows, LANES), per_b),
                              pl.BlockSpec((1, SB_WIDTH, PAGE_SIZE), page),
                              pl.BlockSpec((1, SB_WIDTH, PAGE_SIZE), page)],
                    out_specs=[pl.BlockSpec((1, rows, SB_WIDTH), per_b),
                               pl.BlockSpec((1, rows, LANES), per_b)]),
                out_shape=out_shape,
                compiler_params=_cparams(("parallel", "arbitrary")),
                name="sb_sample_tail",
            )(page_table, qbd, acc, surv, cache_k, cache_v)[0]

        acc = lax.cond(jnp.max(surv) > SURV_FLOOR, older_pages, lambda acc, surv: acc, acc, surv)
    return _take_block_diag(acc, n_tok, SB_HEADS, SB_HEADS)


def _dsa_sel_kernel(pt_ref, iq_ref, iw_ref, ikn_ref, *rest, n_tok, n_pages, group, k_top, n_index_bits):
    pages, (sel_ref, keys_ref) = rest[:group], rest[group:]
    j = pl.program_id(1)
    width = group * PAGE_SIZE

    def scores(s):
        s = jnp.maximum(s, 0.0) * iw_ref[0]
        return jnp.sum(s.reshape(IDX_HEADS, n_tok, s.shape[-1]), axis=0)

    kt = jnp.concatenate([p_ref[0].astype(BF16) for p_ref in pages], axis=1)
    key = _sortable(scores(jnp.dot(iq_ref[0], kt, preferred_element_type=F32)))
    keys_ref[:, pl.ds(pl.multiple_of(j * width, width), width)] = key

    @pl.when(j == n_pages // group - 1)
    def _():
        t = lax.broadcasted_iota(I32, (n_tok, PAGE_SIZE), 0)
        col = lax.broadcasted_iota(I32, (n_tok, PAGE_SIZE), 1)
        s_new = lax.dot_general(iq_ref[0], ikn_ref[0], (((1,), (1,)), ((), ())),
                                preferred_element_type=F32)
        key_new = jnp.where(col <= t, _sortable(scores(s_new)), jnp.int32(INT_MIN))
        keys_ref[:, n_pages * PAGE_SIZE:] = key_new
        t_abs = n_pages * PAGE_SIZE + lax.broadcasted_iota(I32, (n_tok, 1), 0)
        k_row = jnp.minimum(t_abs + 1, k_top)
        thr = _select_topk(keys_ref, 1, keys_ref.shape[1], k_row, n_index_bits)
        sel_ref[0] = (keys_ref[...] >= thr).astype(F32)


def _dsa_sample_kernel(pt_ref, q_ref, sel_ref, seln_ref, kn_ref, vn_ref, *rest, n_tok, n_pages, group):
    k_pages, v_pages, (o_ref, m_ref, l_ref) = rest[:group], rest[group:2 * group], rest[2 * group:]
    j = pl.program_id(1)
    rows = SA_HEADS * n_tok
    nt_dims = (((1,), (1,)), ((), ()))

    @pl.when(j == 0)
    def _():
        m_ref[...] = jnp.full_like(m_ref, NEG_BIG)
        l_ref[...] = jnp.zeros_like(l_ref)
        o_ref[...] = jnp.zeros_like(o_ref)

    def step(s, sel, pv):
        n = s.shape[-1]
        sel = jnp.broadcast_to(sel[None], (SA_HEADS, n_tok, n)).reshape(rows, n)
        s = jnp.where(sel > 0.5, s, NEG_BIG)
        m_old = m_ref[...]
        m_new = jnp.maximum(m_old, jnp.max(s, axis=1, keepdims=True))
        alpha = jnp.exp2(m_old - m_new)
        pr = jnp.exp2(s - m_new)
        l_ref[...] = alpha * l_ref[...] + jnp.sum(pr, axis=1, keepdims=True)
        o_ref[0] = alpha * o_ref[0] + pv(pr.astype(BF16))
        m_ref[...] = m_new

    kt = jnp.concatenate([p_ref[0].astype(BF16) for p_ref in k_pages], axis=1)
    vt = jnp.concatenate([p_ref[0].astype(BF16) for p_ref in v_pages], axis=1)
    step(jnp.dot(q_ref[0], kt, preferred_element_type=F32), sel_ref[0],
         lambda pr: lax.dot_general(pr, vt, nt_dims, preferred_element_type=F32))

    @pl.when(j == n_pages // group - 1)
    def _():
        step(lax.dot_general(q_ref[0], kn_ref[0], nt_dims, preferred_element_type=F32), seln_ref[0],
             lambda pr: jnp.dot(pr, vn_ref[0], preferred_element_type=F32))
        o_ref[0] = o_ref[0] / l_ref[...]


def _dsa_sample(aq, ak, av, iq, iw, ik, cache_k, cache_v, cache_ik, page_table):
    b, n_tok, _ = aq.shape
    n_pages = page_table.shape[1]
    n_keys = (n_pages + 1) * PAGE_SIZE
    k_top = min(TOP_K_MAX, (n_pages * PAGE_SIZE + n_tok) // 4)
    rows = SA_HEADS * n_tok
    per_b = lambda bi, j, pt: (bi, 0, 0)

    def page_specs(width, group):
        return [pl.BlockSpec((1, width, PAGE_SIZE), lambda bi, j, pt, i=i: (pt[bi, j * group + i], 0, 0))
                for i in range(group)]

    g_sel = _largest_divisor(n_pages, DSA_SEL_GROUP)
    iq_rows = iq.reshape(b, n_tok, IDX_HEADS, IDX_DIM).transpose(0, 2, 1, 3).reshape(b, rows, IDX_DIM)
    iw_rows = iw.transpose(0, 2, 1).reshape(b, rows, 1)
    sel = pl.pallas_call(
        functools.partial(_dsa_sel_kernel, n_tok=n_tok, n_pages=n_pages, group=g_sel, k_top=k_top,
                          n_index_bits=max(1, n_keys.bit_length())),
        grid_spec=pltpu.PrefetchScalarGridSpec(
            num_scalar_prefetch=1,
            grid=(b, n_pages // g_sel),
            in_specs=[pl.BlockSpec((1, rows, IDX_DIM), per_b),
                      pl.BlockSpec((1, rows, 1), per_b),
                      pl.BlockSpec((1, PAGE_SIZE, IDX_DIM), per_b)] + page_specs(IDX_DIM, g_sel),
            out_specs=pl.BlockSpec((1, n_tok, n_keys), per_b),
            scratch_shapes=[pltpu.VMEM((n_tok, n_keys), I32)]),
        out_shape=jax.ShapeDtypeStruct((b, n_tok, n_keys), F32),
        compiler_params=_cparams(("parallel", "arbitrary")),
        name="dsa_sel",
    )(page_table, iq_rows, iw_rows, _pad_rows(ik, PAGE_SIZE), *([cache_ik] * g_sel))

    g_att = _largest_divisor(n_pages, DSA_SAMPLE_GROUP)
    qbd = _block_diag_queries(aq, SA_HEADS, SA_KV_HEADS)
    acc = pl.pallas_call(
        functools.partial(_dsa_sample_kernel, n_tok=n_tok, n_pages=n_pages, group=g_att),
        grid_spec=pltpu.PrefetchScalarGridSpec(
            num_scalar_prefetch=1,
            grid=(b, n_pages // g_att),
            in_specs=[pl.BlockSpec((1, rows, SA_KV_WIDTH), per_b),
                      pl.BlockSpec((1, n_tok, g_att * PAGE_SIZE), lambda bi, j, pt: (bi, 0, j)),
                      pl.BlockSpec((1, n_tok, PAGE_SIZE), lambda bi, j, pt: (bi, 0, n_pages)),
                      pl.BlockSpec((1, PAGE_SIZE, SA_KV_WIDTH), per_b),
                      pl.BlockSpec((1, PAGE_SIZE, SA_KV_WIDTH), per_b)]
                     + page_specs(SA_KV_WIDTH, g_att) + page_specs(SA_KV_WIDTH, g_att),
            out_specs=pl.BlockSpec((1, rows, SA_KV_WIDTH), per_b),
            scratch_shapes=[pltpu.VMEM((rows, 1), F32), pltpu.VMEM((rows, 1), F32)]),
        out_shape=jax.ShapeDtypeStruct((b, rows, SA_KV_WIDTH), F32),
        compiler_params=_cparams(("parallel", "arbitrary")),
        name="dsa_sample",
    )(page_table, qbd, sel, sel, _pad_rows(ak, PAGE_SIZE), _pad_rows(av, PAGE_SIZE),
      *([cache_k] * g_att), *([cache_v] * g_att))
    return _take_block_diag(acc, n_tok, SA_HEADS, SA_KV_HEADS)


def _tail_kernel(x_ref, osb_ref, osa_ref, gate_ref, wsb_ref, wsa_ref, wo_ref, g2_ref, wup_ref,
                 wdn_ref, y_ref, *, ff_chunk):
    gate = gate_ref[...]
    merged = (gate[:, :D_MODEL] * jnp.dot(osb_ref[...], wsb_ref[...], preferred_element_type=F32)
              + gate[:, D_MODEL:] * jnp.dot(osa_ref[...], wsa_ref[...], preferred_element_type=F32))
    x1 = x_ref[...] + jnp.dot(merged.astype(BF16), wo_ref[...], preferred_element_type=F32)
    ms = jnp.mean(x1 * x1, axis=-1, keepdims=True)
    h = (x1 * lax.rsqrt(ms + RMS_EPS) * g2_ref[...]).astype(BF16)
    y = x1
    for c in range(D_FF // ff_chunk):
        sl = slice(c * ff_chunk, (c + 1) * ff_chunk)
        u = jnp.maximum(jnp.dot(h, wup_ref[:, sl], preferred_element_type=F32), 0.0)
        y = y + jnp.dot((u * u).astype(BF16), wdn_ref[sl, :], preferred_element_type=F32)
    y_ref[...] = y


def _tail(x2d, o_sb, o_sa, gates, w_sb_out, w_sa_out, w_o, g2, w_up, w_down, tm):
    r = x2d.shape[0]
    row = lambda w: pl.BlockSpec((tm, w), lambda i: (i, 0))
    res = lambda shape: pl.BlockSpec(shape, lambda i: (0, 0), pipeline_mode=pl.Buffered(1))
    return pl.pallas_call(
        functools.partial(_tail_kernel, ff_chunk=1024),
        grid=(r // tm,),
        in_specs=[row(D_MODEL), row(SB_WIDTH), row(SA_WIDTH), row(2 * D_MODEL),
                  res((SB_WIDTH, D_MODEL)), res((SA_WIDTH, D_MODEL)), res((D_MODEL, D_MODEL)),
                  res((1, D_MODEL)), res((D_MODEL, D_FF)), res((D_FF, D_MODEL))],
        out_specs=row(D_MODEL),
        out_shape=jax.ShapeDtypeStruct((r, D_MODEL), F32),
        compiler_params=_cparams(("parallel",)),
        name="tail",
    )(x2d, o_sb, o_sa, gates, w_sb_out, w_sa_out, w_o, g2.reshape(1, D_MODEL), w_up, w_down)


def _unpermute_heads(a):
    shp = a.shape
    a = a.reshape(shp[:-1] + (SA_HEADS, HEAD_DIM))
    return a[..., jnp.array(SA_Q_PERM), :].reshape(shp)


def kernel(x_prompt, x_sample, cache_sb_k, cache_sb_v, cache_sa_k, cache_sa_v, cache_idx_k, page_table,
           norm1_g, w_in, b_gate, q_norm_g, k_norm_g, idx_k_norm_g, w_sb_out, w_sa_out, w_o,
           norm2_g, w_up, w_down):
    depth = w_in.shape[0]
    batch, seq, _ = x_prompt.shape
    dec_batch, dec_seq, _ = x_sample.shape
    n_phys = cache_sb_k.shape[1]
    past_len = page_table.shape[1] * PAGE_SIZE
    assert seq % 512 == 0 and cache_sb_k.shape[2] == PAGE_SIZE
    pos_p = jnp.arange(seq, dtype=F32)
    pos_s = past_len + jnp.arange(dec_seq, dtype=F32)
    tm_p = 256
    tm_s = 256 if (dec_batch * dec_seq) % 256 == 0 else dec_batch * dec_seq

    xp = x_prompt.reshape(batch * seq, D_MODEL)
    xs = x_sample.reshape(dec_batch * dec_seq, D_MODEL)
    new_p = [[] for _ in range(5)]
    new_s = [[] for _ in range(5)]
    for layer in range(depth):
        w_packed = _pack_w_in(w_in[layer])
        w_sa_perm = _unpermute_heads(w_sa_out[layer].T).T.astype(BF16)
        tail_w = (w_sb_out[layer].astype(BF16), w_sa_perm, w_o[layer].astype(BF16), norm2_g[layer],
                  w_up[layer].astype(BF16), w_down[layer].astype(BF16))
        norm_args = (norm1_g[layer], q_norm_g[layer], k_norm_g[layer], idx_k_norm_g[layer], b_gate[layer])

        (*new_kv, avtb, iwt, sbq, sbk, sbv, saq, sak, sav, iq, ik2, gates) = _project(
            xp, pos_p, batch, w_packed, *norm_args, tm=tm_p, group_cols=seq)
        o_sb = _sb_prompt(sbq, sbk, sbv, batch, seq, tq=256)
        o_sa = _dsa_prompt(iq, iwt, saq, ik2, sak, avtb, batch, seq, tq=128, tk=512, nsub=2)
        xp = _tail(xp, o_sb, o_sa, gates, *tail_w, tm=256)
        for lst, arr in zip(new_p, new_kv):
            arr = jnp.moveaxis(arr.reshape(batch, -1, min(arr.shape[1], HEAD_DIM), seq), 3, 1)
            lst.append(arr if arr.shape[2] > 1 else arr[:, :, 0])

        n_s = dec_batch * dec_seq
        (*new_kv, avtb, iwt, sbq, sbk, sbv, saq, sak, sav, iq, ik2, gates) = _project(
            xs, pos_s, dec_batch, w_packed, *norm_args, tm=tm_s, group_cols=n_s)
        b3 = lambda a: a.reshape(dec_batch, dec_seq, a.shape[-1])
        iw = iwt[0].T
        o_sb = _sb_sample(b3(sbq), b3(sbk), b3(sbv),
                          _pages_t(cache_sb_k[layer]), _pages_t(cache_sb_v[layer]), page_table)
        o_sa = _dsa_sample(_unpermute_heads(b3(saq)), b3(sak), b3(sav), b3(iq), b3(iw),
                           b3(ik2)[..., :IDX_DIM],
                           _pages_t(cache_sa_k[layer]), _pages_t(cache_sa_v[layer]),
                           _pages_t(cache_idx_k[layer]), page_table)
        o_sb = o_sb.reshape(dec_batch * dec_seq, SB_WIDTH).astype(BF16)
        o_sa = _unpermute_heads(o_sa).reshape(dec_batch * dec_seq, SA_WIDTH).astype(BF16)
        xs = _tail(xs, o_sb, o_sa, gates, *tail_w, tm=tm_s)
        for lst, arr in zip(new_s, new_kv):
            arr = jnp.moveaxis(arr.reshape(-1, min(arr.shape[1], HEAD_DIM), dec_batch, dec_seq), (2, 3), (0, 1))
            lst.append(arr if arr.shape[2] > 1 else arr[:, :, 0])

    return (xp.reshape(batch, seq, D_MODEL), xs.reshape(dec_batch, dec_seq, D_MODEL),
            *(jnp.stack(l) for l in new_p), *(jnp.stack(l) for l in new_s))
```

```python
import functools

import jax
import jax.numpy as jnp
from jax import lax
from jax.experimental import pallas as pl
from jax.experimental.pallas import tpu as pltpu

F32 = jnp.float32
BF16 = jnp.bfloat16
I32 = jnp.int32

D_MODEL = 1024
HEAD_DIM = 64
SB_HEADS = 8
SA_HEADS = 8
SA_KV_HEADS = 4
IDX_HEADS = 8
IDX_DIM = 64
SB_WIDTH = SB_HEADS * HEAD_DIM
SA_WIDTH = SA_HEADS * HEAD_DIM
SA_KV_WIDTH = SA_KV_HEADS * HEAD_DIM
IDX_WIDTH = IDX_HEADS * IDX_DIM
TOP_K_MAX = 256
ROT_DIM = HEAD_DIM // 4
ROPE_THETA = 500000.0
D_FF = 4 * D_MODEL
PAGE_SIZE = 128
RMS_EPS = 1e-6
QK_SCALE = HEAD_DIM ** -0.5
SA_Q_SCALE = QK_SCALE * 1.4426950408889634
IDX_SCALE = IDX_DIM ** -0.5
IDX_W_SCALE = IDX_HEADS ** -0.5

LANES = 128
VMEM_LIMIT = 56 * 1024 * 1024

C_SQ, C_SK, C_SV = 0, 512, 1024
C_AQ, C_AK, C_AV = 1536, 2048, 2304
C_IQ, C_IK2, C_IW = 2560, 3072, 3200
C_GATE = 3328
C_END = C_GATE + 2 * D_MODEL
SA_Q_PERM = (0, 2, 1, 3, 4, 6, 5, 7)

SURV_FLOOR = -105.0
NEG_BIG = -1e30
INT_MIN = -2 ** 31

SB_SAMPLE_HEAD_PAGES = 4
DSA_SEL_GROUP = 32
DSA_SAMPLE_GROUP = 16


TOPK_GROUPS = 256


def _largest_divisor(n, cap):
    return max(d for d in range(1, min(n, cap) + 1) if n % d == 0)


def _cparams(sem, vmem=VMEM_LIMIT):
    return pltpu.CompilerParams(dimension_semantics=sem, vmem_limit_bytes=vmem)


def _rope_chunk(y, cos, sin_a, sin_b):
    up = pltpu.roll(y, LANES - ROT_DIM // 2, 1)
    dn = pltpu.roll(y, ROT_DIM // 2, 1)
    return y * cos + up * sin_a + dn * sin_b


def _head_rms_chunk(y, e_mat, gain):
    y2 = y * y
    hi = y2.astype(BF16)
    lo = (y2 - hi.astype(F32)).astype(BF16)
    ms = (jnp.dot(hi, e_mat, preferred_element_type=F32)
          + jnp.dot(lo, e_mat, preferred_element_type=F32))
    return y * lax.rsqrt(ms + RMS_EPS) * gain


def _proj_kernel(x_ref, g1_ref, w_ref, qg_ref, kg_ref, ig_ref, bg_ref, cos_ref, sa_ref, sb_ref,
                 e_ref,
                 skt_ref, svt_ref, akt_ref, avt_ref, ikt_ref, avtb_ref, iwt_ref,
                 sbq_ref, sbk_ref, sbv_ref, saq_ref, sak_ref, sav_ref, iq_ref, ik2_ref, gate_ref):
    x = x_ref[...]
    ms = jnp.mean(x * x, axis=-1, keepdims=True)
    h = (x * lax.rsqrt(ms + RMS_EPS) * g1_ref[...]).astype(BF16)

    def mm(lo, hi):
        return jnp.dot(h, w_ref[:, lo:hi], preferred_element_type=F32)

    cos, sin_a, sin_b = cos_ref[...], sa_ref[...], sb_ref[...]
    e_mat = e_ref[...]

    p = mm(C_SQ, C_SK)
    sbq_ref[...] = (p * QK_SCALE).astype(BF16)
    p = mm(C_SK, C_SV)
    skt_ref[0] = p.T
    sbk_ref[...] = p.astype(BF16)
    p = mm(C_SV, C_AQ)
    svt_ref[0] = p.T
    sbv_ref[...] = p.astype(BF16)

    p = mm(C_AQ, C_AK)
    for c in range(SA_WIDTH // LANES):
        sl = slice(c * LANES, (c + 1) * LANES)
        y = _rope_chunk(_head_rms_chunk(p[:, sl], e_mat, qg_ref[...]), cos, sin_a, sin_b)
        saq_ref[:, sl] = (y * SA_Q_SCALE).astype(BF16)
    p = mm(C_AK, C_AV)
    for c in range(SA_KV_WIDTH // LANES):
        sl = slice(c * LANES, (c + 1) * LANES)
        y = _rope_chunk(_head_rms_chunk(p[:, sl], e_mat, kg_ref[...]), cos, sin_a, sin_b)
        akt_ref[0, sl, :] = y.T
        sak_ref[:, sl] = y.astype(BF16)
    p = mm(C_AV, C_IQ)
    pt = p.T
    avt_ref[0] = pt
    avtb_ref[0] = pt.astype(BF16)
    sav_ref[...] = p.astype(BF16)

    p = mm(C_IQ, C_IK2)
    for c in range(IDX_WIDTH // LANES):
        sl = slice(c * LANES, (c + 1) * LANES)
        iq_ref[:, sl] = (_rope_chunk(p[:, sl], cos, sin_a, sin_b) * IDX_SCALE).astype(BF16)
    p = mm(C_IK2, C_GATE)
    y = _rope_chunk(_head_rms_chunk(p[:, :LANES], e_mat, ig_ref[...]), cos, sin_a, sin_b)
    ikt_ref[0] = y.T[:IDX_DIM]
    ik2_ref[...] = y.astype(BF16)
    iwt_ref[0] = p[:, LANES:].T[:IDX_HEADS] * IDX_W_SCALE

    p = mm(C_GATE, C_END) + bg_ref[...]
    gate_ref[...] = jax.nn.sigmoid(p)


def _rope_tables(pos):
    half = ROT_DIM // 2
    inv = 1.0 / (ROPE_THETA ** (jnp.arange(0, ROT_DIM, 2, dtype=F32) / ROT_DIM))
    ang = pos[:, None] * inv[None, :]
    cos, sin = jnp.cos(ang), jnp.sin(ang)
    t = pos.shape[0]
    zeros = jnp.zeros((t, HEAD_DIM - ROT_DIM), F32)
    zh = jnp.zeros((t, half), F32)
    cos_h = jnp.concatenate([cos, cos, zeros + 1.0], axis=1)
    sa_h = jnp.concatenate([-sin, zh, zeros], axis=1)
    sb_h = jnp.concatenate([zh, sin, zeros], axis=1)
    return tuple(jnp.concatenate([a, a], axis=1) for a in (cos_h, sa_h, sb_h))


def _pack_w_in(w_in):
    o = 0
    parts = []
    for n in (SB_WIDTH, SB_WIDTH, SB_WIDTH, SA_WIDTH, SA_KV_WIDTH, SA_KV_WIDTH, IDX_WIDTH,
              IDX_DIM, IDX_HEADS, 2 * D_MODEL):
        parts.append(w_in[:, o:o + n])
        o += n
    sq, sk, sv, aq, ak, av, iq, ik, iw, gates = parts
    aq = aq.reshape(D_MODEL, SA_HEADS, HEAD_DIM)[:, jnp.array(SA_Q_PERM)].reshape(D_MODEL, SA_WIDTH)
    pad = jnp.zeros((D_MODEL, C_GATE - C_IW - IDX_HEADS), w_in.dtype)
    return jnp.concatenate([sq, sk, sv, aq, ak, av, iq, ik, ik, iw, pad, gates], axis=1).astype(BF16)


def _project(x2d, pos, n_rep, w_packed, g1, qg, kg, ig, bg, tm, group_cols):
    r = x2d.shape[0]
    t = pos.shape[0]
    assert r == n_rep * t and r % tm == 0 and (t % tm == 0 or tm % t == 0)
    assert r % group_cols == 0 and group_cols % tm == 0
    ntc = group_cols // tm
    cos, sin_a, sin_b = _rope_tables(pos)
    if tm > t:
        cos, sin_a, sin_b = (jnp.tile(a, (tm // t, 1)) for a in (cos, sin_a, sin_b))
    nt = cos.shape[0] // tm
    e_mat = (jnp.arange(LANES)[:, None] // HEAD_DIM == jnp.arange(LANES)[None, :] // HEAD_DIM)
    e_mat = (e_mat.astype(F32) / HEAD_DIM).astype(BF16)

    def row(width):
        return pl.BlockSpec((tm, width), lambda i: (i, 0))

    def const(shape):
        return pl.BlockSpec(shape, lambda i: (0, 0))

    tab = pl.BlockSpec((tm, LANES), lambda i: (i % nt, 0))
    tile2 = lambda g: jnp.concatenate([g, g]).reshape(1, LANES)
    out_heights = [(SB_WIDTH, F32), (SB_WIDTH, F32), (SA_KV_WIDTH, F32), (SA_KV_WIDTH, F32),
                   (IDX_DIM, F32), (SA_KV_WIDTH, BF16), (IDX_HEADS, F32)]
    out_widths = [(SB_WIDTH, BF16), (SB_WIDTH, BF16), (SB_WIDTH, BF16),
                  (SA_WIDTH, BF16), (SA_KV_WIDTH, BF16), (SA_KV_WIDTH, BF16),
                  (IDX_WIDTH, BF16), (LANES, BF16), (2 * D_MODEL, F32)]
    return pl.pallas_call(
        _proj_kernel,
        grid=(r // tm,),
        in_specs=[row(D_MODEL), const((1, D_MODEL)),
                  pl.BlockSpec((D_MODEL, C_END), lambda i: (0, 0), pipeline_mode=pl.Buffered(1)),
                  const((1, LANES)), const((1, LANES)), const((1, LANES)), const((1, 2 * D_MODEL)),
                  tab, tab, tab, const((LANES, LANES))],
        out_specs=[pl.BlockSpec((1, h_, tm), lambda i: (i // ntc, 0, i % ntc)) for h_, _ in out_heights]
                  + [row(w_) for w_, _ in out_widths],
        out_shape=[jax.ShapeDtypeStruct((r // group_cols, h_, group_cols), dt) for h_, dt in out_heights]
                  + [jax.ShapeDtypeStruct((r, w_), dt) for w_, dt in out_widths],
        compiler_params=_cparams(("parallel",)),
        name="proj",
    )(x2d, g1.reshape(1, D_MODEL), w_packed, tile2(qg), tile2(kg), tile2(ig),
      bg.reshape(1, 2 * D_MODEL), cos, sin_a, sin_b, e_mat)


def _stick_break_tile(z, mask, surv, tri):
    t = jnp.log1p(jnp.exp(-jnp.abs(z)))
    l = -(jnp.maximum(z, 0.0) + t)
    if mask is not None:
        l = jnp.where(mask, l, 0.0)
    ls = jnp.minimum(z, 0.0) - t
    hi = l.astype(BF16)
    lo = (l - hi.astype(F32)).astype(BF16)
    ex = jnp.dot(hi, tri, preferred_element_type=F32) + jnp.dot(lo, tri, preferred_element_type=F32)
    a = jnp.exp(ls + ex + surv)
    if mask is not None:
        a = jnp.where(mask, a, 0.0)
    return a, surv + jnp.sum(l, axis=1, keepdims=True)


def _newer_tri(n):
    j = lax.broadcasted_iota(I32, (n, n), 0)
    s = lax.broadcasted_iota(I32, (n, n), 1)
    return (j > s).astype(BF16)


def _sb_prompt_kernel(q_ref, k_ref, v_ref, o_ref, acc_ref, surv_ref, *, tq):
    i = pl.program_id(2)
    t0 = i * tq
    lane = lax.broadcasted_iota(I32, (tq, LANES), 1)
    row_t = t0 + lax.broadcasted_iota(I32, (tq, tq), 0)
    col0 = lax.broadcasted_iota(I32, (tq, tq), 1)
    tri = _newer_tri(tq)
    q = q_ref[...]
    heads = range(2)
    qm = [jnp.where((lane >= HEAD_DIM) == bool(h), q, jnp.zeros_like(q)) for h in heads]
    acc_ref[...] = jnp.zeros_like(acc_ref)
    surv_ref[...] = jnp.zeros_like(surv_ref)

    def body(carry):
        kt, _ = carry
        start = pl.multiple_of(kt * tq, tq)
        kb = k_ref[pl.ds(start, tq), :]
        vb = v_ref[pl.ds(start, tq), :]
        mask = (start + col0) < row_t
        z = [lax.dot_general(qm[h], kb, (((1,), (1,)), ((), ())), preferred_element_type=F32)
             for h in heads]
        l, ls, ex = [], [], []
        for h in heads:
            t = jnp.log1p(jnp.exp(-jnp.abs(z[h])))
            l.append(jnp.where(mask, -(jnp.maximum(z[h], 0.0) + t), 0.0))
            ls.append(jnp.minimum(z[h], 0.0) - t)
            hi = l[h].astype(BF16)
            lo = (l[h] - hi.astype(F32)).astype(BF16)
            ex.append(jnp.dot(hi, tri, preferred_element_type=F32)
                      + jnp.dot(lo, tri, preferred_element_type=F32))
        acc, surv = [], []
        for h in heads:
            a = jnp.where(mask, jnp.exp(ls[h] + ex[h] + surv_ref[h]), 0.0)
            acc.append(acc_ref[h] + jnp.dot(a.astype(BF16), vb, preferred_element_type=F32))
            surv.append(surv_ref[h] + jnp.sum(l[h], axis=1, keepdims=True))
        acc_ref[...] = jnp.stack(acc)
        surv_ref[...] = jnp.stack(surv)
        return kt - 1, jnp.maximum(jnp.max(surv[0]), jnp.max(surv[1]))

    lax.while_loop(lambda c: (c[0] >= 0) & (c[1] > SURV_FLOOR), body, (i, jnp.float32(0.0)))
    o_ref[...] = jnp.where(lane < HEAD_DIM, acc_ref[0], acc_ref[1]).astype(o_ref.dtype)


def _sb_prompt(q, k, v, batch, seq, tq):
    nq = seq // tq
    n_chunks = SB_WIDTH // LANES
    qspec = pl.BlockSpec((tq, LANES), lambda b, c, i: (b * nq + i, c))
    kvspec = pl.BlockSpec((seq, LANES), lambda b, c, i: (b, c))
    return pl.pallas_call(
        functools.partial(_sb_prompt_kernel, tq=tq),
        grid=(batch, n_chunks, nq),
        in_specs=[qspec, kvspec, kvspec],
        out_specs=qspec,
        out_shape=jax.ShapeDtypeStruct((batch * seq, SB_WIDTH), BF16),
        scratch_shapes=[pltpu.VMEM((2, tq, LANES), F32), pltpu.VMEM((2, tq, 1), F32)],
        compiler_params=_cparams(("parallel", "parallel", "arbitrary")),
        name="sb_prompt",
    )(q, k, v)


def _sortable(x):
    bits = lax.bitcast_convert_type(x, I32)
    key = jnp.where(bits < 0, bits ^ jnp.int32(0x7FFFFFFF), bits)
    return jnp.where(x == 0.0, jnp.int32(0), key)


def _select_topk(keys_ref, n_tiles, tk, k_row, n_index_bits, key_axis=1):
    qn = keys_ref.shape[1 - key_axis]
    sign = jnp.int32(INT_MIN)
    tile_shape = (qn, tk) if key_axis == 1 else (tk, qn)
    fold = LANES if key_axis == 1 else 8
    vec = (qn, 1) if key_axis == 1 else (1, qn)

    def tile_at(kt):
        start = pl.multiple_of(kt * tk, tk)
        idx = (slice(None), pl.ds(start, tk)) if key_axis == 1 else (pl.ds(start, tk), slice(None))
        return start, idx

    def count(pred):
        def tile_body(kt, acc):
            start, idx = tile_at(kt)
            m = pred(keys_ref[idx], start).astype(I32)
            if key_axis == 1:
                for c in range(tk // fold):
                    acc = acc + m[:, c * fold:(c + 1) * fold]
                return acc
            return acc + jnp.sum(m.reshape(tk // fold, fold, qn), axis=0)
        acc0 = jnp.zeros((qn, fold) if key_axis == 1 else (fold, qn), I32)
        acc = lax.fori_loop(0, n_tiles, tile_body, acc0)
        if key_axis == 0:
            for shift in (4, 2, 1):
                acc = acc + pltpu.roll(acc, shift, 0)
            return acc[:1]
        return jnp.sum(acc.astype(F32), axis=1, keepdims=True).astype(I32)

    bracket = None
    if key_axis == 0 and tk % TOPK_GROUPS == 0:
        def gmax_body(kt, acc):
            _, idx = tile_at(kt)
            return jnp.maximum(acc, jnp.max(keys_ref[idx].reshape(tk // TOPK_GROUPS, TOPK_GROUPS, qn), axis=0))
        gmax = lax.fori_loop(0, n_tiles, gmax_body, jnp.full((TOPK_GROUPS, qn), INT_MIN, I32))
        gmax = gmax.reshape(TOPK_GROUPS // fold, fold, qn)
        hi_b, lo_b = jnp.max(gmax, axis=0), jnp.min(gmax, axis=0)
        for shift in (4, 2, 1):
            hi_b = jnp.maximum(hi_b, pltpu.roll(hi_b, shift, 0))
            lo_b = jnp.minimum(lo_b, pltpu.roll(lo_b, shift, 0))
        bracket = (lo_b[:1], hi_b[:1])

    zeros = jnp.zeros(vec, I32)
    b_first, base, span = jnp.int32(31), zeros, zeros - 1
    if bracket is not None:
        base = bracket[0] ^ sign
        span = (bracket[1] ^ sign) - base
        expo = lax.shift_right_logical(lax.bitcast_convert_type(jnp.maximum(span, 1).astype(F32), I32),
                                       jnp.int32(23)) - 127
        top = jnp.where(span < 0, 31, expo)
        b_first = jnp.minimum(jnp.max(top.astype(F32)).astype(I32), 31)

    def one_bit(b, prefix, thr, done):
        bit = jnp.where(b >= 0, lax.shift_left(jnp.int32(1), jnp.maximum(b, 0)), 0)
        cand_off = prefix | bit
        in_span = (cand_off ^ sign) <= (span ^ sign)
        cand = (base + cand_off) ^ sign
        cnt = count(lambda kk, start: kk >= cand)
        live = (done == 0) & in_span
        prefix = jnp.where(live & (cnt >= k_row), cand_off, prefix)
        hit = live & (cnt == k_row)
        return prefix, jnp.where(hit, cand, thr), jnp.where(hit, 1, done)

    def bit_body(carry):
        b, prefix, thr, done, _ = carry
        for u in range(2):
            prefix, thr, done = one_bit(b - u, prefix, thr, done)
        return b - 2, prefix, thr, done, jnp.min(done.astype(F32)).astype(I32)

    _, prefix, thr, done, all_done = lax.while_loop(
        lambda c: (c[0] >= 0) & (c[4] == 0), bit_body,
        (b_first, zeros, zeros, zeros, jnp.int32(0)))
    thr = jnp.where(done == 0, (base + prefix) ^ sign, thr)

    if key_axis == 0:
        @pl.when(all_done == 0)
        def _():
            tied = done == 0
            want = (k_row - count(lambda kk, start: kk > thr)).astype(F32)
            half = tk // 2
            r = lax.broadcasted_iota(I32, (half, half), 0)
            c = lax.broadcasted_iota(I32, (half, half), 1)
            tri = (c <= r).astype(BF16)

            def fix_body(kt, seen):
                start, _ = tile_at(kt)
                for u in range(2):
                    idx = (pl.ds(pl.multiple_of(start + u * half, half), half), slice(None))
                    kk = keys_ref[idx]
                    tie = tied & (kk == thr)
                    rank = seen + jnp.dot(tri, tie.astype(BF16), preferred_element_type=F32)
                    keys_ref[idx] = jnp.where(tie & (rank > want), thr - 1, kk)
                    seen = rank[half - 1:half]
                return seen
            lax.fori_loop(0, n_tiles, fix_body, jnp.zeros(vec, F32))
        return thr

    @pl.when(all_done == 0)
    def _():
        tied = done == 0
        want = k_row - count(lambda kk, start: kk > thr)
        col0 = lax.broadcasted_iota(I32, tile_shape, key_axis)

        def ties_before(x):
            return count(lambda kk, start: (kk == thr) & ((start + col0) < x))

        def idx_body(j, x):
            cand = x | lax.shift_left(jnp.int32(1), n_index_bits - 1 - j)
            return jnp.where(ties_before(cand) < want, cand, x)

        x = lax.fori_loop(0, n_index_bits, idx_body, zeros)

        def fix_body(kt, c):
            start, idx = tile_at(kt)
            kk = keys_ref[idx]
            drop = tied & (kk == thr) & ((start + col0) > x)
            keys_ref[idx] = jnp.where(drop, thr - 1, kk)
            return c
        lax.fori_loop(0, n_tiles, fix_body, 0)

    return thr


def _dsa_prompt_kernel(iq_ref, iwt_ref, aq_ref, ik_ref, ak_ref, avt_ref, o_ref,
                       keys_ref, iqt_ref, aqt_ref, m_ref, l_ref, acc_ref,
                       *, tq, tk, nsub, k_top, n_index_bits):
    assert tq == LANES
    i = pl.program_id(1)
    t0 = i * tq
    n_super = (t0 + tq + nsub * tk - 1) // (nsub * tk)
    n_tiles = nsub * n_super
    dim_row = lax.broadcasted_iota(I32, (LANES, tq), 0)
    halfmask = [(dim_row >= HEAD_DIM) == bool(e) for e in range(2)]
    t_query = t0 + lax.broadcasted_iota(I32, (1, tq), 1)

    def chunk_t(ref, c):
        return ref[:, c * LANES:(c + 1) * LANES].astype(F32).T

    for c in range(IDX_WIDTH // LANES):
        qt = chunk_t(iq_ref, c)
        for e in range(2):
            iqt_ref[c, :, e * tq:(e + 1) * tq] = jnp.where(halfmask[e], qt, 0.0).astype(BF16)

    def score_body(ks, c):
        starts = [pl.multiple_of((ks * nsub + u) * tk, tk) for u in range(nsub)]
        dots = [[jnp.dot(ik_ref[pl.ds(st, tk), :], iqt_ref[c2], preferred_element_type=F32)
                 for c2 in range(IDX_WIDTH // LANES)] for st in starts]
        for u, st in enumerate(starts):
            score = jnp.zeros((tk, tq), F32)
            for c2 in range(IDX_WIDTH // LANES):
                for e in range(2):
                    h = 2 * c2 + e
                    s = dots[u][c2][:, e * tq:(e + 1) * tq]
                    score = score + jnp.maximum(s, 0.0) * iwt_ref[0, h:h + 1, :]
            pos = st + lax.broadcasted_iota(I32, (tk, tq), 0)
            keys_ref[pl.ds(st, tk), :] = jnp.where(pos <= t_query, _sortable(score), jnp.int32(INT_MIN))
        return c
    lax.fori_loop(0, n_super, score_body, 0)

    k_row = jnp.minimum(t_query + 1, k_top)
    thr = _select_topk(keys_ref, (t0 + tq + tk - 1) // tk, tk, k_row, n_index_bits, key_axis=0)

    for c in range(SA_WIDTH // LANES):
        qt = chunk_t(aq_ref, c)
        p, j = c // 2, c % 2
        for e in range(2):
            aqt_ref[2 * p + e, :, j * tq:(j + 1) * tq] = jnp.where(halfmask[e], qt, 0.0).astype(BF16)
    m_ref[...] = jnp.full_like(m_ref, NEG_BIG)
    l_ref[...] = jnp.zeros_like(l_ref)
    acc_ref[...] = jnp.zeros_like(acc_ref)

    ones_rows = jnp.ones((16, tk), BF16)

    def att_body(ks, c):
        starts = [pl.multiple_of((ks * nsub + u) * tk, tk) for u in range(nsub)]
        logits = [[jnp.dot(ak_ref[pl.ds(st, tk), (g // 2) * LANES:(g // 2 + 1) * LANES], aqt_ref[g],
                           preferred_element_type=F32) for g in range(SA_KV_HEADS)]
                  for st in starts]
        m_all, l_all = m_ref[...], l_ref[...]
        m_cur = [m_all[hh:hh + 1] for hh in range(SA_HEADS)]
        l_cur = [l_all[hh:hh + 1] for hh in range(SA_HEADS)]
        acc_cur = [acc_ref[hh * HEAD_DIM:(hh + 1) * HEAD_DIM] for hh in range(SA_HEADS)]
        for u, st in enumerate(starts):
            sel = keys_ref[pl.ds(st, tk), :] >= thr
            probs, alphas = [], []
            for hh in range(SA_HEADS):
                g, j = hh // 2, hh % 2
                s = jnp.where(sel, logits[u][g][:, j * tq:(j + 1) * tq], NEG_BIG)
                m_new = jnp.maximum(m_cur[hh], jnp.max(s, axis=0, keepdims=True))
                alpha = jnp.exp2(m_cur[hh] - m_new)
                m_cur[hh] = m_new
                probs.append(jnp.exp2(s - m_new).astype(BF16))
                alphas.append(alpha)
            for hh in range(SA_HEADS):
                g = hh // 2
                vt = avt_ref[0, g * HEAD_DIM:(g + 1) * HEAD_DIM, pl.ds(st, tk)]
                pv = jnp.dot(jnp.concatenate([vt, ones_rows], axis=0), probs[hh],
                             preferred_element_type=F32)
                acc_cur[hh] = alphas[hh] * acc_cur[hh] + pv[:HEAD_DIM]
                l_cur[hh] = alphas[hh] * l_cur[hh] + pv[HEAD_DIM:HEAD_DIM + 1]
        m_ref[...] = jnp.concatenate(m_cur, axis=0)
        l_ref[...] = jnp.concatenate(l_cur, axis=0)
        acc_ref[...] = jnp.concatenate(acc_cur, axis=0)
        return c
    lax.fori_loop(0, n_super, att_body, 0)

    def head_out(hh):
        return acc_ref[hh * HEAD_DIM:(hh + 1) * HEAD_DIM] / l_ref[hh:hh + 1]

    for p in range(SA_KV_WIDTH // LANES):
        for j in range(2):
            ot = jnp.concatenate([head_out(2 * (2 * p) + j), head_out(2 * (2 * p + 1) + j)], axis=0)
            o_ref[:, (2 * p + j) * LANES:(2 * p + j + 1) * LANES] = ot.T.astype(o_ref.dtype)


def _dsa_prompt(iq, iwt, aq, ik2, ak, avt, batch, seq, tq, tk, nsub):
    assert seq % (nsub * tk) == 0
    nq = seq // tq
    k_top = min(TOP_K_MAX, seq // 4)
    n_index_bits = max(1, seq.bit_length())
    qspec = lambda w: pl.BlockSpec((tq, w), lambda b, i: (b * nq + i, 0))
    kspec = lambda w: pl.BlockSpec((seq, w), lambda b, i: (b, 0), pipeline_mode=pl.Buffered(1))
    return pl.pallas_call(
        functools.partial(_dsa_prompt_kernel, tq=tq, tk=tk, nsub=nsub, k_top=k_top,
                          n_index_bits=n_index_bits),
        grid=(batch, nq),
        in_specs=[qspec(IDX_WIDTH),
                  pl.BlockSpec((1, IDX_HEADS, tq), lambda b, i: (b, 0, i)),
                  qspec(SA_WIDTH), kspec(LANES), kspec(SA_KV_WIDTH),
                  pl.BlockSpec((1, SA_KV_WIDTH, seq), lambda b, i: (b, 0, 0),
                               pipeline_mode=pl.Buffered(1))],
        out_specs=qspec(SA_WIDTH),
        out_shape=jax.ShapeDtypeStruct((batch * seq, SA_WIDTH), BF16),
        scratch_shapes=[pltpu.VMEM((seq, tq), I32),
                        pltpu.VMEM((IDX_HEADS // 2, LANES, 2 * tq), BF16),
                        pltpu.VMEM((SA_KV_HEADS, LANES, 2 * tq), BF16),
                        pltpu.VMEM((SA_HEADS, tq), F32), pltpu.VMEM((SA_HEADS, tq), F32),
                        pltpu.VMEM((SA_HEADS * HEAD_DIM, tq), F32)],
        compiler_params=_cparams(("parallel", "arbitrary")),
        name="dsa_prompt",
    )(iq, iwt, aq, ik2, ak, avt)


def _sb_page_step(q, kc_ref, vc_ref, o_ref, surv_ref, tri):
    @pl.when(jnp.max(surv_ref[0]) > SURV_FLOOR)
    def _():
        kt = kc_ref[0].astype(BF16)
        vt = vc_ref[0].astype(BF16)
        z = jnp.dot(q, kt, preferred_element_type=F32)
        a, surv = _stick_break_tile(z, None, surv_ref[0][:, :1], tri)
        o_ref[0] += lax.dot_general(a.astype(BF16), vt, (((1,), (1,)), ((), ())),
                                    preferred_element_type=F32)
        surv_ref[0] = jnp.broadcast_to(surv, surv_ref.shape[1:])


def _sb_sample_head_kernel(pt_ref, q_ref, kn_ref, vn_ref, *rest, n_tok, n_head_pages):
    pages, (o_ref, surv_ref) = rest[:2 * n_head_pages], rest[2 * n_head_pages:]
    rows = q_ref.shape[1]
    q = q_ref[0]
    tri = _newer_tri(PAGE_SIZE)
    z = lax.dot_general(q, kn_ref[0], (((1,), (1,)), ((), ())), preferred_element_type=F32)
    t = lax.broadcasted_iota(I32, (rows, PAGE_SIZE), 0) % n_tok
    mask = lax.broadcasted_iota(I32, (rows, PAGE_SIZE), 1) < t
    a, surv = _stick_break_tile(z, mask, jnp.zeros((rows, 1), F32), tri)
    o_ref[0] = jnp.dot(a.astype(BF16), vn_ref[0], preferred_element_type=F32)
    surv_ref[0] = jnp.broadcast_to(surv, surv_ref.shape[1:])
    for i in range(n_head_pages):
        _sb_page_step(q, pages[2 * i], pages[2 * i + 1], o_ref, surv_ref, tri)


def _sb_sample_tail_kernel(pt_ref, q_ref, acc_ref, sin_ref, kc_ref, vc_ref, o_ref, surv_ref):
    @pl.when(pl.program_id(1) == 0)
    def _():
        o_ref[...] = acc_ref[...]
        surv_ref[...] = sin_ref[...]
    _sb_page_step(q_ref[0], kc_ref, vc_ref, o_ref, surv_ref, _newer_tri(PAGE_SIZE))


def _block_diag_queries(q, n_heads_q, n_heads_kv):
    group = n_heads_q // n_heads_kv
    blocks = []
    for h in range(n_heads_q):
        kv = h // group
        qh = q[:, :, h * HEAD_DIM:(h + 1) * HEAD_DIM]
        blocks.append(jnp.pad(qh, ((0, 0), (0, 0), (kv * HEAD_DIM, (n_heads_kv - 1 - kv) * HEAD_DIM))))
    return jnp.concatenate(blocks, axis=1)


def _take_block_diag(acc, n_tok, n_heads_q, n_heads_kv):
    group = n_heads_q // n_heads_kv
    return jnp.concatenate(
        [acc[:, h * n_tok:(h + 1) * n_tok, (h // group) * HEAD_DIM:(h // group + 1) * HEAD_DIM]
         for h in range(n_heads_q)], axis=2)


def _pages_t(cache):
    n_phys, page = cache.shape[:2]
    return jnp.moveaxis(cache, 1, -1).reshape(n_phys, -1, page)


def _pad_rows(a, rows):
    return jnp.pad(a, ((0, 0), (0, rows - a.shape[1]), (0, 0)))


def _sb_sample(q, k, v, cache_k, cache_v, page_table):
    b, n_tok, _ = q.shape
    n_pages = page_table.shape[1]
    rows = SB_HEADS * n_tok
    qbd = _block_diag_queries(q, SB_HEADS, SB_HEADS)
    kn, vn = _pad_rows(k, PAGE_SIZE), _pad_rows(v, PAGE_SIZE)
    n_head = min(SB_SAMPLE_HEAD_PAGES, n_pages)
    out_shape = [jax.ShapeDtypeStruct((b, rows, SB_WIDTH), F32),
                 jax.ShapeDtypeStruct((b, rows, LANES), F32)]

    per_b1 = lambda bi, pt: (bi, 0, 0)
    page_specs = []
    for i in range(n_head):
        spec = pl.BlockSpec((1, SB_WIDTH, PAGE_SIZE), lambda bi, pt, i=i: (pt[bi, n_pages - 1 - i], 0, 0))
        page_specs += [spec, spec]
    acc, surv = pl.pallas_call(
        functools.partial(_sb_sample_head_kernel, n_tok=n_tok, n_head_pages=n_head),
        grid_spec=pltpu.PrefetchScalarGridSpec(
            num_scalar_prefetch=1,
            grid=(b,),
            in_specs=[pl.BlockSpec((1, rows, SB_WIDTH), per_b1),
                      pl.BlockSpec((1, PAGE_SIZE, SB_WIDTH), per_b1),
                      pl.BlockSpec((1, PAGE_SIZE, SB_WIDTH), per_b1)] + page_specs,
            out_specs=[pl.BlockSpec((1, rows, SB_WIDTH), per_b1), pl.BlockSpec((1, rows, LANES), per_b1)]),
        out_shape=out_shape,
        compiler_params=_cparams(("parallel",)),
        name="sb_sample_head",
    )(page_table, qbd, kn, vn, *([cache_k, cache_v] * n_head))

    n_tail = n_pages - n_head
    if n_tail > 0:
        per_b = lambda bi, j, pt: (bi, 0, 0)
        page = lambda bi, j, pt: (pt[bi, n_tail - 1 - j], 0, 0)

        def older_pages(acc, surv):
            return pl.pallas_call(
                _sb_sample_tail_kernel,
                grid_spec=pltpu.PrefetchScalarGridSpec(
                    num_scalar_prefetch=1,
                    grid=(b, n_tail),
                    in_specs=[pl.BlockSpec((1, rows, SB_WIDTH), per_b),
                              pl.BlockSpec((1, rows, SB_WIDTH), per_b),
                              pl.BlockSpec((1, rows, LANES), per_b),
                              pl.BlockSpec((1, SB_WIDTH, PAGE_SIZE), page),
                              pl.BlockSpec((1, SB_WIDTH, PAGE_SIZE), page)],
                    out_specs=[pl.BlockSpec((1, rows, SB_WIDTH), per_b),
                               pl.BlockSpec((1, rows, LANES), per_b)]),
                out_shape=out_shape,
                compiler_params=_cparams(("parallel", "arbitrary")),
                name="sb_sample_tail",
            )(page_table, qbd, acc, surv, cache_k, cache_v)[0]

        acc = lax.cond(jnp.max(surv) > SURV_FLOOR, older_pages, lambda acc, surv: acc, acc, surv)
    return _take_block_diag(acc, n_tok, SB_HEADS, SB_HEADS)


def _dsa_sel_kernel(pt_ref, iq_ref, iw_ref, ikn_ref, *rest, n_tok, n_pages, group, k_top, n_index_bits):
    pages, (sel_ref, keys_ref) = rest[:group], rest[group:]
    j = pl.program_id(1)
    width = group * PAGE_SIZE

    def scores(s):
        s = jnp.maximum(s, 0.0) * iw_ref[0]
        return jnp.sum(s.reshape(IDX_HEADS, n_tok, s.shape[-1]), axis=0)

    kt = jnp.concatenate([p_ref[0].astype(BF16) for p_ref in pages], axis=1)
    key = _sortable(scores(jnp.dot(iq_ref[0], kt, preferred_element_type=F32)))
    keys_ref[:, pl.ds(pl.multiple_of(j * width, width), width)] = key

    @pl.when(j == n_pages // group - 1)
    def _():
        t = lax.broadcasted_iota(I32, (n_tok, PAGE_SIZE), 0)
        col = lax.broadcasted_iota(I32, (n_tok, PAGE_SIZE), 1)
        s_new = lax.dot_general(iq_ref[0], ikn_ref[0], (((1,), (1,)), ((), ())),
                                preferred_element_type=F32)
        key_new = jnp.where(col <= t, _sortable(scores(s_new)), jnp.int32(INT_MIN))
        keys_ref[:, n_pages * PAGE_SIZE:] = key_new
        t_abs = n_pages * PAGE_SIZE + lax.broadcasted_iota(I32, (n_tok, 1), 0)
        k_row = jnp.minimum(t_abs + 1, k_top)
        thr = _select_topk(keys_ref, 1, keys_ref.shape[1], k_row, n_index_bits)
        sel_ref[0] = (keys_ref[...] >= thr).astype(F32)


def _dsa_sample_kernel(pt_ref, q_ref, sel_ref, seln_ref, kn_ref, vn_ref, *rest, n_tok, n_pages, group):
    k_pages, v_pages, (o_ref, m_ref, l_ref) = rest[:group], rest[group:2 * group], rest[2 * group:]
    j = pl.program_id(1)
    rows = SA_HEADS * n_tok
    nt_dims = (((1,), (1,)), ((), ()))

    @pl.when(j == 0)
    def _():
        m_ref[...] = jnp.full_like(m_ref, NEG_BIG)
        l_ref[...] = jnp.zeros_like(l_ref)
        o_ref[...] = jnp.zeros_like(o_ref)

    def step(s, sel, pv):
        n = s.shape[-1]
        sel = jnp.broadcast_to(sel[None], (SA_HEADS, n_tok, n)).reshape(rows, n)
        s = jnp.where(sel > 0.5, s, NEG_BIG)
        m_old = m_ref[...]
        m_new = jnp.maximum(m_old, jnp.max(s, axis=1, keepdims=True))
        alpha = jnp.exp2(m_old - m_new)
        pr = jnp.exp2(s - m_new)
        l_ref[...] = alpha * l_ref[...] + jnp.sum(pr, axis=1, keepdims=True)
        o_ref[0] = alpha * o_ref[0] + pv(pr.astype(BF16))
        m_ref[...] = m_new

    kt = jnp.concatenate([p_ref[0].astype(BF16) for p_ref in k_pages], axis=1)
    vt = jnp.concatenate([p_ref[0].astype(BF16) for p_ref in v_pages], axis=1)
    step(jnp.dot(q_ref[0], kt, preferred_element_type=F32), sel_ref[0],
         lambda pr: lax.dot_general(pr, vt, nt_dims, preferred_element_type=F32))

    @pl.when(j == n_pages // group - 1)
    def _():
        step(lax.dot_general(q_ref[0], kn_ref[0], nt_dims, preferred_element_type=F32), seln_ref[0],
             lambda pr: jnp.dot(pr, vn_ref[0], preferred_element_type=F32))
        o_ref[0] = o_ref[0] / l_ref[...]


def _dsa_sample(aq, ak, av, iq, iw, ik, cache_k, cache_v, cache_ik, page_table):
    b, n_tok, _ = aq.shape
    n_pages = page_table.shape[1]
    n_keys = (n_pages + 1) * PAGE_SIZE
    k_top = min(TOP_K_MAX, (n_pages * PAGE_SIZE + n_tok) // 4)
    rows = SA_HEADS * n_tok
    per_b = lambda bi, j, pt: (bi, 0, 0)

    def page_specs(width, group):
        return [pl.BlockSpec((1, width, PAGE_SIZE), lambda bi, j, pt, i=i: (pt[bi, j * group + i], 0, 0))
                for i in range(group)]

    g_sel = _largest_divisor(n_pages, DSA_SEL_GROUP)
    iq_rows = iq.reshape(b, n_tok, IDX_HEADS, IDX_DIM).transpose(0, 2, 1, 3).reshape(b, rows, IDX_DIM)
    iw_rows = iw.transpose(0, 2, 1).reshape(b, rows, 1)
    sel = pl.pallas_call(
        functools.partial(_dsa_sel_kernel, n_tok=n_tok, n_pages=n_pages, group=g_sel, k_top=k_top,
                          n_index_bits=max(1, n_keys.bit_length())),
        grid_spec=pltpu.PrefetchScalarGridSpec(
            num_scalar_prefetch=1,
            grid=(b, n_pages // g_sel),
            in_specs=[pl.BlockSpec((1, rows, IDX_DIM), per_b),
                      pl.BlockSpec((1, rows, 1), per_b),
                      pl.BlockSpec((1, PAGE_SIZE, IDX_DIM), per_b)] + page_specs(IDX_DIM, g_sel),
            out_specs=pl.BlockSpec((1, n_tok, n_keys), per_b),
            scratch_shapes=[pltpu.VMEM((n_tok, n_keys), I32)]),
        out_shape=jax.ShapeDtypeStruct((b, n_tok, n_keys), F32),
        compiler_params=_cparams(("parallel", "arbitrary")),
        name="dsa_sel",
    )(page_table, iq_rows, iw_rows, _pad_rows(ik, PAGE_SIZE), *([cache_ik] * g_sel))

    g_att = _largest_divisor(n_pages, DSA_SAMPLE_GROUP)
    qbd = _block_diag_queries(aq, SA_HEADS, SA_KV_HEADS)
    acc = pl.pallas_call(
        functools.partial(_dsa_sample_kernel, n_tok=n_tok, n_pages=n_pages, group=g_att),
        grid_spec=pltpu.PrefetchScalarGridSpec(
            num_scalar_prefetch=1,
            grid=(b, n_pages // g_att),
            in_specs=[pl.BlockSpec((1, rows, SA_KV_WIDTH), per_b),
                      pl.BlockSpec((1, n_tok, g_att * PAGE_SIZE), lambda bi, j, pt: (bi, 0, j)),
                      pl.BlockSpec((1, n_tok, PAGE_SIZE), lambda bi, j, pt: (bi, 0, n_pages)),
                      pl.BlockSpec((1, PAGE_SIZE, SA_KV_WIDTH), per_b),
                      pl.BlockSpec((1, PAGE_SIZE, SA_KV_WIDTH), per_b)]
                     + page_specs(SA_KV_WIDTH, g_att) + page_specs(SA_KV_WIDTH, g_att),
            out_specs=pl.BlockSpec((1, rows, SA_KV_WIDTH), per_b),
            scratch_shapes=[pltpu.VMEM((rows, 1), F32), pltpu.VMEM((rows, 1), F32)]),
        out_shape=jax.ShapeDtypeStruct((b, rows, SA_KV_WIDTH), F32),
        compiler_params=_cparams(("parallel", "arbitrary")),
        name="dsa_sample",
    )(page_table, qbd, sel, sel, _pad_rows(ak, PAGE_SIZE), _pad_rows(av, PAGE_SIZE),
      *([cache_k] * g_att), *([cache_v] * g_att))
    return _take_block_diag(acc, n_tok, SA_HEADS, SA_KV_HEADS)


def _tail_kernel(x_ref, osb_ref, osa_ref, gate_ref, wsb_ref, wsa_ref, wo_ref, g2_ref, wup_ref,
                 wdn_ref, y_ref, *, ff_chunk):
    gate = gate_ref[...]
    merged = (gate[:, :D_MODEL] * jnp.dot(osb_ref[...], wsb_ref[...], preferred_element_type=F32)
              + gate[:, D_MODEL:] * jnp.dot(osa_ref[...], wsa_ref[...], preferred_element_type=F32))
    x1 = x_ref[...] + jnp.dot(merged.astype(BF16), wo_ref[...], preferred_element_type=F32)
    ms = jnp.mean(x1 * x1, axis=-1, keepdims=True)
    h = (x1 * lax.rsqrt(ms + RMS_EPS) * g2_ref[...]).astype(BF16)
    y = x1
    for c in range(D_FF // ff_chunk):
        sl = slice(c * ff_chunk, (c + 1) * ff_chunk)
        u = jnp.maximum(jnp.dot(h, wup_ref[:, sl], preferred_element_type=F32), 0.0)
        y = y + jnp.dot((u * u).astype(BF16), wdn_ref[sl, :], preferred_element_type=F32)
    y_ref[...] = y


def _tail(x2d, o_sb, o_sa, gates, w_sb_out, w_sa_out, w_o, g2, w_up, w_down, tm):
    r = x2d.shape[0]
    row = lambda w: pl.BlockSpec((tm, w), lambda i: (i, 0))
    res = lambda shape: pl.BlockSpec(shape, lambda i: (0, 0), pipeline_mode=pl.Buffered(1))
    return pl.pallas_call(
        functools.partial(_tail_kernel, ff_chunk=1024),
        grid=(r // tm,),
        in_specs=[row(D_MODEL), row(SB_WIDTH), row(SA_WIDTH), row(2 * D_MODEL),
                  res((SB_WIDTH, D_MODEL)), res((SA_WIDTH, D_MODEL)), res((D_MODEL, D_MODEL)),
                  res((1, D_MODEL)), res((D_MODEL, D_FF)), res((D_FF, D_MODEL))],
        out_specs=row(D_MODEL),
        out_shape=jax.ShapeDtypeStruct((r, D_MODEL), F32),
        compiler_params=_cparams(("parallel",)),
        name="tail",
    )(x2d, o_sb, o_sa, gates, w_sb_out, w_sa_out, w_o, g2.reshape(1, D_MODEL), w_up, w_down)


def _unpermute_heads(a):
    shp = a.shape
    a = a.reshape(shp[:-1] + (SA_HEADS, HEAD_DIM))
    return a[..., jnp.array(SA_Q_PERM), :].reshape(shp)


def kernel(x_prompt, x_sample, cache_sb_k, cache_sb_v, cache_sa_k, cache_sa_v, cache_idx_k, page_table,
           norm1_g, w_in, b_gate, q_norm_g, k_norm_g, idx_k_norm_g, w_sb_out, w_sa_out, w_o,
           norm2_g, w_up, w_down):
    depth = w_in.shape[0]
    batch, seq, _ = x_prompt.shape
    dec_batch, dec_seq, _ = x_sample.shape
    n_phys = cache_sb_k.shape[1]
    past_len = page_table.shape[1] * PAGE_SIZE
    assert seq % 512 == 0 and cache_sb_k.shape[2] == PAGE_SIZE
    pos_p = jnp.arange(seq, dtype=F32)
    pos_s = past_len + jnp.arange(dec_seq, dtype=F32)
    tm_p = 256
    tm_s = 256 if (dec_batch * dec_seq) % 256 == 0 else dec_batch * dec_seq

    xp = x_prompt.reshape(batch * seq, D_MODEL)
    xs = x_sample.reshape(dec_batch * dec_seq, D_MODEL)
    new_p = [[] for _ in range(5)]
    new_s = [[] for _ in range(5)]
    for layer in range(depth):
        w_packed = _pack_w_in(w_in[layer])
        w_sa_perm = _unpermute_heads(w_sa_out[layer].T).T.astype(BF16)
        tail_w = (w_sb_out[layer].astype(BF16), w_sa_perm, w_o[layer].astype(BF16), norm2_g[layer],
                  w_up[layer].astype(BF16), w_down[layer].astype(BF16))
        norm_args = (norm1_g[layer], q_norm_g[layer], k_norm_g[layer], idx_k_norm_g[layer], b_gate[layer])

        (*new_kv, avtb, iwt, sbq, sbk, sbv, saq, sak, sav, iq, ik2, gates) = _project(
            xp, pos_p, batch, w_packed, *norm_args, tm=tm_p, group_cols=seq)
        o_sb = _sb_prompt(sbq, sbk, sbv, batch, seq, tq=256)
        o_sa = _dsa_prompt(iq, iwt, saq, ik2, sak, avtb, batch, seq, tq=128, tk=512, nsub=2)
        xp = _tail(xp, o_sb, o_sa, gates, *tail_w, tm=256)
        for lst, arr in zip(new_p, new_kv):
            arr = jnp.moveaxis(arr.reshape(batch, -1, min(arr.shape[1], HEAD_DIM), seq), 3, 1)
            lst.append(arr if arr.shape[2] > 1 else arr[:, :, 0])

        n_s = dec_batch * dec_seq
        (*new_kv, avtb, iwt, sbq, sbk, sbv, saq, sak, sav, iq, ik2, gates) = _project(
            xs, pos_s, dec_batch, w_packed, *norm_args, tm=tm_s, group_cols=n_s)
        b3 = lambda a: a.reshape(dec_batch, dec_seq, a.shape[-1])
        iw = iwt[0].T
        o_sb = _sb_sample(b3(sbq), b3(sbk), b3(sbv),
                          _pages_t(cache_sb_k[layer]), _pages_t(cache_sb_v[layer]), page_table)
        o_sa = _dsa_sample(_unpermute_heads(b3(saq)), b3(sak), b3(sav), b3(iq), b3(iw),
                           b3(ik2)[..., :IDX_DIM],
                           _pages_t(cache_sa_k[layer]), _pages_t(cache_sa_v[layer]),
                           _pages_t(cache_idx_k[layer]), page_table)
        o_sb = o_sb.reshape(dec_batch * dec_seq, SB_WIDTH).astype(BF16)
        o_sa = _unpermute_heads(o_sa).reshape(dec_batch * dec_seq, SA_WIDTH).astype(BF16)
        xs = _tail(xs, o_sb, o_sa, gates, *tail_w, tm=tm_s)
        for lst, arr in zip(new_s, new_kv):
            arr = jnp.moveaxis(arr.reshape(-1, min(arr.shape[1], HEAD_DIM), dec_batch, dec_seq), (2, 3), (0, 1))
            lst.append(arr if arr.shape[2] > 1 else arr[:, :, 0])

    return (xp.reshape(batch, seq, D_MODEL), xs.reshape(dec_batch, dec_seq, D_MODEL),
            *(jnp.stack(l) for l in new_p), *(jnp.stack(l) for l in new_s))
```

```python
import functools

import jax
import jax.numpy as jnp
from jax import lax
from jax.experimental import pallas as pl
from jax.experimental.pallas import tpu as pltpu

F32 = jnp.float32
BF16 = jnp.bfloat16
I32 = jnp.int32

D_MODEL = 1024
HEAD_DIM = 64
SB_HEADS = 8
SA_HEADS = 8
SA_KV_HEADS = 4
IDX_HEADS = 8
IDX_DIM = 64
SB_WIDTH = SB_HEADS * HEAD_DIM
SA_WIDTH = SA_HEADS * HEAD_DIM
SA_KV_WIDTH = SA_KV_HEADS * HEAD_DIM
IDX_WIDTH = IDX_HEADS * IDX_DIM
TOP_K_MAX = 256
ROT_DIM = HEAD_DIM // 4
ROPE_THETA = 500000.0
D_FF = 4 * D_MODEL
PAGE_SIZE = 128
RMS_EPS = 1e-6
QK_SCALE = HEAD_DIM ** -0.5
SA_Q_SCALE = QK_SCALE * 1.4426950408889634
IDX_SCALE = IDX_DIM ** -0.5
IDX_W_SCALE = IDX_HEADS ** -0.5

LANES = 128
VMEM_LIMIT = 56 * 1024 * 1024

C_SQ, C_SK, C_SV = 0, 512, 1024
C_AQ, C_AK, C_AV = 1536, 2048, 2304
C_IQ, C_IK2, C_IW = 2560, 3072, 3200
C_GATE = 3328
C_END = C_GATE + 2 * D_MODEL
SA_Q_PERM = (0, 2, 1, 3, 4, 6, 5, 7)

SURV_FLOOR = -105.0
NEG_BIG = -1e30
INT_MIN = -2 ** 31

SB_SAMPLE_HEAD_PAGES = 4
DSA_SEL_GROUP = 64
DSA_SAMPLE_GROUP = 64


TOPK_GROUPS = 256


def _largest_divisor(n, cap):
    return max(d for d in range(1, min(n, cap) + 1) if n % d == 0)


def _cparams(sem, vmem=VMEM_LIMIT):
    return pltpu.CompilerParams(dimension_semantics=sem, vmem_limit_bytes=vmem)


def _rope_chunk(y, cos, sin_a, sin_b):
    up = pltpu.roll(y, LANES - ROT_DIM // 2, 1)
    dn = pltpu.roll(y, ROT_DIM // 2, 1)
    return y * cos + up * sin_a + dn * sin_b


def _head_rms_chunk(y, e_mat, gain):
    y2 = y * y
    hi = y2.astype(BF16)
    lo = (y2 - hi.astype(F32)).astype(BF16)
    ms = (jnp.dot(hi, e_mat, preferred_element_type=F32)
          + jnp.dot(lo, e_mat, preferred_element_type=F32))
    return y * lax.rsqrt(ms + RMS_EPS) * gain


def _proj_kernel(x_ref, g1_ref, w_ref, qg_ref, kg_ref, ig_ref, bg_ref, cos_ref, sa_ref, sb_ref,
                 e_ref,
                 skt_ref, svt_ref, akt_ref, avt_ref, ikt_ref, avtb_ref, iwt_ref,
                 sbq_ref, sbk_ref, sbv_ref, saq_ref, sak_ref, sav_ref, iq_ref, ik2_ref, gate_ref):
    x = x_ref[...]
    ms = jnp.mean(x * x, axis=-1, keepdims=True)
    h = (x * lax.rsqrt(ms + RMS_EPS) * g1_ref[...]).astype(BF16)

    def mm(lo, hi):
        return jnp.dot(h, w_ref[:, lo:hi], preferred_element_type=F32)

    cos, sin_a, sin_b = cos_ref[...], sa_ref[...], sb_ref[...]
    e_mat = e_ref[...]

    p = mm(C_SQ, C_SK)
    sbq_ref[...] = (p * QK_SCALE).astype(BF16)
    p = mm(C_SK, C_SV)
    skt_ref[0] = p.T
    sbk_ref[...] = p.astype(BF16)
    p = mm(C_SV, C_AQ)
    svt_ref[0] = p.T
    sbv_ref[...] = p.astype(BF16)

    p = mm(C_AQ, C_AK)
    for c in range(SA_WIDTH // LANES):
        sl = slice(c * LANES, (c + 1) * LANES)
        y = _rope_chunk(_head_rms_chunk(p[:, sl], e_mat, qg_ref[...]), cos, sin_a, sin_b)
        saq_ref[:, sl] = (y * SA_Q_SCALE).astype(BF16)
    p = mm(C_AK, C_AV)
    for c in range(SA_KV_WIDTH // LANES):
        sl = slice(c * LANES, (c + 1) * LANES)
        y = _rope_chunk(_head_rms_chunk(p[:, sl], e_mat, kg_ref[...]), cos, sin_a, sin_b)
        akt_ref[0, sl, :] = y.T
        sak_ref[:, sl] = y.astype(BF16)
    p = mm(C_AV, C_IQ)
    pt = p.T
    avt_ref[0] = pt
    avtb_ref[0] = pt.astype(BF16)
    sav_ref[...] = p.astype(BF16)

    p = mm(C_IQ, C_IK2)
    for c in range(IDX_WIDTH // LANES):
        sl = slice(c * LANES, (c + 1) * LANES)
        iq_ref[:, sl] = (_rope_chunk(p[:, sl], cos, sin_a, sin_b) * IDX_SCALE).astype(BF16)
    p = mm(C_IK2, C_GATE)
    y = _rope_chunk(_head_rms_chunk(p[:, :LANES], e_mat, ig_ref[...]), cos, sin_a, sin_b)
    ikt_ref[0] = y.T[:IDX_DIM]
    ik2_ref[...] = y.astype(BF16)
    iwt_ref[0] = p[:, LANES:].T[:IDX_HEADS] * IDX_W_SCALE

    p = mm(C_GATE, C_END) + bg_ref[...]
    gate_ref[...] = jax.nn.sigmoid(p)


def _rope_tables(pos):
    half = ROT_DIM // 2
    inv = 1.0 / (ROPE_THETA ** (jnp.arange(0, ROT_DIM, 2, dtype=F32) / ROT_DIM))
    ang = pos[:, None] * inv[None, :]
    cos, sin = jnp.cos(ang), jnp.sin(ang)
    t = pos.shape[0]
    zeros = jnp.zeros((t, HEAD_DIM - ROT_DIM), F32)
    zh = jnp.zeros((t, half), F32)
    cos_h = jnp.concatenate([cos, cos, zeros + 1.0], axis=1)
    sa_h = jnp.concatenate([-sin, zh, zeros], axis=1)
    sb_h = jnp.concatenate([zh, sin, zeros], axis=1)
    return tuple(jnp.concatenate([a, a], axis=1) for a in (cos_h, sa_h, sb_h))


def _pack_w_in(w_in):
    o = 0
    parts = []
    for n in (SB_WIDTH, SB_WIDTH, SB_WIDTH, SA_WIDTH, SA_KV_WIDTH, SA_KV_WIDTH, IDX_WIDTH,
              IDX_DIM, IDX_HEADS, 2 * D_MODEL):
        parts.append(w_in[:, o:o + n])
        o += n
    sq, sk, sv, aq, ak, av, iq, ik, iw, gates = parts
    aq = aq.reshape(D_MODEL, SA_HEADS, HEAD_DIM)[:, jnp.array(SA_Q_PERM)].reshape(D_MODEL, SA_WIDTH)
    pad = jnp.zeros((D_MODEL, C_GATE - C_IW - IDX_HEADS), w_in.dtype)
    return jnp.concatenate([sq, sk, sv, aq, ak, av, iq, ik, ik, iw, pad, gates], axis=1).astype(BF16)


def _project(x2d, pos, n_rep, w_packed, g1, qg, kg, ig, bg, tm, group_cols):
    r = x2d.shape[0]
    t = pos.shape[0]
    assert r == n_rep * t and r % tm == 0 and (t % tm == 0 or tm % t == 0)
    assert r % group_cols == 0 and group_cols % tm == 0
    ntc = group_cols // tm
    cos, sin_a, sin_b = _rope_tables(pos)
    if tm > t:
        cos, sin_a, sin_b = (jnp.tile(a, (tm // t, 1)) for a in (cos, sin_a, sin_b))
    nt = cos.shape[0] // tm
    e_mat = (jnp.arange(LANES)[:, None] // HEAD_DIM == jnp.arange(LANES)[None, :] // HEAD_DIM)
    e_mat = (e_mat.astype(F32) / HEAD_DIM).astype(BF16)

    def row(width):
        return pl.BlockSpec((tm, width), lambda i: (i, 0))

    def const(shape):
        return pl.BlockSpec(shape, lambda i: (0, 0))

    tab = pl.BlockSpec((tm, LANES), lambda i: (i % nt, 0))
    tile2 = lambda g: jnp.concatenate([g, g]).reshape(1, LANES)
    out_heights = [(SB_WIDTH, F32), (SB_WIDTH, F32), (SA_KV_WIDTH, F32), (SA_KV_WIDTH, F32),
                   (IDX_DIM, F32), (SA_KV_WIDTH, BF16), (IDX_HEADS, F32)]
    out_widths = [(SB_WIDTH, BF16), (SB_WIDTH, BF16), (SB_WIDTH, BF16),
                  (SA_WIDTH, BF16), (SA_KV_WIDTH, BF16), (SA_KV_WIDTH, BF16),
                  (IDX_WIDTH, BF16), (LANES, BF16), (2 * D_MODEL, F32)]
    return pl.pallas_call(
        _proj_kernel,
        grid=(r // tm,),
        in_specs=[row(D_MODEL), const((1, D_MODEL)),
                  pl.BlockSpec((D_MODEL, C_END), lambda i: (0, 0), pipeline_mode=pl.Buffered(1)),
                  const((1, LANES)), const((1, LANES)), const((1, LANES)), const((1, 2 * D_MODEL)),
                  tab, tab, tab, const((LANES, LANES))],
        out_specs=[pl.BlockSpec((1, h_, tm), lambda i: (i // ntc, 0, i % ntc)) for h_, _ in out_heights]
                  + [row(w_) for w_, _ in out_widths],
        out_shape=[jax.ShapeDtypeStruct((r // group_cols, h_, group_cols), dt) for h_, dt in out_heights]
                  + [jax.ShapeDtypeStruct((r, w_), dt) for w_, dt in out_widths],
        compiler_params=_cparams(("parallel",)),
        name="proj",
    )(x2d, g1.reshape(1, D_MODEL), w_packed, tile2(qg), tile2(kg), tile2(ig),
      bg.reshape(1, 2 * D_MODEL), cos, sin_a, sin_b, e_mat)


def _stick_break_tile(z, mask, surv, tri):
    t = jnp.log1p(jnp.exp(-jnp.abs(z)))
    l = -(jnp.maximum(z, 0.0) + t)
    if mask is not None:
        l = jnp.where(mask, l, 0.0)
    ls = jnp.minimum(z, 0.0) - t
    hi = l.astype(BF16)
    lo = (l - hi.astype(F32)).astype(BF16)
    ex = jnp.dot(hi, tri, preferred_element_type=F32) + jnp.dot(lo, tri, preferred_element_type=F32)
    a = jnp.exp(ls + ex + surv)
    if mask is not None:
        a = jnp.where(mask, a, 0.0)
    return a, surv + jnp.sum(l, axis=1, keepdims=True)


def _newer_tri(n):
    j = lax.broadcasted_iota(I32, (n, n), 0)
    s = lax.broadcasted_iota(I32, (n, n), 1)
    return (j > s).astype(BF16)


def _sb_prompt_kernel(q_ref, k_ref, v_ref, o_ref, acc_ref, surv_ref, *, tq):
    i = pl.program_id(2)
    t0 = i * tq
    lane = lax.broadcasted_iota(I32, (tq, LANES), 1)
    row_t = t0 + lax.broadcasted_iota(I32, (tq, tq), 0)
    col0 = lax.broadcasted_iota(I32, (tq, tq), 1)
    tri = _newer_tri(tq)
    q = q_ref[...]
    heads = range(2)
    qm = [jnp.where((lane >= HEAD_DIM) == bool(h), q, jnp.zeros_like(q)) for h in heads]
    acc_ref[...] = jnp.zeros_like(acc_ref)
    surv_ref[...] = jnp.zeros_like(surv_ref)

    def body(carry):
        kt, _ = carry
        start = pl.multiple_of(kt * tq, tq)
        kb = k_ref[pl.ds(start, tq), :]
        vb = v_ref[pl.ds(start, tq), :]
        mask = (start + col0) < row_t
        z = [lax.dot_general(qm[h], kb, (((1,), (1,)), ((), ())), preferred_element_type=F32)
             for h in heads]
        l, ls, ex = [], [], []
        for h in heads:
            t = jnp.log1p(jnp.exp(-jnp.abs(z[h])))
            l.append(jnp.where(mask, -(jnp.maximum(z[h], 0.0) + t), 0.0))
            ls.append(jnp.minimum(z[h], 0.0) - t)
            hi = l[h].astype(BF16)
            lo = (l[h] - hi.astype(F32)).astype(BF16)
            ex.append(jnp.dot(hi, tri, preferred_element_type=F32)
                      + jnp.dot(lo, tri, preferred_element_type=F32))
        acc, surv = [], []
        for h in heads:
            a = jnp.where(mask, jnp.exp(ls[h] + ex[h] + surv_ref[h]), 0.0)
            acc.append(acc_ref[h] + jnp.dot(a.astype(BF16), vb, preferred_element_type=F32))
            surv.append(surv_ref[h] + jnp.sum(l[h], axis=1, keepdims=True))
        acc_ref[...] = jnp.stack(acc)
        surv_ref[...] = jnp.stack(surv)
        return kt - 1, jnp.maximum(jnp.max(surv[0]), jnp.max(surv[1]))

    lax.while_loop(lambda c: (c[0] >= 0) & (c[1] > SURV_FLOOR), body, (i, jnp.float32(0.0)))
    o_ref[...] = jnp.where(lane < HEAD_DIM, acc_ref[0], acc_ref[1]).astype(o_ref.dtype)


def _sb_prompt(q, k, v, batch, seq, tq):
    nq = seq // tq
    n_chunks = SB_WIDTH // LANES
    qspec = pl.BlockSpec((tq, LANES), lambda b, c, i: (b * nq + i, c))
    kvspec = pl.BlockSpec((seq, LANES), lambda b, c, i: (b, c))
    return pl.pallas_call(
        functools.partial(_sb_prompt_kernel, tq=tq),
        grid=(batch, n_chunks, nq),
        in_specs=[qspec, kvspec, kvspec],
        out_specs=qspec,
        out_shape=jax.ShapeDtypeStruct((batch * seq, SB_WIDTH), BF16),
        scratch_shapes=[pltpu.VMEM((2, tq, LANES), F32), pltpu.VMEM((2, tq, 1), F32)],
        compiler_params=_cparams(("parallel", "parallel", "arbitrary")),
        name="sb_prompt",
    )(q, k, v)


def _sortable(x):
    bits = lax.bitcast_convert_type(x, I32)
    key = jnp.where(bits < 0, bits ^ jnp.int32(0x7FFFFFFF), bits)
    return jnp.where(x == 0.0, jnp.int32(0), key)


def _select_topk(keys_ref, n_tiles, tk, k_row, n_index_bits, key_axis=1):
    qn = keys_ref.shape[1 - key_axis]
    sign = jnp.int32(INT_MIN)
    tile_shape = (qn, tk) if key_axis == 1 else (tk, qn)
    fold = LANES if key_axis == 1 else 8
    vec = (qn, 1) if key_axis == 1 else (1, qn)

    def tile_at(kt):
        start = pl.multiple_of(kt * tk, tk)
        idx = (slice(None), pl.ds(start, tk)) if key_axis == 1 else (pl.ds(start, tk), slice(None))
        return start, idx

    def count(pred):
        def tile_body(kt, acc):
            start, idx = tile_at(kt)
            m = pred(keys_ref[idx], start).astype(I32)
            if key_axis == 1:
                for c in range(tk // fold):
                    acc = acc + m[:, c * fold:(c + 1) * fold]
                return acc
            return acc + jnp.sum(m.reshape(tk // fold, fold, qn), axis=0)
        acc0 = jnp.zeros((qn, fold) if key_axis == 1 else (fold, qn), I32)
        acc = lax.fori_loop(0, n_tiles, tile_body, acc0)
        if key_axis == 0:
            for shift in (4, 2, 1):
                acc = acc + pltpu.roll(acc, shift, 0)
            return acc[:1]
        return jnp.sum(acc.astype(F32), axis=1, keepdims=True).astype(I32)

    bracket = None
    if key_axis == 0 and tk % TOPK_GROUPS == 0:
        def gmax_body(kt, acc):
            _, idx = tile_at(kt)
            return jnp.maximum(acc, jnp.max(keys_ref[idx].reshape(tk // TOPK_GROUPS, TOPK_GROUPS, qn), axis=0))
        gmax = lax.fori_loop(0, n_tiles, gmax_body, jnp.full((TOPK_GROUPS, qn), INT_MIN, I32))
        gmax = gmax.reshape(TOPK_GROUPS // fold, fold, qn)
        hi_b, lo_b = jnp.max(gmax, axis=0), jnp.min(gmax, axis=0)
        for shift in (4, 2, 1):
            hi_b = jnp.maximum(hi_b, pltpu.roll(hi_b, shift, 0))
            lo_b = jnp.minimum(lo_b, pltpu.roll(lo_b, shift, 0))
        bracket = (lo_b[:1], hi_b[:1])
    elif key_axis == 1 and 2 * fold == TOPK_GROUPS and tk >= 2 * fold:
        def gmax_body(kt, accs):
            _, idx = tile_at(kt)
            tile, accs = keys_ref[idx], list(accs)
            for c in range(tk // fold):
                accs[c % 2] = jnp.maximum(accs[c % 2], tile[:, c * fold:(c + 1) * fold])
            return tuple(accs)
        init = jnp.full((qn, fold), INT_MIN, I32)
        even, odd = lax.fori_loop(0, n_tiles, gmax_body, (init, init))
        hi_b, lo_b = jnp.maximum(even, odd), jnp.minimum(even, odd)
        for shift in (64, 32, 16, 8, 4, 2, 1):
            hi_b = jnp.maximum(hi_b, pltpu.roll(hi_b, shift, 1))
            lo_b = jnp.minimum(lo_b, pltpu.roll(lo_b, shift, 1))
        bracket = (lo_b[:, :1], hi_b[:, :1])

    zeros = jnp.zeros(vec, I32)
    b_first, base, span = jnp.int32(31), zeros, zeros - 1
    if bracket is not None:
        base = bracket[0] ^ sign
        span = (bracket[1] ^ sign) - base
        expo = lax.shift_right_logical(lax.bitcast_convert_type(jnp.maximum(span, 1).astype(F32), I32),
                                       jnp.int32(23)) - 127
        top = jnp.where(span < 0, 31, expo)
        b_first = jnp.minimum(jnp.max(top.astype(F32)).astype(I32), 31)

    def one_bit(b, prefix, thr, done):
        bit = jnp.where(b >= 0, lax.shift_left(jnp.int32(1), jnp.maximum(b, 0)), 0)
        cand_off = prefix | bit
        in_span = (cand_off ^ sign) <= (span ^ sign)
        cand = (base + cand_off) ^ sign
        cnt = count(lambda kk, start: kk >= cand)
        live = (done == 0) & in_span
        prefix = jnp.where(live & (cnt >= k_row), cand_off, prefix)
        hit = live & (cnt == k_row)
        return prefix, jnp.where(hit, cand, thr), jnp.where(hit, 1, done)

    def bit_body(carry):
        b, prefix, thr, done, _ = carry
        for u in range(2):
            prefix, thr, done = one_bit(b - u, prefix, thr, done)
        return b - 2, prefix, thr, done, jnp.min(done.astype(F32)).astype(I32)

    _, prefix, thr, done, all_done = lax.while_loop(
        lambda c: (c[0] >= 0) & (c[4] == 0), bit_body,
        (b_first, zeros, zeros, zeros, jnp.int32(0)))
    thr = jnp.where(done == 0, (base + prefix) ^ sign, thr)

    if key_axis == 0:
        @pl.when(all_done == 0)
        def _():
            tied = done == 0
            want = (k_row - count(lambda kk, start: kk > thr)).astype(F32)
            half = tk // 2
            r = lax.broadcasted_iota(I32, (half, half), 0)
            c = lax.broadcasted_iota(I32, (half, half), 1)
            tri = (c <= r).astype(BF16)

            def fix_body(kt, seen):
                start, _ = tile_at(kt)
                for u in range(2):
                    idx = (pl.ds(pl.multiple_of(start + u * half, half), half), slice(None))
                    kk = keys_ref[idx]
                    tie = tied & (kk == thr)
                    rank = seen + jnp.dot(tri, tie.astype(BF16), preferred_element_type=F32)
                    keys_ref[idx] = jnp.where(tie & (rank > want), thr - 1, kk)
                    seen = rank[half - 1:half]
                return seen
            lax.fori_loop(0, n_tiles, fix_body, jnp.zeros(vec, F32))
        return thr

    @pl.when(all_done == 0)
    def _():
        tied = done == 0
        want = k_row - count(lambda kk, start: kk > thr)
        col0 = lax.broadcasted_iota(I32, tile_shape, key_axis)

        def ties_before(x):
            return count(lambda kk, start: (kk == thr) & ((start + col0) < x))

        def idx_body(j, x):
            cand = x | lax.shift_left(jnp.int32(1), n_index_bits - 1 - j)
            return jnp.where(ties_before(cand) < want, cand, x)

        x = lax.fori_loop(0, n_index_bits, idx_body, zeros)

        def fix_body(kt, c):
            start, idx = tile_at(kt)
            kk = keys_ref[idx]
            drop = tied & (kk == thr) & ((start + col0) > x)
            keys_ref[idx] = jnp.where(drop, thr - 1, kk)
            return c
        lax.fori_loop(0, n_tiles, fix_body, 0)

    return thr


def _dsa_prompt_kernel(iq_ref, iwt_ref, aq_ref, ik_ref, ak_ref, avt_ref, o_ref,
                       keys_ref, iqt_ref, aqt_ref, m_ref, l_ref, acc_ref,
                       *, tq, tk, nsub, k_top, n_index_bits):
    assert tq == LANES
    i = pl.program_id(1)
    t0 = i * tq
    n_super = (t0 + tq + nsub * tk - 1) // (nsub * tk)
    n_tiles = nsub * n_super
    dim_row = lax.broadcasted_iota(I32, (LANES, tq), 0)
    halfmask = [(dim_row >= HEAD_DIM) == bool(e) for e in range(2)]
    t_query = t0 + lax.broadcasted_iota(I32, (1, tq), 1)

    def chunk_t(ref, c):
        return ref[:, c * LANES:(c + 1) * LANES].astype(F32).T

    for c in range(IDX_WIDTH // LANES):
        qt = chunk_t(iq_ref, c)
        for e in range(2):
            iqt_ref[c, :, e * tq:(e + 1) * tq] = jnp.where(halfmask[e], qt, 0.0).astype(BF16)

    def score_body(ks, c):
        starts = [pl.multiple_of((ks * nsub + u) * tk, tk) for u in range(nsub)]
        dots = [[jnp.dot(ik_ref[pl.ds(st, tk), :], iqt_ref[c2], preferred_element_type=F32)
                 for c2 in range(IDX_WIDTH // LANES)] for st in starts]
        for u, st in enumerate(starts):
            score = jnp.zeros((tk, tq), F32)
            for c2 in range(IDX_WIDTH // LANES):
                for e in range(2):
                    h = 2 * c2 + e
                    s = dots[u][c2][:, e * tq:(e + 1) * tq]
                    score = score + jnp.maximum(s, 0.0) * iwt_ref[0, h:h + 1, :]
            pos = st + lax.broadcasted_iota(I32, (tk, tq), 0)
            keys_ref[pl.ds(st, tk), :] = jnp.where(pos <= t_query, _sortable(score), jnp.int32(INT_MIN))
        return c
    lax.fori_loop(0, n_super, score_body, 0)

    k_row = jnp.minimum(t_query + 1, k_top)
    thr = _select_topk(keys_ref, (t0 + tq + tk - 1) // tk, tk, k_row, n_index_bits, key_axis=0)

    for c in range(SA_WIDTH // LANES):
        qt = chunk_t(aq_ref, c)
        p, j = c // 2, c % 2
        for e in range(2):
            aqt_ref[2 * p + e, :, j * tq:(j + 1) * tq] = jnp.where(halfmask[e], qt, 0.0).astype(BF16)
    m_ref[...] = jnp.full_like(m_ref, NEG_BIG)
    l_ref[...] = jnp.zeros_like(l_ref)
    acc_ref[...] = jnp.zeros_like(acc_ref)

    ones_rows = jnp.ones((16, tk), BF16)

    def att_body(ks, c):
        starts = [pl.multiple_of((ks * nsub + u) * tk, tk) for u in range(nsub)]
        logits = [[jnp.dot(ak_ref[pl.ds(st, tk), (g // 2) * LANES:(g // 2 + 1) * LANES], aqt_ref[g],
                           preferred_element_type=F32) for g in range(SA_KV_HEADS)]
                  for st in starts]
        m_all, l_all = m_ref[...], l_ref[...]
        m_cur = [m_all[hh:hh + 1] for hh in range(SA_HEADS)]
        l_cur = [l_all[hh:hh + 1] for hh in range(SA_HEADS)]
        acc_cur = [acc_ref[hh * HEAD_DIM:(hh + 1) * HEAD_DIM] for hh in range(SA_HEADS)]
        for u, st in enumerate(starts):
            sel = keys_ref[pl.ds(st, tk), :] >= thr
            probs, alphas = [], []
            for hh in range(SA_HEADS):
                g, j = hh // 2, hh % 2
                s = jnp.where(sel, logits[u][g][:, j * tq:(j + 1) * tq], NEG_BIG)
                m_new = jnp.maximum(m_cur[hh], jnp.max(s, axis=0, keepdims=True))
                alpha = jnp.exp2(m_cur[hh] - m_new)
                m_cur[hh] = m_new
                probs.append(jnp.exp2(s - m_new).astype(BF16))
                alphas.append(alpha)
            for hh in range(SA_HEADS):
                g = hh // 2
                vt = avt_ref[0, g * HEAD_DIM:(g + 1) * HEAD_DIM, pl.ds(st, tk)]
                pv = jnp.dot(jnp.concatenate([vt, ones_rows], axis=0), probs[hh],
                             preferred_element_type=F32)
                acc_cur[hh] = alphas[hh] * acc_cur[hh] + pv[:HEAD_DIM]
                l_cur[hh] = alphas[hh] * l_cur[hh] + pv[HEAD_DIM:HEAD_DIM + 1]
        m_ref[...] = jnp.concatenate(m_cur, axis=0)
        l_ref[...] = jnp.concatenate(l_cur, axis=0)
        acc_ref[...] = jnp.concatenate(acc_cur, axis=0)
        return c
    lax.fori_loop(0, n_super, att_body, 0)

    def head_out(hh):
        return acc_ref[hh * HEAD_DIM:(hh + 1) * HEAD_DIM] / l_ref[hh:hh + 1]

    for p in range(SA_KV_WIDTH // LANES):
        for j in range(2):
            ot = jnp.concatenate([head_out(2 * (2 * p) + j), head_out(2 * (2 * p + 1) + j)], axis=0)
            o_ref[:, (2 * p + j) * LANES:(2 * p + j + 1) * LANES] = ot.T.astype(o_ref.dtype)


def _dsa_prompt(iq, iwt, aq, ik2, ak, avt, batch, seq, tq, tk, nsub):
    assert seq % (nsub * tk) == 0
    nq = seq // tq
    k_top = min(TOP_K_MAX, seq // 4)
    n_index_bits = max(1, seq.bit_length())
    qspec = lambda w: pl.BlockSpec((tq, w), lambda b, i: (b * nq + i, 0))
    kspec = lambda w: pl.BlockSpec((seq, w), lambda b, i: (b, 0), pipeline_mode=pl.Buffered(1))
    return pl.pallas_call(
        functools.partial(_dsa_prompt_kernel, tq=tq, tk=tk, nsub=nsub, k_top=k_top,
                          n_index_bits=n_index_bits),
        grid=(batch, nq),
        in_specs=[qspec(IDX_WIDTH),
                  pl.BlockSpec((1, IDX_HEADS, tq), lambda b, i: (b, 0, i)),
                  qspec(SA_WIDTH), kspec(LANES), kspec(SA_KV_WIDTH),
                  pl.BlockSpec((1, SA_KV_WIDTH, seq), lambda b, i: (b, 0, 0),
                               pipeline_mode=pl.Buffered(1))],
        out_specs=qspec(SA_WIDTH),
        out_shape=jax.ShapeDtypeStruct((batch * seq, SA_WIDTH), BF16),
        scratch_shapes=[pltpu.VMEM((seq, tq), I32),
                        pltpu.VMEM((IDX_HEADS // 2, LANES, 2 * tq), BF16),
                        pltpu.VMEM((SA_KV_HEADS, LANES, 2 * tq), BF16),
                        pltpu.VMEM((SA_HEADS, tq), F32), pltpu.VMEM((SA_HEADS, tq), F32),
                        pltpu.VMEM((SA_HEADS * HEAD_DIM, tq), F32)],
        compiler_params=_cparams(("parallel", "arbitrary")),
        name="dsa_prompt",
    )(iq, iwt, aq, ik2, ak, avt)


def _sb_page_step(q, kc_ref, vc_ref, o_ref, surv_ref, tri):
    @pl.when(jnp.max(surv_ref[0]) > SURV_FLOOR)
    def _():
        kt = kc_ref[0].astype(BF16)
        vt = vc_ref[0].astype(BF16)
        z = jnp.dot(q, kt, preferred_element_type=F32)
        a, surv = _stick_break_tile(z, None, surv_ref[0][:, :1], tri)
        o_ref[0] += lax.dot_general(a.astype(BF16), vt, (((1,), (1,)), ((), ())),
                                    preferred_element_type=F32)
        surv_ref[0] = jnp.broadcast_to(surv, surv_ref.shape[1:])


def _sb_sample_head_kernel(pt_ref, q_ref, kn_ref, vn_ref, *rest, n_tok, n_head_pages):
    pages, (o_ref, surv_ref) = rest[:2 * n_head_pages], rest[2 * n_head_pages:]
    rows = q_ref.shape[1]
    q = q_ref[0]
    tri = _newer_tri(PAGE_SIZE)
    z = lax.dot_general(q, kn_ref[0], (((1,), (1,)), ((), ())), preferred_element_type=F32)
    t = lax.broadcasted_iota(I32, (rows, PAGE_SIZE), 0) % n_tok
    mask = lax.broadcasted_iota(I32, (rows, PAGE_SIZE), 1) < t
    a, surv = _stick_break_tile(z, mask, jnp.zeros((rows, 1), F32), tri)
    o_ref[0] = jnp.dot(a.astype(BF16), vn_ref[0], preferred_element_type=F32)
    surv_ref[0] = jnp.broadcast_to(surv, surv_ref.shape[1:])
    for i in range(n_head_pages):
        _sb_page_step(q, pages[2 * i], pages[2 * i + 1], o_ref, surv_ref, tri)


def _sb_sample_tail_kernel(pt_ref, q_ref, acc_ref, sin_ref, kc_ref, vc_ref, o_ref, surv_ref):
    @pl.when(pl.program_id(1) == 0)
    def _():
        o_ref[...] = acc_ref[...]
        surv_ref[...] = sin_ref[...]
    _sb_page_step(q_ref[0], kc_ref, vc_ref, o_ref, surv_ref, _newer_tri(PAGE_SIZE))


def _block_diag_queries(q, n_heads_q, n_heads_kv):
    group = n_heads_q // n_heads_kv
    blocks = []
    for h in range(n_heads_q):
        kv = h // group
        qh = q[:, :, h * HEAD_DIM:(h + 1) * HEAD_DIM]
        blocks.append(jnp.pad(qh, ((0, 0), (0, 0), (kv * HEAD_DIM, (n_heads_kv - 1 - kv) * HEAD_DIM))))
    return jnp.concatenate(blocks, axis=1)


def _take_block_diag(acc, n_tok, n_heads_q, n_heads_kv):
    group = n_heads_q // n_heads_kv
    return jnp.concatenate(
        [acc[:, h * n_tok:(h + 1) * n_tok, (h // group) * HEAD_DIM:(h // group + 1) * HEAD_DIM]
         for h in range(n_heads_q)], axis=2)


def _pages_t(cache):
    n_phys, page = cache.shape[:2]
    return jnp.moveaxis(cache, 1, -1).reshape(n_phys, -1, page)


def _pad_rows(a, rows):
    return jnp.pad(a, ((0, 0), (0, rows - a.shape[1]), (0, 0)))


def _sb_sample(q, k, v, cache_k, cache_v, page_table):
    b, n_tok, _ = q.shape
    n_pages = page_table.shape[1]
    rows = SB_HEADS * n_tok
    qbd = _block_diag_queries(q, SB_HEADS, SB_HEADS)
    kn, vn = _pad_rows(k, PAGE_SIZE), _pad_rows(v, PAGE_SIZE)
    n_head = min(SB_SAMPLE_HEAD_PAGES, n_pages)
    out_shape = [jax.ShapeDtypeStruct((b, rows, SB_WIDTH), F32),
                 jax.ShapeDtypeStruct((b, rows, LANES), F32)]

    per_b1 = lambda bi, pt: (bi, 0, 0)
    page_specs = []
    for i in range(n_head):
        spec = pl.BlockSpec((1, SB_WIDTH, PAGE_SIZE), lambda bi, pt, i=i: (pt[bi, n_pages - 1 - i], 0, 0))
        page_specs += [spec, spec]
    acc, surv = pl.pallas_call(
        functools.partial(_sb_sample_head_kernel, n_tok=n_tok, n_head_pages=n_head),
        grid_spec=pltpu.PrefetchScalarGridSpec(
            num_scalar_prefetch=1,
            grid=(b,),
            in_specs=[pl.BlockSpec((1, rows, SB_WIDTH), per_b1),
                      pl.BlockSpec((1, PAGE_SIZE, SB_WIDTH), per_b1),
                      pl.BlockSpec((1, PAGE_SIZE, SB_WIDTH), per_b1)] + page_specs,
            out_specs=[pl.BlockSpec((1, rows, SB_WIDTH), per_b1), pl.BlockSpec((1, rows, LANES), per_b1)]),
        out_shape=out_shape,
        compiler_params=_cparams(("parallel",)),
        name="sb_sample_head",
    )(page_table, qbd, kn, vn, *([cache_k, cache_v] * n_head))

    n_tail = n_pages - n_head
    if n_tail > 0:
        per_b = lambda bi, j, pt: (bi, 0, 0)
        page = lambda bi, j, pt: (pt[bi, n_tail - 1 - j], 0, 0)

        def older_pages(acc, surv):
            return pl.pallas_call(
                _sb_sample_tail_kernel,
                grid_spec=pltpu.PrefetchScalarGridSpec(
                    num_scalar_prefetch=1,
                    grid=(b, n_tail),
                    in_specs=[pl.BlockSpec((1, rows, SB_WIDTH), per_b),
                              pl.BlockSpec((1, rows, SB_WIDTH), per_b),
                              pl.BlockSpec((1, rows, LANES), per_b),
                              pl.BlockSpec((1, SB_WIDTH, PAGE_SIZE), page),
                              pl.BlockSpec((1, SB_WIDTH, PAGE_SIZE), page)],
                    out_specs=[pl.BlockSpec((1, rows, SB_WIDTH), per_b),
                               pl.BlockSpec((1, rows, LANES), per_b)]),
                out_shape=out_shape,
                compiler_params=_cparams(("parallel", "arbitrary")),
                name="sb_sample_tail",
            )(page_table, qbd, acc, surv, cache_k, cache_v)[0]

        acc = lax.cond(jnp.max(surv) > SURV_FLOOR, older_pages, lambda acc, surv: acc, acc, surv)
    return _take_block_diag(acc, n_tok, SB_HEADS, SB_HEADS)


def _dsa_sel_kernel(pt_ref, iq_ref, iw_ref, ikn_ref, *rest, n_tok, n_pages, group, k_top, n_index_bits):
    pages, (sel_ref, keys_ref) = rest[:group], rest[group:]
    j = pl.program_id(1)
    width = group * PAGE_SIZE

    def scores(s):
        s = jnp.maximum(s, 0.0) * iw_ref[0]
        return jnp.sum(s.reshape(IDX_HEADS, n_tok, s.shape[-1]), axis=0)

    kt = jnp.concatenate([p_ref[0].astype(BF16) for p_ref in pages], axis=1)
    key = _sortable(scores(jnp.dot(iq_ref[0], kt, preferred_element_type=F32)))
    keys_ref[:, pl.ds(pl.multiple_of(j * width, width), width)] = key

    @pl.when(j == n_pages // group - 1)
    def _():
        t = lax.broadcasted_iota(I32, (n_tok, PAGE_SIZE), 0)
        col = lax.broadcasted_iota(I32, (n_tok, PAGE_SIZE), 1)
        s_new = lax.dot_general(iq_ref[0], ikn_ref[0], (((1,), (1,)), ((), ())),
                                preferred_element_type=F32)
        key_new = jnp.where(col <= t, _sortable(scores(s_new)), jnp.int32(INT_MIN))
        keys_ref[:, n_pages * PAGE_SIZE:] = key_new
        t_abs = n_pages * PAGE_SIZE + lax.broadcasted_iota(I32, (n_tok, 1), 0)
        k_row = jnp.minimum(t_abs + 1, k_top)
        thr = _select_topk(keys_ref, 1, keys_ref.shape[1], k_row, n_index_bits)
        sel_ref[0] = (keys_ref[...] >= thr).astype(F32)


def _dsa_sample_kernel(pt_ref, q_ref, sel_ref, seln_ref, kn_ref, vn_ref, *rest, n_tok, n_pages, group):
    k_pages, v_pages, (o_ref, m_ref, l_ref) = rest[:group], rest[group:2 * group], rest[2 * group:]
    j = pl.program_id(1)
    rows = SA_HEADS * n_tok
    nt_dims = (((1,), (1,)), ((), ()))

    @pl.when(j == 0)
    def _():
        m_ref[...] = jnp.full_like(m_ref, NEG_BIG)
        l_ref[...] = jnp.zeros_like(l_ref)
        o_ref[...] = jnp.zeros_like(o_ref)

    def step(s, sel, pv):
        n = s.shape[-1]
        sel = jnp.broadcast_to(sel[None], (SA_HEADS, n_tok, n)).reshape(rows, n)
        s = jnp.where(sel > 0.5, s, NEG_BIG)
        m_old = m_ref[...]
        m_new = jnp.maximum(m_old, jnp.max(s, axis=1, keepdims=True))
        alpha = jnp.exp2(m_old - m_new)
        pr = jnp.exp2(s - m_new)
        l_ref[...] = alpha * l_ref[...] + jnp.sum(pr, axis=1, keepdims=True)
        o_ref[0] = alpha * o_ref[0] + pv(pr.astype(BF16))
        m_ref[...] = m_new

    kt = jnp.concatenate([p_ref[0].astype(BF16) for p_ref in k_pages], axis=1)
    vt = jnp.concatenate([p_ref[0].astype(BF16) for p_ref in v_pages], axis=1)
    step(jnp.dot(q_ref[0], kt, preferred_element_type=F32), sel_ref[0],
         lambda pr: lax.dot_general(pr, vt, nt_dims, preferred_element_type=F32))

    @pl.when(j == n_pages // group - 1)
    def _():
        step(lax.dot_general(q_ref[0], kn_ref[0], nt_dims, preferred_element_type=F32), seln_ref[0],
             lambda pr: jnp.dot(pr, vn_ref[0], preferred_element_type=F32))
        o_ref[0] = o_ref[0] / l_ref[...]


def _dsa_sample(aq, ak, av, iq, iw, ik, cache_k, cache_v, cache_ik, page_table):
    b, n_tok, _ = aq.shape
    n_pages = page_table.shape[1]
    n_keys = (n_pages + 1) * PAGE_SIZE
    k_top = min(TOP_K_MAX, (n_pages * PAGE_SIZE + n_tok) // 4)
    rows = SA_HEADS * n_tok
    per_b = lambda bi, j, pt: (bi, 0, 0)

    def page_specs(width, group):
        return [pl.BlockSpec((1, width, PAGE_SIZE), lambda bi, j, pt, i=i: (pt[bi, j * group + i], 0, 0))
                for i in range(group)]

    g_sel = _largest_divisor(n_pages, DSA_SEL_GROUP)
    iq_rows = iq.reshape(b, n_tok, IDX_HEADS, IDX_DIM).transpose(0, 2, 1, 3).reshape(b, rows, IDX_DIM)
    iw_rows = iw.transpose(0, 2, 1).reshape(b, rows, 1)
    sel = pl.pallas_call(
        functools.partial(_dsa_sel_kernel, n_tok=n_tok, n_pages=n_pages, group=g_sel, k_top=k_top,
                          n_index_bits=max(1, n_keys.bit_length())),
        grid_spec=pltpu.PrefetchScalarGridSpec(
            num_scalar_prefetch=1,
            grid=(b, n_pages // g_sel),
            in_specs=[pl.BlockSpec((1, rows, IDX_DIM), per_b),
                      pl.BlockSpec((1, rows, 1), per_b),
                      pl.BlockSpec((1, PAGE_SIZE, IDX_DIM), per_b)] + page_specs(IDX_DIM, g_sel),
            out_specs=pl.BlockSpec((1, n_tok, n_keys), per_b),
            scratch_shapes=[pltpu.VMEM((n_tok, n_keys), I32)]),
        out_shape=jax.ShapeDtypeStruct((b, n_tok, n_keys), F32),
        compiler_params=_cparams(("parallel", "arbitrary")),
        name="dsa_sel",
    )(page_table, iq_rows, iw_rows, _pad_rows(ik, PAGE_SIZE), *([cache_ik] * g_sel))

    g_att = _largest_divisor(n_pages, DSA_SAMPLE_GROUP)
    qbd = _block_diag_queries(aq, SA_HEADS, SA_KV_HEADS)
    acc = pl.pallas_call(
        functools.partial(_dsa_sample_kernel, n_tok=n_tok, n_pages=n_pages, group=g_att),
        grid_spec=pltpu.PrefetchScalarGridSpec(
            num_scalar_prefetch=1,
            grid=(b, n_pages // g_att),
            in_specs=[pl.BlockSpec((1, rows, SA_KV_WIDTH), per_b),
                      pl.BlockSpec((1, n_tok, g_att * PAGE_SIZE), lambda bi, j, pt: (bi, 0, j)),
                      pl.BlockSpec((1, n_tok, PAGE_SIZE), lambda bi, j, pt: (bi, 0, n_pages)),
                      pl.BlockSpec((1, PAGE_SIZE, SA_KV_WIDTH), per_b),
                      pl.BlockSpec((1, PAGE_SIZE, SA_KV_WIDTH), per_b)]
                     + page_specs(SA_KV_WIDTH, g_att) + page_specs(SA_KV_WIDTH, g_att),
            out_specs=pl.BlockSpec((1, rows, SA_KV_WIDTH), per_b),
            scratch_shapes=[pltpu.VMEM((rows, 1), F32), pltpu.VMEM((rows, 1), F32)]),
        out_shape=jax.ShapeDtypeStruct((b, rows, SA_KV_WIDTH), F32),
        compiler_params=_cparams(("parallel", "arbitrary")),
        name="dsa_sample",
    )(page_table, qbd, sel, sel, _pad_rows(ak, PAGE_SIZE), _pad_rows(av, PAGE_SIZE),
      *([cache_k] * g_att), *([cache_v] * g_att))
    return _take_block_diag(acc, n_tok, SA_HEADS, SA_KV_HEADS)


def _tail_kernel(x_ref, osb_ref, osa_ref, gate_ref, wsb_ref, wsa_ref, wo_ref, g2_ref, wup_ref,
                 wdn_ref, y_ref, *, ff_chunk):
    gate = gate_ref[...]
    merged = (gate[:, :D_MODEL] * jnp.dot(osb_ref[...], wsb_ref[...], preferred_element_type=F32)
              + gate[:, D_MODEL:] * jnp.dot(osa_ref[...], wsa_ref[...], preferred_element_type=F32))
    x1 = x_ref[...] + jnp.dot(merged.astype(BF16), wo_ref[...], preferred_element_type=F32)
    ms = jnp.mean(x1 * x1, axis=-1, keepdims=True)
    h = (x1 * lax.rsqrt(ms + RMS_EPS) * g2_ref[...]).astype(BF16)
    y = x1
    n_chunks = D_FF // ff_chunk
    up = lambda c: jnp.dot(h, wup_ref[:, c * ff_chunk:(c + 1) * ff_chunk], preferred_element_type=F32)
    u_next = up(0)
    for c in range(n_chunks):
        u, u_next = u_next, (up(c + 1) if c + 1 < n_chunks else None)
        u = jnp.maximum(u, 0.0)
        y = y + jnp.dot((u * u).astype(BF16), wdn_ref[c * ff_chunk:(c + 1) * ff_chunk, :],
                        preferred_element_type=F32)
    y_ref[...] = y


def _tail(x2d, o_sb, o_sa, gates, w_sb_out, w_sa_out, w_o, g2, w_up, w_down, tm):
    r = x2d.shape[0]
    row = lambda w: pl.BlockSpec((tm, w), lambda i: (i, 0))
    res = lambda shape: pl.BlockSpec(shape, lambda i: (0, 0), pipeline_mode=pl.Buffered(1))
    return pl.pallas_call(
        functools.partial(_tail_kernel, ff_chunk=1024),
        grid=(r // tm,),
        in_specs=[row(D_MODEL), row(SB_WIDTH), row(SA_WIDTH), row(2 * D_MODEL),
                  res((SB_WIDTH, D_MODEL)), res((SA_WIDTH, D_MODEL)), res((D_MODEL, D_MODEL)),
                  res((1, D_MODEL)), res((D_MODEL, D_FF)), res((D_FF, D_MODEL))],
        out_specs=row(D_MODEL),
        out_shape=jax.ShapeDtypeStruct((r, D_MODEL), F32),
        compiler_params=_cparams(("parallel",)),
        name="tail",
    )(x2d, o_sb, o_sa, gates, w_sb_out, w_sa_out, w_o, g2.reshape(1, D_MODEL), w_up, w_down)


def _unpermute_heads(a):
    shp = a.shape
    a = a.reshape(shp[:-1] + (SA_HEADS, HEAD_DIM))
    return a[..., jnp.array(SA_Q_PERM), :].reshape(shp)


def kernel(x_prompt, x_sample, cache_sb_k, cache_sb_v, cache_sa_k, cache_sa_v, cache_idx_k, page_table,
           norm1_g, w_in, b_gate, q_norm_g, k_norm_g, idx_k_norm_g, w_sb_out, w_sa_out, w_o,
           norm2_g, w_up, w_down):
    depth = w_in.shape[0]
    batch, seq, _ = x_prompt.shape
    dec_batch, dec_seq, _ = x_sample.shape
    n_phys = cache_sb_k.shape[1]
    past_len = page_table.shape[1] * PAGE_SIZE
    assert seq % 512 == 0 and cache_sb_k.shape[2] == PAGE_SIZE
    pos_p = jnp.arange(seq, dtype=F32)
    pos_s = past_len + jnp.arange(dec_seq, dtype=F32)
    tm_p = 512
    tm_s = 256 if (dec_batch * dec_seq) % 256 == 0 else dec_batch * dec_seq

    xp = x_prompt.reshape(batch * seq, D_MODEL)
    xs = x_sample.reshape(dec_batch * dec_seq, D_MODEL)
    new_p = [[] for _ in range(5)]
    new_s = [[] for _ in range(5)]
    for layer in range(depth):
        w_packed = _pack_w_in(w_in[layer])
        w_sa_perm = _unpermute_heads(w_sa_out[layer].T).T.astype(BF16)
        tail_w = (w_sb_out[layer].astype(BF16), w_sa_perm, w_o[layer].astype(BF16), norm2_g[layer],
                  w_up[layer].astype(BF16), w_down[layer].astype(BF16))
        norm_args = (norm1_g[layer], q_norm_g[layer], k_norm_g[layer], idx_k_norm_g[layer], b_gate[layer])

        (*new_kv, avtb, iwt, sbq, sbk, sbv, saq, sak, sav, iq, ik2, gates) = _project(
            xp, pos_p, batch, w_packed, *norm_args, tm=tm_p, group_cols=seq)
        o_sb = _sb_prompt(sbq, sbk, sbv, batch, seq, tq=256)
        o_sa = _dsa_prompt(iq, iwt, saq, ik2, sak, avtb, batch, seq, tq=128, tk=512, nsub=2)
        xp = _tail(xp, o_sb, o_sa, gates, *tail_w, tm=tm_p)
        for lst, arr in zip(new_p, new_kv):
            arr = jnp.moveaxis(arr.reshape(batch, -1, min(arr.shape[1], HEAD_DIM), seq), 3, 1)
            lst.append(arr if arr.shape[2] > 1 else arr[:, :, 0])

        n_s = dec_batch * dec_seq
        (*new_kv, avtb, iwt, sbq, sbk, sbv, saq, sak, sav, iq, ik2, gates) = _project(
            xs, pos_s, dec_batch, w_packed, *norm_args, tm=tm_s, group_cols=n_s)
        b3 = lambda a: a.reshape(dec_batch, dec_seq, a.shape[-1])
        iw = iwt[0].T
        o_sb = _sb_sample(b3(sbq), b3(sbk), b3(sbv),
                          _pages_t(cache_sb_k[layer]), _pages_t(cache_sb_v[layer]), page_table)
        o_sa = _dsa_sample(_unpermute_heads(b3(saq)), b3(sak), b3(sav), b3(iq), b3(iw),
                           b3(ik2)[..., :IDX_DIM],
                           _pages_t(cache_sa_k[layer]), _pages_t(cache_sa_v[layer]),
                           _pages_t(cache_idx_k[layer]), page_table)
        o_sb = o_sb.reshape(dec_batch * dec_seq, SB_WIDTH).astype(BF16)
        o_sa = _unpermute_heads(o_sa).reshape(dec_batch * dec_seq, SA_WIDTH).astype(BF16)
        xs = _tail(xs, o_sb, o_sa, gates, *tail_w, tm=tm_s)
        for lst, arr in zip(new_s, new_kv):
            arr = jnp.moveaxis(arr.reshape(-1, min(arr.shape[1], HEAD_DIM), dec_batch, dec_seq), (2, 3), (0, 1))
            lst.append(arr if arr.shape[2] > 1 else arr[:, :, 0])

    return (xp.reshape(batch, seq, D_MODEL), xs.reshape(dec_batch, dec_seq, D_MODEL),
            *(jnp.stack(l) for l in new_p), *(jnp.stack(l) for l in new_s))
```

```python
import functools

import jax
import jax.numpy as jnp
from jax import lax
from jax.experimental import pallas as pl
from jax.experimental.pallas import tpu as pltpu

F32 = jnp.float32
BF16 = jnp.bfloat16
I32 = jnp.int32

D_MODEL = 1024
HEAD_DIM = 64
SB_HEADS = 8
SA_HEADS = 8
SA_KV_HEADS = 4
IDX_HEADS = 8
IDX_DIM = 64
SB_WIDTH = SB_HEADS * HEAD_DIM
SA_WIDTH = SA_HEADS * HEAD_DIM
SA_KV_WIDTH = SA_KV_HEADS * HEAD_DIM
IDX_WIDTH = IDX_HEADS * IDX_DIM
TOP_K_MAX = 256
ROT_DIM = HEAD_DIM // 4
ROPE_THETA = 500000.0
D_FF = 4 * D_MODEL
PAGE_SIZE = 128
RMS_EPS = 1e-6
QK_SCALE = HEAD_DIM ** -0.5
SA_Q_SCALE = QK_SCALE * 1.4426950408889634
IDX_SCALE = IDX_DIM ** -0.5
IDX_W_SCALE = IDX_HEADS ** -0.5

LANES = 128
VMEM_LIMIT = 56 * 1024 * 1024

C_SQ, C_SK, C_SV = 0, 512, 1024
C_AQ, C_AK, C_AV = 1536, 2048, 2304
C_IQ, C_IK2, C_IW = 2560, 3072, 3200
C_GATE = 3328
C_END = C_GATE + 2 * D_MODEL
SA_Q_PERM = (0, 2, 1, 3, 4, 6, 5, 7)

SURV_FLOOR = -105.0
NEG_BIG = -1e30
INT_MIN = -2 ** 31

SB_SAMPLE_HEAD_PAGES = 4
DSA_SEL_GROUP = 32
DSA_SAMPLE_GROUP = 64


TOPK_GROUPS = 256


def _largest_divisor(n, cap):
    return max(d for d in range(1, min(n, cap) + 1) if n % d == 0)


def _cparams(sem, vmem=VMEM_LIMIT):
    return pltpu.CompilerParams(dimension_semantics=sem, vmem_limit_bytes=vmem)


def _rope_chunk(y, cos, sin_a, sin_b):
    up = pltpu.roll(y, LANES - ROT_DIM // 2, 1)
    dn = pltpu.roll(y, ROT_DIM // 2, 1)
    return y * cos + up * sin_a + dn * sin_b


def _head_rms_chunk(y, e_mat, gain):
    y2 = y * y
    hi = y2.astype(BF16)
    lo = (y2 - hi.astype(F32)).astype(BF16)
    ms = (jnp.dot(hi, e_mat, preferred_element_type=F32)
          + jnp.dot(lo, e_mat, preferred_element_type=F32))
    return y * lax.rsqrt(ms + RMS_EPS) * gain


def _proj_kernel(x_ref, g1_ref, w_ref, qg_ref, kg_ref, ig_ref, bg_ref, cos_ref, sa_ref, sb_ref,
                 e_ref,
                 skt_ref, svt_ref, akt_ref, avt_ref, ikt_ref, avtb_ref, iwt_ref,
                 sbq_ref, sbk_ref, sbv_ref, saq_ref, sak_ref, sav_ref, iq_ref, ik2_ref, gate_ref):
    x = x_ref[...]
    ms = jnp.mean(x * x, axis=-1, keepdims=True)
    h = (x * lax.rsqrt(ms + RMS_EPS) * g1_ref[...]).astype(BF16)

    def mm(lo, hi):
        return jnp.dot(h, w_ref[:, lo:hi], preferred_element_type=F32)

    cos, sin_a, sin_b = cos_ref[...], sa_ref[...], sb_ref[...]
    e_mat = e_ref[...]

    p = mm(C_SQ, C_SK)
    sbq_ref[...] = (p * QK_SCALE).astype(BF16)
    p = mm(C_SK, C_SV)
    skt_ref[0] = p.T
    sbk_ref[...] = p.astype(BF16)
    p = mm(C_SV, C_AQ)
    svt_ref[0] = p.T
    sbv_ref[...] = p.astype(BF16)

    p = mm(C_AQ, C_AK)
    for c in range(SA_WIDTH // LANES):
        sl = slice(c * LANES, (c + 1) * LANES)
        y = _rope_chunk(_head_rms_chunk(p[:, sl], e_mat, qg_ref[...]), cos, sin_a, sin_b)
        saq_ref[:, sl] = (y * SA_Q_SCALE).astype(BF16)
    p = mm(C_AK, C_AV)
    for c in range(SA_KV_WIDTH // LANES):
        sl = slice(c * LANES, (c + 1) * LANES)
        y = _rope_chunk(_head_rms_chunk(p[:, sl], e_mat, kg_ref[...]), cos, sin_a, sin_b)
        akt_ref[0, sl, :] = y.T
        sak_ref[:, sl] = y.astype(BF16)
    p = mm(C_AV, C_IQ)
    pt = p.T
    avt_ref[0] = pt
    avtb_ref[0] = pt.astype(BF16)
    sav_ref[...] = p.astype(BF16)

    p = mm(C_IQ, C_IK2)
    for c in range(IDX_WIDTH // LANES):
        sl = slice(c * LANES, (c + 1) * LANES)
        iq_ref[:, sl] = (_rope_chunk(p[:, sl], cos, sin_a, sin_b) * IDX_SCALE).astype(BF16)
    p = mm(C_IK2, C_GATE)
    y = _rope_chunk(_head_rms_chunk(p[:, :LANES], e_mat, ig_ref[...]), cos, sin_a, sin_b)
    ikt_ref[0] = y.T[:IDX_DIM]
    ik2_ref[...] = y.astype(BF16)
    iwt_ref[0] = p[:, LANES:].T[:IDX_HEADS] * IDX_W_SCALE

    p = mm(C_GATE, C_END) + bg_ref[...]
    gate_ref[...] = jax.nn.sigmoid(p)


def _rope_tables(pos):
    half = ROT_DIM // 2
    inv = 1.0 / (ROPE_THETA ** (jnp.arange(0, ROT_DIM, 2, dtype=F32) / ROT_DIM))
    ang = pos[:, None] * inv[None, :]
    cos, sin = jnp.cos(ang), jnp.sin(ang)
    t = pos.shape[0]
    zeros = jnp.zeros((t, HEAD_DIM - ROT_DIM), F32)
    zh = jnp.zeros((t, half), F32)
    cos_h = jnp.concatenate([cos, cos, zeros + 1.0], axis=1)
    sa_h = jnp.concatenate([-sin, zh, zeros], axis=1)
    sb_h = jnp.concatenate([zh, sin, zeros], axis=1)
    return tuple(jnp.concatenate([a, a], axis=1) for a in (cos_h, sa_h, sb_h))


def _pack_w_in(w_in):
    o = 0
    parts = []
    for n in (SB_WIDTH, SB_WIDTH, SB_WIDTH, SA_WIDTH, SA_KV_WIDTH, SA_KV_WIDTH, IDX_WIDTH,
              IDX_DIM, IDX_HEADS, 2 * D_MODEL):
        parts.append(w_in[:, o:o + n])
        o += n
    sq, sk, sv, aq, ak, av, iq, ik, iw, gates = parts
    aq = aq.reshape(D_MODEL, SA_HEADS, HEAD_DIM)[:, jnp.array(SA_Q_PERM)].reshape(D_MODEL, SA_WIDTH)
    pad = jnp.zeros((D_MODEL, C_GATE - C_IW - IDX_HEADS), w_in.dtype)
    return jnp.concatenate([sq, sk, sv, aq, ak, av, iq, ik, ik, iw, pad, gates], axis=1).astype(BF16)


def _project(x2d, pos, n_rep, w_packed, g1, qg, kg, ig, bg, tm, group_cols):
    r = x2d.shape[0]
    t = pos.shape[0]
    assert r == n_rep * t and r % tm == 0 and (t % tm == 0 or tm % t == 0)
    assert r % group_cols == 0 and group_cols % tm == 0
    ntc = group_cols // tm
    cos, sin_a, sin_b = _rope_tables(pos)
    if tm > t:
        cos, sin_a, sin_b = (jnp.tile(a, (tm // t, 1)) for a in (cos, sin_a, sin_b))
    nt = cos.shape[0] // tm
    e_mat = (jnp.arange(LANES)[:, None] // HEAD_DIM == jnp.arange(LANES)[None, :] // HEAD_DIM)
    e_mat = (e_mat.astype(F32) / HEAD_DIM).astype(BF16)

    def row(width):
        return pl.BlockSpec((tm, width), lambda i: (i, 0))

    def const(shape):
        return pl.BlockSpec(shape, lambda i: (0, 0))

    tab = pl.BlockSpec((tm, LANES), lambda i: (i % nt, 0))
    tile2 = lambda g: jnp.concatenate([g, g]).reshape(1, LANES)
    out_heights = [(SB_WIDTH, F32), (SB_WIDTH, F32), (SA_KV_WIDTH, F32), (SA_KV_WIDTH, F32),
                   (IDX_DIM, F32), (SA_KV_WIDTH, BF16), (IDX_HEADS, F32)]
    out_widths = [(SB_WIDTH, BF16), (SB_WIDTH, BF16), (SB_WIDTH, BF16),
                  (SA_WIDTH, BF16), (SA_KV_WIDTH, BF16), (SA_KV_WIDTH, BF16),
                  (IDX_WIDTH, BF16), (LANES, BF16), (2 * D_MODEL, F32)]
    return pl.pallas_call(
        _proj_kernel,
        grid=(r // tm,),
        in_specs=[row(D_MODEL), const((1, D_MODEL)),
                  pl.BlockSpec((D_MODEL, C_END), lambda i: (0, 0), pipeline_mode=pl.Buffered(1)),
                  const((1, LANES)), const((1, LANES)), const((1, LANES)), const((1, 2 * D_MODEL)),
                  tab, tab, tab, const((LANES, LANES))],
        out_specs=[pl.BlockSpec((1, h_, tm), lambda i: (i // ntc, 0, i % ntc)) for h_, _ in out_heights]
                  + [row(w_) for w_, _ in out_widths],
        out_shape=[jax.ShapeDtypeStruct((r // group_cols, h_, group_cols), dt) for h_, dt in out_heights]
                  + [jax.ShapeDtypeStruct((r, w_), dt) for w_, dt in out_widths],
        compiler_params=_cparams(("parallel",)),
        name="proj",
    )(x2d, g1.reshape(1, D_MODEL), w_packed, tile2(qg), tile2(kg), tile2(ig),
      bg.reshape(1, 2 * D_MODEL), cos, sin_a, sin_b, e_mat)


def _stick_break_tile(z, mask, surv, tri):
    t = jnp.log1p(jnp.exp(-jnp.abs(z)))
    l = -(jnp.maximum(z, 0.0) + t)
    if mask is not None:
        l = jnp.where(mask, l, 0.0)
    ls = jnp.minimum(z, 0.0) - t
    hi = l.astype(BF16)
    lo = (l - hi.astype(F32)).astype(BF16)
    ex = jnp.dot(hi, tri, preferred_element_type=F32) + jnp.dot(lo, tri, preferred_element_type=F32)
    a = jnp.exp(ls + ex + surv)
    if mask is not None:
        a = jnp.where(mask, a, 0.0)
    return a, surv + jnp.sum(l, axis=1, keepdims=True)


def _newer_tri(n):
    j = lax.broadcasted_iota(I32, (n, n), 0)
    s = lax.broadcasted_iota(I32, (n, n), 1)
    return (j > s).astype(BF16)


def _sb_prompt_kernel(q_ref, k_ref, v_ref, o_ref, acc_ref, surv_ref, *, tq):
    i = pl.program_id(2)
    t0 = i * tq
    lane = lax.broadcasted_iota(I32, (tq, LANES), 1)
    row_t = t0 + lax.broadcasted_iota(I32, (tq, tq), 0)
    col0 = lax.broadcasted_iota(I32, (tq, tq), 1)
    tri = _newer_tri(tq)
    q = q_ref[...]
    heads = range(2)
    qm = [jnp.where((lane >= HEAD_DIM) == bool(h), q, jnp.zeros_like(q)) for h in heads]
    acc_ref[...] = jnp.zeros_like(acc_ref)
    surv_ref[...] = jnp.zeros_like(surv_ref)

    def body(carry):
        kt, _ = carry
        start = pl.multiple_of(kt * tq, tq)
        kb = k_ref[pl.ds(start, tq), :]
        vb = v_ref[pl.ds(start, tq), :]
        mask = (start + col0) < row_t
        z = [lax.dot_general(qm[h], kb, (((1,), (1,)), ((), ())), preferred_element_type=F32)
             for h in heads]
        l, ls, ex = [], [], []
        for h in heads:
            t = jnp.log1p(jnp.exp(-jnp.abs(z[h])))
            l.append(jnp.where(mask, -(jnp.maximum(z[h], 0.0) + t), 0.0))
            ls.append(jnp.minimum(z[h], 0.0) - t)
            hi = l[h].astype(BF16)
            lo = (l[h] - hi.astype(F32)).astype(BF16)
            ex.append(jnp.dot(hi, tri, preferred_element_type=F32)
                      + jnp.dot(lo, tri, preferred_element_type=F32))
        acc, surv = [], []
        for h in heads:
            a = jnp.where(mask, jnp.exp(ls[h] + ex[h] + surv_ref[h]), 0.0)
            acc.append(acc_ref[h] + jnp.dot(a.astype(BF16), vb, preferred_element_type=F32))
            surv.append(surv_ref[h] + jnp.sum(l[h], axis=1, keepdims=True))
        acc_ref[...] = jnp.stack(acc)
        surv_ref[...] = jnp.stack(surv)
        return kt - 1, jnp.maximum(jnp.max(surv[0]), jnp.max(surv[1]))

    lax.while_loop(lambda c: (c[0] >= 0) & (c[1] > SURV_FLOOR), body, (i, jnp.float32(0.0)))
    o_ref[...] = jnp.where(lane < HEAD_DIM, acc_ref[0], acc_ref[1]).astype(o_ref.dtype)


def _sb_prompt(q, k, v, batch, seq, tq):
    nq = seq // tq
    n_chunks = SB_WIDTH // LANES
    qspec = pl.BlockSpec((tq, LANES), lambda b, c, i: (b * nq + i, c))
    kvspec = pl.BlockSpec((seq, LANES), lambda b, c, i: (b, c))
    return pl.pallas_call(
        functools.partial(_sb_prompt_kernel, tq=tq),
        grid=(batch, n_chunks, nq),
        in_specs=[qspec, kvspec, kvspec],
        out_specs=qspec,
        out_shape=jax.ShapeDtypeStruct((batch * seq, SB_WIDTH), BF16),
        scratch_shapes=[pltpu.VMEM((2, tq, LANES), F32), pltpu.VMEM((2, tq, 1), F32)],
        compiler_params=_cparams(("parallel", "parallel", "arbitrary")),
        name="sb_prompt",
    )(q, k, v)


def _sortable(x):
    bits = lax.bitcast_convert_type(x, I32)
    key = jnp.where(bits < 0, bits ^ jnp.int32(0x7FFFFFFF), bits)
    return jnp.where(x == 0.0, jnp.int32(0), key)


def _select_topk(keys_ref, n_tiles, tk, k_row, n_index_bits, key_axis=1):
    qn = keys_ref.shape[1 - key_axis]
    sign = jnp.int32(INT_MIN)
    tile_shape = (qn, tk) if key_axis == 1 else (tk, qn)
    fold = LANES if key_axis == 1 else 8
    vec = (qn, 1) if key_axis == 1 else (1, qn)

    def tile_at(kt):
        start = pl.multiple_of(kt * tk, tk)
        idx = (slice(None), pl.ds(start, tk)) if key_axis == 1 else (pl.ds(start, tk), slice(None))
        return start, idx

    def count(pred):
        def tile_body(kt, acc):
            start, idx = tile_at(kt)
            m = pred(keys_ref[idx], start).astype(I32)
            if key_axis == 1:
                for c in range(tk // fold):
                    acc = acc + m[:, c * fold:(c + 1) * fold]
                return acc
            return acc + jnp.sum(m.reshape(tk // fold, fold, qn), axis=0)
        acc0 = jnp.zeros((qn, fold) if key_axis == 1 else (fold, qn), I32)
        acc = lax.fori_loop(0, n_tiles, tile_body, acc0)
        if key_axis == 0:
            for shift in (4, 2, 1):
                acc = acc + pltpu.roll(acc, shift, 0)
            return acc[:1]
        return jnp.sum(acc.astype(F32), axis=1, keepdims=True).astype(I32)

    bracket = None
    if key_axis == 0 and tk % TOPK_GROUPS == 0:
        def gmax_body(kt, acc):
            _, idx = tile_at(kt)
            return jnp.maximum(acc, jnp.max(keys_ref[idx].reshape(tk // TOPK_GROUPS, TOPK_GROUPS, qn), axis=0))
        gmax = lax.fori_loop(0, n_tiles, gmax_body, jnp.full((TOPK_GROUPS, qn), INT_MIN, I32))
        gmax = gmax.reshape(TOPK_GROUPS // fold, fold, qn)
        hi_b, lo_b = jnp.max(gmax, axis=0), jnp.min(gmax, axis=0)
        for shift in (4, 2, 1):
            hi_b = jnp.maximum(hi_b, pltpu.roll(hi_b, shift, 0))
            lo_b = jnp.minimum(lo_b, pltpu.roll(lo_b, shift, 0))
        bracket = (lo_b[:1], hi_b[:1])

    zeros = jnp.zeros(vec, I32)
    b_first, base, span = jnp.int32(31), zeros, zeros - 1
    if bracket is not None:
        base = bracket[0] ^ sign
        span = (bracket[1] ^ sign) - base
        expo = lax.shift_right_logical(lax.bitcast_convert_type(jnp.maximum(span, 1).astype(F32), I32),
                                       jnp.int32(23)) - 127
        top = jnp.where(span < 0, 31, expo)
        b_first = jnp.minimum(jnp.max(top.astype(F32)).astype(I32), 31)

    def one_bit(b, prefix, thr, done):
        bit = jnp.where(b >= 0, lax.shift_left(jnp.int32(1), jnp.maximum(b, 0)), 0)
        cand_off = prefix | bit
        in_span = (cand_off ^ sign) <= (span ^ sign)
        cand = (base + cand_off) ^ sign
        cnt = count(lambda kk, start: kk >= cand)
        live = (done == 0) & in_span
        prefix = jnp.where(live & (cnt >= k_row), cand_off, prefix)
        hit = live & (cnt == k_row)
        return prefix, jnp.where(hit, cand, thr), jnp.where(hit, 1, done)

    def bit_body(carry):
        b, prefix, thr, done, _ = carry
        for u in range(2):
            prefix, thr, done = one_bit(b - u, prefix, thr, done)
        return b - 2, prefix, thr, done, jnp.min(done.astype(F32)).astype(I32)

    _, prefix, thr, done, all_done = lax.while_loop(
        lambda c: (c[0] >= 0) & (c[4] == 0), bit_body,
        (b_first, zeros, zeros, zeros, jnp.int32(0)))
    thr = jnp.where(done == 0, (base + prefix) ^ sign, thr)

    if key_axis == 0:
        @pl.when(all_done == 0)
        def _():
            tied = done == 0
            want = (k_row - count(lambda kk, start: kk > thr)).astype(F32)
            half = tk // 2
            r = lax.broadcasted_iota(I32, (half, half), 0)
            c = lax.broadcasted_iota(I32, (half, half), 1)
            tri = (c <= r).astype(BF16)

            def fix_body(kt, seen):
                start, _ = tile_at(kt)
                for u in range(2):
                    idx = (pl.ds(pl.multiple_of(start + u * half, half), half), slice(None))
                    kk = keys_ref[idx]
                    tie = tied & (kk == thr)
                    rank = seen + jnp.dot(tri, tie.astype(BF16), preferred_element_type=F32)
                    keys_ref[idx] = jnp.where(tie & (rank > want), thr - 1, kk)
                    seen = rank[half - 1:half]
                return seen
            lax.fori_loop(0, n_tiles, fix_body, jnp.zeros(vec, F32))
        return thr

    @pl.when(all_done == 0)
    def _():
        tied = done == 0
        want = k_row - count(lambda kk, start: kk > thr)
        col0 = lax.broadcasted_iota(I32, tile_shape, key_axis)

        def ties_before(x):
            return count(lambda kk, start: (kk == thr) & ((start + col0) < x))

        def idx_body(j, x):
            cand = x | lax.shift_left(jnp.int32(1), n_index_bits - 1 - j)
            return jnp.where(ties_before(cand) < want, cand, x)

        x = lax.fori_loop(0, n_index_bits, idx_body, zeros)

        def fix_body(kt, c):
            start, idx = tile_at(kt)
            kk = keys_ref[idx]
            drop = tied & (kk == thr) & ((start + col0) > x)
            keys_ref[idx] = jnp.where(drop, thr - 1, kk)
            return c
        lax.fori_loop(0, n_tiles, fix_body, 0)

    return thr


def _dsa_prompt_kernel(iq_ref, iwt_ref, aq_ref, ik_ref, ak_ref, avt_ref, o_ref,
                       keys_ref, iqt_ref, aqt_ref, m_ref, l_ref, acc_ref,
                       *, tq, tk, nsub, k_top, n_index_bits):
    assert tq == LANES
    i = pl.program_id(1)
    t0 = i * tq
    n_super = (t0 + tq + nsub * tk - 1) // (nsub * tk)
    n_tiles = nsub * n_super
    dim_row = lax.broadcasted_iota(I32, (LANES, tq), 0)
    halfmask = [(dim_row >= HEAD_DIM) == bool(e) for e in range(2)]
    t_query = t0 + lax.broadcasted_iota(I32, (1, tq), 1)

    def chunk_t(ref, c):
        return ref[:, c * LANES:(c + 1) * LANES].astype(F32).T

    for c in range(IDX_WIDTH // LANES):
        qt = chunk_t(iq_ref, c)
        for e in range(2):
            iqt_ref[c, :, e * tq:(e + 1) * tq] = jnp.where(halfmask[e], qt, 0.0).astype(BF16)

    def score_body(ks, c):
        starts = [pl.multiple_of((ks * nsub + u) * tk, tk) for u in range(nsub)]
        dots = [[jnp.dot(ik_ref[pl.ds(st, tk), :], iqt_ref[c2], preferred_element_type=F32)
                 for c2 in range(IDX_WIDTH // LANES)] for st in starts]
        for u, st in enumerate(starts):
            score = jnp.zeros((tk, tq), F32)
            for c2 in range(IDX_WIDTH // LANES):
                for e in range(2):
                    h = 2 * c2 + e
                    s = dots[u][c2][:, e * tq:(e + 1) * tq]
                    score = score + jnp.maximum(s, 0.0) * iwt_ref[0, h:h + 1, :]
            pos = st + lax.broadcasted_iota(I32, (tk, tq), 0)
            keys_ref[pl.ds(st, tk), :] = jnp.where(pos <= t_query, _sortable(score), jnp.int32(INT_MIN))
        return c
    lax.fori_loop(0, n_super, score_body, 0)

    k_row = jnp.minimum(t_query + 1, k_top)
    thr = _select_topk(keys_ref, (t0 + tq + tk - 1) // tk, tk, k_row, n_index_bits, key_axis=0)

    for c in range(SA_WIDTH // LANES):
        qt = chunk_t(aq_ref, c)
        p, j = c // 2, c % 2
        for e in range(2):
            aqt_ref[2 * p + e, :, j * tq:(j + 1) * tq] = jnp.where(halfmask[e], qt, 0.0).astype(BF16)
    m_ref[...] = jnp.full_like(m_ref, NEG_BIG)
    l_ref[...] = jnp.zeros_like(l_ref)
    acc_ref[...] = jnp.zeros_like(acc_ref)

    ones_rows = jnp.ones((16, tk), BF16)

    def att_body(ks, c):
        starts = [pl.multiple_of((ks * nsub + u) * tk, tk) for u in range(nsub)]
        logits = [[jnp.dot(ak_ref[pl.ds(st, tk), (g // 2) * LANES:(g // 2 + 1) * LANES], aqt_ref[g],
                           preferred_element_type=F32) for g in range(SA_KV_HEADS)]
                  for st in starts]
        m_all, l_all = m_ref[...], l_ref[...]
        m_cur = [m_all[hh:hh + 1] for hh in range(SA_HEADS)]
        l_cur = [l_all[hh:hh + 1] for hh in range(SA_HEADS)]
        acc_cur = [acc_ref[hh * HEAD_DIM:(hh + 1) * HEAD_DIM] for hh in range(SA_HEADS)]
        for u, st in enumerate(starts):
            sel = keys_ref[pl.ds(st, tk), :] >= thr
            probs, alphas = [], []
            for hh in range(SA_HEADS):
                g, j = hh // 2, hh % 2
                s = jnp.where(sel, logits[u][g][:, j * tq:(j + 1) * tq], NEG_BIG)
                m_new = jnp.maximum(m_cur[hh], jnp.max(s, axis=0, keepdims=True))
                alpha = jnp.exp2(m_cur[hh] - m_new)
                m_cur[hh] = m_new
                probs.append(jnp.exp2(s - m_new).astype(BF16))
                alphas.append(alpha)
            for hh in range(SA_HEADS):
                g = hh // 2
                vt = avt_ref[0, g * HEAD_DIM:(g + 1) * HEAD_DIM, pl.ds(st, tk)]
                pv = jnp.dot(jnp.concatenate([vt, ones_rows], axis=0), probs[hh],
                             preferred_element_type=F32)
                acc_cur[hh] = alphas[hh] * acc_cur[hh] + pv[:HEAD_DIM]
                l_cur[hh] = alphas[hh] * l_cur[hh] + pv[HEAD_DIM:HEAD_DIM + 1]
        m_ref[...] = jnp.concatenate(m_cur, axis=0)
        l_ref[...] = jnp.concatenate(l_cur, axis=0)
        acc_ref[...] = jnp.concatenate(acc_cur, axis=0)
        return c
    lax.fori_loop(0, n_super, att_body, 0)

    def head_out(hh):
        return acc_ref[hh * HEAD_DIM:(hh + 1) * HEAD_DIM] / l_ref[hh:hh + 1]

    for p in range(SA_KV_WIDTH // LANES):
        for j in range(2):
            ot = jnp.concatenate([head_out(2 * (2 * p) + j), head_out(2 * (2 * p + 1) + j)], axis=0)
            o_ref[:, (2 * p + j) * LANES:(2 * p + j + 1) * LANES] = ot.T.astype(o_ref.dtype)


def _dsa_prompt(iq, iwt, aq, ik2, ak, avt, batch, seq, tq, tk, nsub):
    assert seq % (nsub * tk) == 0
    nq = seq // tq
    k_top = min(TOP_K_MAX, seq // 4)
    n_index_bits = max(1, seq.bit_length())
    qspec = lambda w: pl.BlockSpec((tq, w), lambda b, i: (b * nq + i, 0))
    kspec = lambda w: pl.BlockSpec((seq, w), lambda b, i: (b, 0), pipeline_mode=pl.Buffered(1))
    return pl.pallas_call(
        functools.partial(_dsa_prompt_kernel, tq=tq, tk=tk, nsub=nsub, k_top=k_top,
                          n_index_bits=n_index_bits),
        grid=(batch, nq),
        in_specs=[qspec(IDX_WIDTH),
                  pl.BlockSpec((1, IDX_HEADS, tq), lambda b, i: (b, 0, i)),
                  qspec(SA_WIDTH), kspec(LANES), kspec(SA_KV_WIDTH),
                  pl.BlockSpec((1, SA_KV_WIDTH, seq), lambda b, i: (b, 0, 0),
                               pipeline_mode=pl.Buffered(1))],
        out_specs=qspec(SA_WIDTH),
        out_shape=jax.ShapeDtypeStruct((batch * seq, SA_WIDTH), BF16),
        scratch_shapes=[pltpu.VMEM((seq, tq), I32),
                        pltpu.VMEM((IDX_HEADS // 2, LANES, 2 * tq), BF16),
                        pltpu.VMEM((SA_KV_HEADS, LANES, 2 * tq), BF16),
                        pltpu.VMEM((SA_HEADS, tq), F32), pltpu.VMEM((SA_HEADS, tq), F32),
                        pltpu.VMEM((SA_HEADS * HEAD_DIM, tq), F32)],
        compiler_params=_cparams(("parallel", "arbitrary")),
        name="dsa_prompt",
    )(iq, iwt, aq, ik2, ak, avt)


def _sb_page_step(q, kc_ref, vc_ref, o_ref, surv_ref, tri):
    @pl.when(jnp.max(surv_ref[0]) > SURV_FLOOR)
    def _():
        kt = kc_ref[0].astype(BF16)
        vt = vc_ref[0].astype(BF16)
        z = jnp.dot(q, kt, preferred_element_type=F32)
        a, surv = _stick_break_tile(z, None, surv_ref[0][:, :1], tri)
        o_ref[0] += lax.dot_general(a.astype(BF16), vt, (((1,), (1,)), ((), ())),
                                    preferred_element_type=F32)
        surv_ref[0] = jnp.broadcast_to(surv, surv_ref.shape[1:])


def _sb_sample_head_kernel(pt_ref, q_ref, kn_ref, vn_ref, *rest, n_tok, n_head_pages):
    pages, (o_ref, surv_ref) = rest[:2 * n_head_pages], rest[2 * n_head_pages:]
    rows = q_ref.shape[1]
    q = q_ref[0]
    tri = _newer_tri(PAGE_SIZE)
    z = lax.dot_general(q, kn_ref[0], (((1,), (1,)), ((), ())), preferred_element_type=F32)
    t = lax.broadcasted_iota(I32, (rows, PAGE_SIZE), 0) % n_tok
    mask = lax.broadcasted_iota(I32, (rows, PAGE_SIZE), 1) < t
    a, surv = _stick_break_tile(z, mask, jnp.zeros((rows, 1), F32), tri)
    o_ref[0] = jnp.dot(a.astype(BF16), vn_ref[0], preferred_element_type=F32)
    surv_ref[0] = jnp.broadcast_to(surv, surv_ref.shape[1:])
    for i in range(n_head_pages):
        _sb_page_step(q, pages[2 * i], pages[2 * i + 1], o_ref, surv_ref, tri)


def _sb_sample_tail_kernel(pt_ref, q_ref, acc_ref, sin_ref, kc_ref, vc_ref, o_ref, surv_ref):
    @pl.when(pl.program_id(1) == 0)
    def _():
        o_ref[...] = acc_ref[...]
        surv_ref[...] = sin_ref[...]
    _sb_page_step(q_ref[0], kc_ref, vc_ref, o_ref, surv_ref, _newer_tri(PAGE_SIZE))


def _block_diag_queries(q, n_heads_q, n_heads_kv):
    group = n_heads_q // n_heads_kv
    blocks = []
    for h in range(n_heads_q):
        kv = h // group
        qh = q[:, :, h * HEAD_DIM:(h + 1) * HEAD_DIM]
        blocks.append(jnp.pad(qh, ((0, 0), (0, 0), (kv * HEAD_DIM, (n_heads_kv - 1 - kv) * HEAD_DIM))))
    return jnp.concatenate(blocks, axis=1)


def _take_block_diag(acc, n_tok, n_heads_q, n_heads_kv):
    group = n_heads_q // n_heads_kv
    return jnp.concatenate(
        [acc[:, h * n_tok:(h + 1) * n_tok, (h // group) * HEAD_DIM:(h // group + 1) * HEAD_DIM]
         for h in range(n_heads_q)], axis=2)


def _pages_t(cache):
    n_phys, page = cache.shape[:2]
    return jnp.moveaxis(cache, 1, -1).reshape(n_phys, -1, page)


def _pad_rows(a, rows):
    return jnp.pad(a, ((0, 0), (0, rows - a.shape[1]), (0, 0)))


def _sb_sample(q, k, v, cache_k, cache_v, page_table):
    b, n_tok, _ = q.shape
    n_pages = page_table.shape[1]
    rows = SB_HEADS * n_tok
    qbd = _block_diag_queries(q, SB_HEADS, SB_HEADS)
    kn, vn = _pad_rows(k, PAGE_SIZE), _pad_rows(v, PAGE_SIZE)
    n_head = min(SB_SAMPLE_HEAD_PAGES, n_pages)
    out_shape = [jax.ShapeDtypeStruct((b, rows, SB_WIDTH), F32),
                 jax.ShapeDtypeStruct((b, rows, LANES), F32)]

    per_b1 = lambda bi, pt: (bi, 0, 0)
    page_specs = []
    for i in range(n_head):
        spec = pl.BlockSpec((1, SB_WIDTH, PAGE_SIZE), lambda bi, pt, i=i: (pt[bi, n_pages - 1 - i], 0, 0))
        page_specs += [spec, spec]
    acc, surv = pl.pallas_call(
        functools.partial(_sb_sample_head_kernel, n_tok=n_tok, n_head_pages=n_head),
        grid_spec=pltpu.PrefetchScalarGridSpec(
            num_scalar_prefetch=1,
            grid=(b,),
            in_specs=[pl.BlockSpec((1, rows, SB_WIDTH), per_b1),
                      pl.BlockSpec((1, PAGE_SIZE, SB_WIDTH), per_b1),
                      pl.BlockSpec((1, PAGE_SIZE, SB_WIDTH), per_b1)] + page_specs,
            out_specs=[pl.BlockSpec((1, rows, SB_WIDTH), per_b1), pl.BlockSpec((1, rows, LANES), per_b1)]),
        out_shape=out_shape,
        compiler_params=_cparams(("parallel",)),
        name="sb_sample_head",
    )(page_table, qbd, kn, vn, *([cache_k, cache_v] * n_head))

    n_tail = n_pages - n_head
    if n_tail > 0:
        per_b = lambda bi, j, pt: (bi, 0, 0)
        page = lambda bi, j, pt: (pt[bi, n_tail - 1 - j], 0, 0)

        def older_pages(acc, surv):
            return pl.pallas_call(
                _sb_sample_tail_kernel,
                grid_spec=pltpu.PrefetchScalarGridSpec(
                    num_scalar_prefetch=1,
                    grid=(b, n_tail),
                    in_specs=[pl.BlockSpec((1, rows, SB_WIDTH), per_b),
                              pl.BlockSpec((1, rows, SB_WIDTH), per_b),
                              pl.BlockSpec((1, rows, LANES), per_b),
                              pl.BlockSpec((1, SB_WIDTH, PAGE_SIZE), page),
                              pl.BlockSpec((1, SB_WIDTH, PAGE_SIZE), page)],
                    out_specs=[pl.BlockSpec((1, rows, SB_WIDTH), per_b),
                               pl.BlockSpec((1, rows, LANES), per_b)]),
                out_shape=out_shape,
                compiler_params=_cparams(("parallel", "arbitrary")),
                name="sb_sample_tail",
            )(page_table, qbd, acc, surv, cache_k, cache_v)[0]

        acc = lax.cond(jnp.max(surv) > SURV_FLOOR, older_pages, lambda acc, surv: acc, acc, surv)
    return _take_block_diag(acc, n_tok, SB_HEADS, SB_HEADS)


def _dsa_sel_kernel(pt_ref, iq_ref, iw_ref, ikn_ref, *rest, n_tok, n_pages, group, k_top, n_index_bits):
    pages, (sel_ref, keys_ref) = rest[:group], rest[group:]
    j = pl.program_id(1)
    width = group * PAGE_SIZE

    def scores(s):
        s = jnp.maximum(s, 0.0) * iw_ref[0]
        return jnp.sum(s.reshape(IDX_HEADS, n_tok, s.shape[-1]), axis=0)

    kt = jnp.concatenate([p_ref[0].astype(BF16) for p_ref in pages], axis=1)
    key = _sortable(scores(jnp.dot(iq_ref[0], kt, preferred_element_type=F32)))
    keys_ref[:, pl.ds(pl.multiple_of(j * width, width), width)] = key

    @pl.when(j == n_pages // group - 1)
    def _():
        t = lax.broadcasted_iota(I32, (n_tok, PAGE_SIZE), 0)
        col = lax.broadcasted_iota(I32, (n_tok, PAGE_SIZE), 1)
        s_new = lax.dot_general(iq_ref[0], ikn_ref[0], (((1,), (1,)), ((), ())),
                                preferred_element_type=F32)
        key_new = jnp.where(col <= t, _sortable(scores(s_new)), jnp.int32(INT_MIN))
        keys_ref[:, n_pages * PAGE_SIZE:] = key_new
        t_abs = n_pages * PAGE_SIZE + lax.broadcasted_iota(I32, (n_tok, 1), 0)
        k_row = jnp.minimum(t_abs + 1, k_top)
        thr = _select_topk(keys_ref, 1, keys_ref.shape[1], k_row, n_index_bits)
        sel_ref[0] = (keys_ref[...] >= thr).astype(F32)


def _dsa_sample_kernel(pt_ref, q_ref, sel_ref, seln_ref, kn_ref, vn_ref, *rest, n_tok, n_pages, group):
    k_pages, v_pages, (o_ref, m_ref, l_ref) = rest[:group], rest[group:2 * group], rest[2 * group:]
    j = pl.program_id(1)
    rows = SA_HEADS * n_tok
    nt_dims = (((1,), (1,)), ((), ()))

    @pl.when(j == 0)
    def _():
        m_ref[...] = jnp.full_like(m_ref, NEG_BIG)
        l_ref[...] = jnp.zeros_like(l_ref)
        o_ref[...] = jnp.zeros_like(o_ref)

    def step(s, sel, pv):
        n = s.shape[-1]
        sel = jnp.broadcast_to(sel[None], (SA_HEADS, n_tok, n)).reshape(rows, n)
        s = jnp.where(sel > 0.5, s, NEG_BIG)
        m_old = m_ref[...]
        m_new = jnp.maximum(m_old, jnp.max(s, axis=1, keepdims=True))
        alpha = jnp.exp2(m_old - m_new)
        pr = jnp.exp2(s - m_new)
        l_ref[...] = alpha * l_ref[...] + jnp.sum(pr, axis=1, keepdims=True)
        o_ref[0] = alpha * o_ref[0] + pv(pr.astype(BF16))
        m_ref[...] = m_new

    kt = jnp.concatenate([p_ref[0].astype(BF16) for p_ref in k_pages], axis=1)
    vt = jnp.concatenate([p_ref[0].astype(BF16) for p_ref in v_pages], axis=1)
    step(jnp.dot(q_ref[0], kt, preferred_element_type=F32), sel_ref[0],
         lambda pr: lax.dot_general(pr, vt, nt_dims, preferred_element_type=F32))

    @pl.when(j == n_pages // group - 1)
    def _():
        step(lax.dot_general(q_ref[0], kn_ref[0], nt_dims, preferred_element_type=F32), seln_ref[0],
             lambda pr: jnp.dot(pr, vn_ref[0], preferred_element_type=F32))
        o_ref[0] = o_ref[0] / l_ref[...]


def _dsa_sample(aq, ak, av, iq, iw, ik, cache_k, cache_v, cache_ik, page_table):
    b, n_tok, _ = aq.shape
    n_pages = page_table.shape[1]
    n_keys = (n_pages + 1) * PAGE_SIZE
    k_top = min(TOP_K_MAX, (n_pages * PAGE_SIZE + n_tok) // 4)
    rows = SA_HEADS * n_tok
    per_b = lambda bi, j, pt: (bi, 0, 0)

    def page_specs(width, group):
        return [pl.BlockSpec((1, width, PAGE_SIZE), lambda bi, j, pt, i=i: (pt[bi, j * group + i], 0, 0))
                for i in range(group)]

    g_sel = _largest_divisor(n_pages, DSA_SEL_GROUP)
    iq_rows = iq.reshape(b, n_tok, IDX_HEADS, IDX_DIM).transpose(0, 2, 1, 3).reshape(b, rows, IDX_DIM)
    iw_rows = iw.transpose(0, 2, 1).reshape(b, rows, 1)
    sel = pl.pallas_call(
        functools.partial(_dsa_sel_kernel, n_tok=n_tok, n_pages=n_pages, group=g_sel, k_top=k_top,
                          n_index_bits=max(1, n_keys.bit_length())),
        grid_spec=pltpu.PrefetchScalarGridSpec(
            num_scalar_prefetch=1,
            grid=(b, n_pages // g_sel),
            in_specs=[pl.BlockSpec((1, rows, IDX_DIM), per_b),
                      pl.BlockSpec((1, rows, 1), per_b),
                      pl.BlockSpec((1, PAGE_SIZE, IDX_DIM), per_b)] + page_specs(IDX_DIM, g_sel),
            out_specs=pl.BlockSpec((1, n_tok, n_keys), per_b),
            scratch_shapes=[pltpu.VMEM((n_tok, n_keys), I32)]),
        out_shape=jax.ShapeDtypeStruct((b, n_tok, n_keys), F32),
        compiler_params=_cparams(("parallel", "arbitrary")),
        name="dsa_sel",
    )(page_table, iq_rows, iw_rows, _pad_rows(ik, PAGE_SIZE), *([cache_ik] * g_sel))

    g_att = _largest_divisor(n_pages, DSA_SAMPLE_GROUP)
    qbd = _block_diag_queries(aq, SA_HEADS, SA_KV_HEADS)
    acc = pl.pallas_call(
        functools.partial(_dsa_sample_kernel, n_tok=n_tok, n_pages=n_pages, group=g_att),
        grid_spec=pltpu.PrefetchScalarGridSpec(
            num_scalar_prefetch=1,
            grid=(b, n_pages // g_att),
            in_specs=[pl.BlockSpec((1, rows, SA_KV_WIDTH), per_b),
                      pl.BlockSpec((1, n_tok, g_att * PAGE_SIZE), lambda bi, j, pt: (bi, 0, j)),
                      pl.BlockSpec((1, n_tok, PAGE_SIZE), lambda bi, j, pt: (bi, 0, n_pages)),
                      pl.BlockSpec((1, PAGE_SIZE, SA_KV_WIDTH), per_b),
                      pl.BlockSpec((1, PAGE_SIZE, SA_KV_WIDTH), per_b)]
                     + page_specs(SA_KV_WIDTH, g_att) + page_specs(SA_KV_WIDTH, g_att),
            out_specs=pl.BlockSpec((1, rows, SA_KV_WIDTH), per_b),
            scratch_shapes=[pltpu.VMEM((rows, 1), F32), pltpu.VMEM((rows, 1), F32)]),
        out_shape=jax.ShapeDtypeStruct((b, rows, SA_KV_WIDTH), F32),
        compiler_params=_cparams(("parallel", "arbitrary")),
        name="dsa_sample",
    )(page_table, qbd, sel, sel, _pad_rows(ak, PAGE_SIZE), _pad_rows(av, PAGE_SIZE),
      *([cache_k] * g_att), *([cache_v] * g_att))
    return _take_block_diag(acc, n_tok, SA_HEADS, SA_KV_HEADS)


def _tail_kernel(x_ref, osb_ref, osa_ref, gate_ref, wsb_ref, wsa_ref, wo_ref, g2_ref, wup_ref,
                 wdn_ref, y_ref, *, ff_chunk):
    gate = gate_ref[...]
    merged = (gate[:, :D_MODEL] * jnp.dot(osb_ref[...], wsb_ref[...], preferred_element_type=F32)
              + gate[:, D_MODEL:] * jnp.dot(osa_ref[...], wsa_ref[...], preferred_element_type=F32))
    x1 = x_ref[...] + jnp.dot(merged.astype(BF16), wo_ref[...], preferred_element_type=F32)
    ms = jnp.mean(x1 * x1, axis=-1, keepdims=True)
    h = (x1 * lax.rsqrt(ms + RMS_EPS) * g2_ref[...]).astype(BF16)
    y = x1
    n_chunks = D_FF // ff_chunk
    up = lambda c: jnp.dot(h, wup_ref[:, c * ff_chunk:(c + 1) * ff_chunk], preferred_element_type=F32)
    u_next = up(0)
    for c in range(n_chunks):
        u, u_next = u_next, (up(c + 1) if c + 1 < n_chunks else None)
        u = jnp.maximum(u, 0.0)
        y = y + jnp.dot((u * u).astype(BF16), wdn_ref[c * ff_chunk:(c + 1) * ff_chunk, :],
                        preferred_element_type=F32)
    y_ref[...] = y


def _tail(x2d, o_sb, o_sa, gates, w_sb_out, w_sa_out, w_o, g2, w_up, w_down, tm):
    r = x2d.shape[0]
    row = lambda w: pl.BlockSpec((tm, w), lambda i: (i, 0))
    res = lambda shape: pl.BlockSpec(shape, lambda i: (0, 0), pipeline_mode=pl.Buffered(1))
    return pl.pallas_call(
        functools.partial(_tail_kernel, ff_chunk=1024),
        grid=(r // tm,),
        in_specs=[row(D_MODEL), row(SB_WIDTH), row(SA_WIDTH), row(2 * D_MODEL),
                  res((SB_WIDTH, D_MODEL)), res((SA_WIDTH, D_MODEL)), res((D_MODEL, D_MODEL)),
                  res((1, D_MODEL)), res((D_MODEL, D_FF)), res((D_FF, D_MODEL))],
        out_specs=row(D_MODEL),
        out_shape=jax.ShapeDtypeStruct((r, D_MODEL), F32),
        compiler_params=_cparams(("parallel",)),
        name="tail",
    )(x2d, o_sb, o_sa, gates, w_sb_out, w_sa_out, w_o, g2.reshape(1, D_MODEL), w_up, w_down)


def _unpermute_heads(a):
    shp = a.shape
    a = a.reshape(shp[:-1] + (SA_HEADS, HEAD_DIM))
    return a[..., jnp.array(SA_Q_PERM), :].reshape(shp)


def kernel(x_prompt, x_sample, cache_sb_k, cache_sb_v, cache_sa_k, cache_sa_v, cache_idx_k, page_table,
           norm1_g, w_in, b_gate, q_norm_g, k_norm_g, idx_k_norm_g, w_sb_out, w_sa_out, w_o,
           norm2_g, w_up, w_down):
    depth = w_in.shape[0]
    batch, seq, _ = x_prompt.shape
    dec_batch, dec_seq, _ = x_sample.shape
    n_phys = cache_sb_k.shape[1]
    past_len = page_table.shape[1] * PAGE_SIZE
    assert seq % 512 == 0 and cache_sb_k.shape[2] == PAGE_SIZE
    pos_p = jnp.arange(seq, dtype=F32)
    pos_s = past_len + jnp.arange(dec_seq, dtype=F32)
    tm_p = 512
    tm_s = 256 if (dec_batch * dec_seq) % 256 == 0 else dec_batch * dec_seq

    xp = x_prompt.reshape(batch * seq, D_MODEL)
    xs = x_sample.reshape(dec_batch * dec_seq, D_MODEL)
    new_p = [[] for _ in range(5)]
    new_s = [[] for _ in range(5)]
    for layer in range(depth):
        w_packed = _pack_w_in(w_in[layer])
        w_sa_perm = _unpermute_heads(w_sa_out[layer].T).T.astype(BF16)
        tail_w = (w_sb_out[layer].astype(BF16), w_sa_perm, w_o[layer].astype(BF16), norm2_g[layer],
                  w_up[layer].astype(BF16), w_down[layer].astype(BF16))
        norm_args = (norm1_g[layer], q_norm_g[layer], k_norm_g[layer], idx_k_norm_g[layer], b_gate[layer])

        (*new_kv, avtb, iwt, sbq, sbk, sbv, saq, sak, sav, iq, ik2, gates) = _project(
            xp, pos_p, batch, w_packed, *norm_args, tm=tm_p, group_cols=seq)
        o_sb = _sb_prompt(sbq, sbk, sbv, batch, seq, tq=256)
        o_sa = _dsa_prompt(iq, iwt, saq, ik2, sak, avtb, batch, seq, tq=128, tk=512, nsub=2)
        xp = _tail(xp, o_sb, o_sa, gates, *tail_w, tm=tm_p)
        for lst, arr in zip(new_p, new_kv):
            arr = jnp.moveaxis(arr.reshape(batch, -1, min(arr.shape[1], HEAD_DIM), seq), 3, 1)
            lst.append(arr if arr.shape[2] > 1 else arr[:, :, 0])

        n_s = dec_batch * dec_seq
        (*new_kv, avtb, iwt, sbq, sbk, sbv, saq, sak, sav, iq, ik2, gates) = _project(
            xs, pos_s, dec_batch, w_packed, *norm_args, tm=tm_s, group_cols=n_s)
        b3 = lambda a: a.reshape(dec_batch, dec_seq, a.shape[-1])
        iw = iwt[0].T
        o_sb = _sb_sample(b3(sbq), b3(sbk), b3(sbv),
                          _pages_t(cache_sb_k[layer]), _pages_t(cache_sb_v[layer]), page_table)
        o_sa = _dsa_sample(_unpermute_heads(b3(saq)), b3(sak), b3(sav), b3(iq), b3(iw),
                           b3(ik2)[..., :IDX_DIM],
                           _pages_t(cache_sa_k[layer]), _pages_t(cache_sa_v[layer]),
                           _pages_t(cache_idx_k[layer]), page_table)
        o_sb = o_sb.reshape(dec_batch * dec_seq, SB_WIDTH).astype(BF16)
        o_sa = _unpermute_heads(o_sa).reshape(dec_batch * dec_seq, SA_WIDTH).astype(BF16)
        xs = _tail(xs, o_sb, o_sa, gates, *tail_w, tm=tm_s)
        for lst, arr in zip(new_s, new_kv):
            arr = jnp.moveaxis(arr.reshape(-1, min(arr.shape[1], HEAD_DIM), dec_batch, dec_seq), (2, 3), (0, 1))
            lst.append(arr if arr.shape[2] > 1 else arr[:, :, 0])

    return (xp.reshape(batch, seq, D_MODEL), xs.reshape(dec_batch, dec_seq, D_MODEL),
            *(jnp.stack(l) for l in new_p), *(jnp.stack(l) for l in new_s))
```

```python
import functools

import jax
import jax.numpy as jnp
from jax import lax
from jax.experimental import pallas as pl
from jax.experimental.pallas import tpu as pltpu

F32 = jnp.float32
BF16 = jnp.bfloat16
I32 = jnp.int32

D_MODEL = 1024
HEAD_DIM = 64
SB_HEADS = 8
SA_HEADS = 8
SA_KV_HEADS = 4
IDX_HEADS = 8
IDX_DIM = 64
SB_WIDTH = SB_HEADS * HEAD_DIM
SA_WIDTH = SA_HEADS * HEAD_DIM
SA_KV_WIDTH = SA_KV_HEADS * HEAD_DIM
IDX_WIDTH = IDX_HEADS * IDX_DIM
TOP_K_MAX = 256
ROT_DIM = HEAD_DIM // 4
ROPE_THETA = 500000.0
D_FF = 4 * D_MODEL
PAGE_SIZE = 128
RMS_EPS = 1e-6
QK_SCALE = HEAD_DIM ** -0.5
SA_Q_SCALE = QK_SCALE * 1.4426950408889634
IDX_SCALE = IDX_DIM ** -0.5
IDX_W_SCALE = IDX_HEADS ** -0.5

LANES = 128
VMEM_LIMIT = 56 * 1024 * 1024

C_SQ, C_SK, C_SV = 0, 512, 1024
C_AQ, C_AK, C_AV = 1536, 2048, 2304
C_IQ, C_IK2, C_IW = 2560, 3072, 3200
C_GATE = 3328
C_END = C_GATE + 2 * D_MODEL
SA_Q_PERM = (0, 2, 1, 3, 4, 6, 5, 7)

SURV_FLOOR = -105.0
NEG_BIG = -1e30
INT_MIN = -2 ** 31

SB_SAMPLE_HEAD_PAGES = 4
DSA_SEL_GROUP = 32
DSA_SAMPLE_GROUP = 64


TOPK_GROUPS = 256


def _largest_divisor(n, cap):
    return max(d for d in range(1, min(n, cap) + 1) if n % d == 0)


def _cparams(sem, vmem=VMEM_LIMIT):
    return pltpu.CompilerParams(dimension_semantics=sem, vmem_limit_bytes=vmem)


def _rope_chunk(y, cos, sin_a, sin_b):
    up = pltpu.roll(y, LANES - ROT_DIM // 2, 1)
    dn = pltpu.roll(y, ROT_DIM // 2, 1)
    return y * cos + up * sin_a + dn * sin_b


def _head_rms_chunk(y, e_mat, gain):
    y2 = y * y
    hi = y2.astype(BF16)
    lo = (y2 - hi.astype(F32)).astype(BF16)
    ms = (jnp.dot(hi, e_mat, preferred_element_type=F32)
          + jnp.dot(lo, e_mat, preferred_element_type=F32))
    return y * lax.rsqrt(ms + RMS_EPS) * gain


def _proj_kernel(x_ref, g1_ref, w_ref, qg_ref, kg_ref, ig_ref, bg_ref, cos_ref, sa_ref, sb_ref,
                 e_ref,
                 skt_ref, svt_ref, akt_ref, avt_ref, ikt_ref, avtb_ref, iwt_ref,
                 sbq_ref, sbk_ref, sbv_ref, saq_ref, sak_ref, sav_ref, iq_ref, ik2_ref, gate_ref):
    x = x_ref[...]
    ms = jnp.mean(x * x, axis=-1, keepdims=True)
    h = (x * lax.rsqrt(ms + RMS_EPS) * g1_ref[...]).astype(BF16)

    def mm(lo, hi):
        return jnp.dot(h, w_ref[:, lo:hi], preferred_element_type=F32)

    cos, sin_a, sin_b = cos_ref[...], sa_ref[...], sb_ref[...]
    e_mat = e_ref[...]

    p = mm(C_SQ, C_SK)
    sbq_ref[...] = (p * QK_SCALE).astype(BF16)
    p = mm(C_SK, C_SV)
    skt_ref[0] = p.T
    sbk_ref[...] = p.astype(BF16)
    p = mm(C_SV, C_AQ)
    svt_ref[0] = p.T
    sbv_ref[...] = p.astype(BF16)

    p = mm(C_AQ, C_AK)
    for c in range(SA_WIDTH // LANES):
        sl = slice(c * LANES, (c + 1) * LANES)
        y = _rope_chunk(_head_rms_chunk(p[:, sl], e_mat, qg_ref[...]), cos, sin_a, sin_b)
        saq_ref[:, sl] = (y * SA_Q_SCALE).astype(BF16)
    p = mm(C_AK, C_AV)
    for c in range(SA_KV_WIDTH // LANES):
        sl = slice(c * LANES, (c + 1) * LANES)
        y = _rope_chunk(_head_rms_chunk(p[:, sl], e_mat, kg_ref[...]), cos, sin_a, sin_b)
        akt_ref[0, sl, :] = y.T
        sak_ref[:, sl] = y.astype(BF16)
    p = mm(C_AV, C_IQ)
    pt = p.T
    avt_ref[0] = pt
    avtb_ref[0] = pt.astype(BF16)
    sav_ref[...] = p.astype(BF16)

    p = mm(C_IQ, C_IK2)
    for c in range(IDX_WIDTH // LANES):
        sl = slice(c * LANES, (c + 1) * LANES)
        iq_ref[:, sl] = (_rope_chunk(p[:, sl], cos, sin_a, sin_b) * IDX_SCALE).astype(BF16)
    p = mm(C_IK2, C_GATE)
    y = _rope_chunk(_head_rms_chunk(p[:, :LANES], e_mat, ig_ref[...]), cos, sin_a, sin_b)
    ikt_ref[0] = y.T[:IDX_DIM]
    ik2_ref[...] = y.astype(BF16)
    iwt_ref[0] = p[:, LANES:].T[:IDX_HEADS] * IDX_W_SCALE

    p = mm(C_GATE, C_END) + bg_ref[...]
    gate_ref[...] = jax.nn.sigmoid(p)


def _rope_tables(pos):
    half = ROT_DIM // 2
    inv = 1.0 / (ROPE_THETA ** (jnp.arange(0, ROT_DIM, 2, dtype=F32) / ROT_DIM))
    ang = pos[:, None] * inv[None, :]
    cos, sin = jnp.cos(ang), jnp.sin(ang)
    t = pos.shape[0]
    zeros = jnp.zeros((t, HEAD_DIM - ROT_DIM), F32)
    zh = jnp.zeros((t, half), F32)
    cos_h = jnp.concatenate([cos, cos, zeros + 1.0], axis=1)
    sa_h = jnp.concatenate([-sin, zh, zeros], axis=1)
    sb_h = jnp.concatenate([zh, sin, zeros], axis=1)
    return tuple(jnp.concatenate([a, a], axis=1) for a in (cos_h, sa_h, sb_h))


def _pack_w_in(w_in):
    o = 0
    parts = []
    for n in (SB_WIDTH, SB_WIDTH, SB_WIDTH, SA_WIDTH, SA_KV_WIDTH, SA_KV_WIDTH, IDX_WIDTH,
              IDX_DIM, IDX_HEADS, 2 * D_MODEL):
        parts.append(w_in[:, o:o + n])
        o += n
    sq, sk, sv, aq, ak, av, iq, ik, iw, gates = parts
    aq = aq.reshape(D_MODEL, SA_HEADS, HEAD_DIM)[:, jnp.array(SA_Q_PERM)].reshape(D_MODEL, SA_WIDTH)
    pad = jnp.zeros((D_MODEL, C_GATE - C_IW - IDX_HEADS), w_in.dtype)
    return jnp.concatenate([sq, sk, sv, aq, ak, av, iq, ik, ik, iw, pad, gates], axis=1).astype(BF16)


def _project(x2d, pos, n_rep, w_packed, g1, qg, kg, ig, bg, tm, group_cols):
    r = x2d.shape[0]
    t = pos.shape[0]
    assert r == n_rep * t and r % tm == 0 and (t % tm == 0 or tm % t == 0)
    assert r % group_cols == 0 and group_cols % tm == 0
    ntc = group_cols // tm
    cos, sin_a, sin_b = _rope_tables(pos)
    if tm > t:
        cos, sin_a, sin_b = (jnp.tile(a, (tm // t, 1)) for a in (cos, sin_a, sin_b))
    nt = cos.shape[0] // tm
    e_mat = (jnp.arange(LANES)[:, None] // HEAD_DIM == jnp.arange(LANES)[None, :] // HEAD_DIM)
    e_mat = (e_mat.astype(F32) / HEAD_DIM).astype(BF16)

    def row(width):
        return pl.BlockSpec((tm, width), lambda i: (i, 0))

    def const(shape):
        return pl.BlockSpec(shape, lambda i: (0, 0))

    tab = pl.BlockSpec((tm, LANES), lambda i: (i % nt, 0))
    tile2 = lambda g: jnp.concatenate([g, g]).reshape(1, LANES)
    out_heights = [(SB_WIDTH, F32), (SB_WIDTH, F32), (SA_KV_WIDTH, F32), (SA_KV_WIDTH, F32),
                   (IDX_DIM, F32), (SA_KV_WIDTH, BF16), (IDX_HEADS, F32)]
    out_widths = [(SB_WIDTH, BF16), (SB_WIDTH, BF16), (SB_WIDTH, BF16),
                  (SA_WIDTH, BF16), (SA_KV_WIDTH, BF16), (SA_KV_WIDTH, BF16),
                  (IDX_WIDTH, BF16), (LANES, BF16), (2 * D_MODEL, F32)]
    return pl.pallas_call(
        _proj_kernel,
        grid=(r // tm,),
        in_specs=[row(D_MODEL), const((1, D_MODEL)),
                  pl.BlockSpec((D_MODEL, C_END), lambda i: (0, 0), pipeline_mode=pl.Buffered(1)),
                  const((1, LANES)), const((1, LANES)), const((1, LANES)), const((1, 2 * D_MODEL)),
                  tab, tab, tab, const((LANES, LANES))],
        out_specs=[pl.BlockSpec((1, h_, tm), lambda i: (i // ntc, 0, i % ntc)) for h_, _ in out_heights]
                  + [row(w_) for w_, _ in out_widths],
        out_shape=[jax.ShapeDtypeStruct((r // group_cols, h_, group_cols), dt) for h_, dt in out_heights]
                  + [jax.ShapeDtypeStruct((r, w_), dt) for w_, dt in out_widths],
        compiler_params=_cparams(("parallel",)),
        name="proj",
    )(x2d, g1.reshape(1, D_MODEL), w_packed, tile2(qg), tile2(kg), tile2(ig),
      bg.reshape(1, 2 * D_MODEL), cos, sin_a, sin_b, e_mat)


def _stick_break_tile(z, mask, surv, tri):
    t = jnp.log1p(jnp.exp(-jnp.abs(z)))
    l = -(jnp.maximum(z, 0.0) + t)
    if mask is not None:
        l = jnp.where(mask, l, 0.0)
    ls = jnp.minimum(z, 0.0) - t
    hi = l.astype(BF16)
    lo = (l - hi.astype(F32)).astype(BF16)
    ex = jnp.dot(hi, tri, preferred_element_type=F32) + jnp.dot(lo, tri, preferred_element_type=F32)
    a = jnp.exp(ls + ex + surv)
    if mask is not None:
        a = jnp.where(mask, a, 0.0)
    return a, surv + jnp.sum(l, axis=1, keepdims=True)


def _newer_tri(n):
    j = lax.broadcasted_iota(I32, (n, n), 0)
    s = lax.broadcasted_iota(I32, (n, n), 1)
    return (j > s).astype(BF16)


def _sb_prompt_kernel(q_ref, k_ref, v_ref, o_ref, acc_ref, surv_ref, *, tq):
    i = pl.program_id(2)
    t0 = i * tq
    lane = lax.broadcasted_iota(I32, (tq, LANES), 1)
    row_t = t0 + lax.broadcasted_iota(I32, (tq, tq), 0)
    col0 = lax.broadcasted_iota(I32, (tq, tq), 1)
    tri = _newer_tri(tq)
    q = q_ref[...]
    heads = range(2)
    qm = [jnp.where((lane >= HEAD_DIM) == bool(h), q, jnp.zeros_like(q)) for h in heads]
    acc_ref[...] = jnp.zeros_like(acc_ref)
    surv_ref[...] = jnp.zeros_like(surv_ref)

    def body(carry):
        kt, _ = carry
        start = pl.multiple_of(kt * tq, tq)
        kb = k_ref[pl.ds(start, tq), :]
        vb = v_ref[pl.ds(start, tq), :]
        mask = (start + col0) < row_t
        z = [lax.dot_general(qm[h], kb, (((1,), (1,)), ((), ())), preferred_element_type=F32)
             for h in heads]
        l, ls, ex = [], [], []
        for h in heads:
            t = jnp.log1p(jnp.exp(-jnp.abs(z[h])))
            l.append(jnp.where(mask, -(jnp.maximum(z[h], 0.0) + t), 0.0))
            ls.append(jnp.minimum(z[h], 0.0) - t)
            hi = l[h].astype(BF16)
            lo = (l[h] - hi.astype(F32)).astype(BF16)
            ex.append(jnp.dot(hi, tri, preferred_element_type=F32)
                      + jnp.dot(lo, tri, preferred_element_type=F32))
        acc, surv = [], []
        for h in heads:
            a = jnp.where(mask, jnp.exp(ls[h] + ex[h] + surv_ref[h]), 0.0)
            acc.append(acc_ref[h] + jnp.dot(a.astype(BF16), vb, preferred_element_type=F32))
            surv.append(surv_ref[h] + jnp.sum(l[h], axis=1, keepdims=True))
        acc_ref[...] = jnp.stack(acc)
        surv_ref[...] = jnp.stack(surv)
        return kt - 1, jnp.maximum(jnp.max(surv[0]), jnp.max(surv[1]))

    lax.while_loop(lambda c: (c[0] >= 0) & (c[1] > SURV_FLOOR), body, (i, jnp.float32(0.0)))
    o_ref[...] = jnp.where(lane < HEAD_DIM, acc_ref[0], acc_ref[1]).astype(o_ref.dtype)


def _sb_prompt(q, k, v, batch, seq, tq):
    nq = seq // tq
    n_chunks = SB_WIDTH // LANES
    qspec = pl.BlockSpec((tq, LANES), lambda b, c, i: (b * nq + i, c))
    kvspec = pl.BlockSpec((seq, LANES), lambda b, c, i: (b, c))
    return pl.pallas_call(
        functools.partial(_sb_prompt_kernel, tq=tq),
        grid=(batch, n_chunks, nq),
        in_specs=[qspec, kvspec, kvspec],
        out_specs=qspec,
        out_shape=jax.ShapeDtypeStruct((batch * seq, SB_WIDTH), BF16),
        scratch_shapes=[pltpu.VMEM((2, tq, LANES), F32), pltpu.VMEM((2, tq, 1), F32)],
        compiler_params=_cparams(("parallel", "parallel", "arbitrary")),
        name="sb_prompt",
    )(q, k, v)


def _sortable(x):
    bits = lax.bitcast_convert_type(x, I32)
    key = jnp.where(bits < 0, bits ^ jnp.int32(0x7FFFFFFF), bits)
    return jnp.where(x == 0.0, jnp.int32(0), key)


def _select_topk(keys_ref, n_tiles, tk, k_row, n_index_bits, key_axis=1):
    qn = keys_ref.shape[1 - key_axis]
    sign = jnp.int32(INT_MIN)
    tile_shape = (qn, tk) if key_axis == 1 else (tk, qn)
    fold = LANES if key_axis == 1 else 8
    vec = (qn, 1) if key_axis == 1 else (1, qn)

    def tile_at(kt):
        start = pl.multiple_of(kt * tk, tk)
        idx = (slice(None), pl.ds(start, tk)) if key_axis == 1 else (pl.ds(start, tk), slice(None))
        return start, idx

    def count(pred):
        def tile_body(kt, acc):
            start, idx = tile_at(kt)
            m = pred(keys_ref[idx], start).astype(I32)
            if key_axis == 1:
                for c in range(tk // fold):
                    acc = acc + m[:, c * fold:(c + 1) * fold]
                return acc
            return acc + jnp.sum(m.reshape(tk // fold, fold, qn), axis=0)
        acc0 = jnp.zeros((qn, fold) if key_axis == 1 else (fold, qn), I32)
        acc = lax.fori_loop(0, n_tiles, tile_body, acc0)
        if key_axis == 0:
            for shift in (4, 2, 1):
                acc = acc + pltpu.roll(acc, shift, 0)
            return acc[:1]
        return jnp.sum(acc.astype(F32), axis=1, keepdims=True).astype(I32)

    bracket = None
    if key_axis == 0 and tk % TOPK_GROUPS == 0:
        def gmax_body(kt, acc):
            _, idx = tile_at(kt)
            return jnp.maximum(acc, jnp.max(keys_ref[idx].reshape(tk // TOPK_GROUPS, TOPK_GROUPS, qn), axis=0))
        gmax = lax.fori_loop(0, n_tiles, gmax_body, jnp.full((TOPK_GROUPS, qn), INT_MIN, I32))
        gmax = gmax.reshape(TOPK_GROUPS // fold, fold, qn)
        hi_b, lo_b = jnp.max(gmax, axis=0), jnp.min(gmax, axis=0)
        for shift in (4, 2, 1):
            hi_b = jnp.maximum(hi_b, pltpu.roll(hi_b, shift, 0))
            lo_b = jnp.minimum(lo_b, pltpu.roll(lo_b, shift, 0))
        bracket = (lo_b[:1], hi_b[:1])

    zeros = jnp.zeros(vec, I32)
    b_first, base, span = jnp.int32(31), zeros, zeros - 1
    if bracket is not None:
        base = bracket[0] ^ sign
        span = (bracket[1] ^ sign) - base
        expo = lax.shift_right_logical(lax.bitcast_convert_type(jnp.maximum(span, 1).astype(F32), I32),
                                       jnp.int32(23)) - 127
        top = jnp.where(span < 0, 31, expo)
        b_first = jnp.minimum(jnp.max(top.astype(F32)).astype(I32), 31)

    def one_bit(b, prefix, thr, done):
        bit = jnp.where(b >= 0, lax.shift_left(jnp.int32(1), jnp.maximum(b, 0)), 0)
        cand_off = prefix | bit
        in_span = (cand_off ^ sign) <= (span ^ sign)
        cand = (base + cand_off) ^ sign
        cnt = count(lambda kk, start: kk >= cand)
        live = (done == 0) & in_span
        prefix = jnp.where(live & (cnt >= k_row), cand_off, prefix)
        hit = live & (cnt == k_row)
        return prefix, jnp.where(hit, cand, thr), jnp.where(hit, 1, done)

    bits_per_test = 2 if key_axis == 0 else 4

    def bit_body(carry):
        b, prefix, thr, done, _ = carry
        for u in range(bits_per_test):
            prefix, thr, done = one_bit(b - u, prefix, thr, done)
        return b - bits_per_test, prefix, thr, done, jnp.min(done.astype(F32)).astype(I32)

    _, prefix, thr, done, all_done = lax.while_loop(
        lambda c: (c[0] >= 0) & (c[4] == 0), bit_body,
        (b_first, zeros, zeros, zeros, jnp.int32(0)))
    thr = jnp.where(done == 0, (base + prefix) ^ sign, thr)

    if key_axis == 0:
        @pl.when(all_done == 0)
        def _():
            tied = done == 0
            want = (k_row - count(lambda kk, start: kk > thr)).astype(F32)
            half = tk // 2
            r = lax.broadcasted_iota(I32, (half, half), 0)
            c = lax.broadcasted_iota(I32, (half, half), 1)
            tri = (c <= r).astype(BF16)

            def fix_body(kt, seen):
                start, _ = tile_at(kt)
                for u in range(2):
                    idx = (pl.ds(pl.multiple_of(start + u * half, half), half), slice(None))
                    kk = keys_ref[idx]
                    tie = tied & (kk == thr)
                    rank = seen + jnp.dot(tri, tie.astype(BF16), preferred_element_type=F32)
                    keys_ref[idx] = jnp.where(tie & (rank > want), thr - 1, kk)
                    seen = rank[half - 1:half]
                return seen
            lax.fori_loop(0, n_tiles, fix_body, jnp.zeros(vec, F32))
        return thr

    @pl.when(all_done == 0)
    def _():
        tied = done == 0
        want = k_row - count(lambda kk, start: kk > thr)
        col0 = lax.broadcasted_iota(I32, tile_shape, key_axis)

        def ties_before(x):
            return count(lambda kk, start: (kk == thr) & ((start + col0) < x))

        def idx_body(j, x):
            cand = x | lax.shift_left(jnp.int32(1), n_index_bits - 1 - j)
            return jnp.where(ties_before(cand) < want, cand, x)

        x = lax.fori_loop(0, n_index_bits, idx_body, zeros)

        def fix_body(kt, c):
            start, idx = tile_at(kt)
            kk = keys_ref[idx]
            drop = tied & (kk == thr) & ((start + col0) > x)
            keys_ref[idx] = jnp.where(drop, thr - 1, kk)
            return c
        lax.fori_loop(0, n_tiles, fix_body, 0)

    return thr


def _dsa_prompt_kernel(iq_ref, iwt_ref, aq_ref, ik_ref, ak_ref, avt_ref, o_ref,
                       keys_ref, iqt_ref, aqt_ref, m_ref, l_ref, acc_ref,
                       *, tq, tk, nsub, k_top, n_index_bits):
    assert tq == LANES
    i = pl.program_id(1)
    t0 = i * tq
    n_super = (t0 + tq + nsub * tk - 1) // (nsub * tk)
    n_tiles = nsub * n_super
    dim_row = lax.broadcasted_iota(I32, (LANES, tq), 0)
    halfmask = [(dim_row >= HEAD_DIM) == bool(e) for e in range(2)]
    t_query = t0 + lax.broadcasted_iota(I32, (1, tq), 1)

    def chunk_t(ref, c):
        return ref[:, c * LANES:(c + 1) * LANES].astype(F32).T

    for c in range(IDX_WIDTH // LANES):
        qt = chunk_t(iq_ref, c)
        for e in range(2):
            iqt_ref[c, :, e * tq:(e + 1) * tq] = jnp.where(halfmask[e], qt, 0.0).astype(BF16)

    def score_body(ks, c):
        starts = [pl.multiple_of((ks * nsub + u) * tk, tk) for u in range(nsub)]
        dots = [[jnp.dot(ik_ref[pl.ds(st, tk), :], iqt_ref[c2], preferred_element_type=F32)
                 for c2 in range(IDX_WIDTH // LANES)] for st in starts]
        for u, st in enumerate(starts):
            score = jnp.zeros((tk, tq), F32)
            for c2 in range(IDX_WIDTH // LANES):
                for e in range(2):
                    h = 2 * c2 + e
                    s = dots[u][c2][:, e * tq:(e + 1) * tq]
                    score = score + jnp.maximum(s, 0.0) * iwt_ref[0, h:h + 1, :]
            pos = st + lax.broadcasted_iota(I32, (tk, tq), 0)
            keys_ref[pl.ds(st, tk), :] = jnp.where(pos <= t_query, _sortable(score), jnp.int32(INT_MIN))
        return c
    lax.fori_loop(0, n_super, score_body, 0)

    k_row = jnp.minimum(t_query + 1, k_top)
    thr = _select_topk(keys_ref, (t0 + tq + tk - 1) // tk, tk, k_row, n_index_bits, key_axis=0)

    for c in range(SA_WIDTH // LANES):
        qt = chunk_t(aq_ref, c)
        p, j = c // 2, c % 2
        for e in range(2):
            aqt_ref[2 * p + e, :, j * tq:(j + 1) * tq] = jnp.where(halfmask[e], qt, 0.0).astype(BF16)
    m_ref[...] = jnp.full_like(m_ref, NEG_BIG)
    l_ref[...] = jnp.zeros_like(l_ref)
    acc_ref[...] = jnp.zeros_like(acc_ref)

    ones_rows = jnp.ones((16, tk), BF16)

    def att_body(ks, c):
        starts = [pl.multiple_of((ks * nsub + u) * tk, tk) for u in range(nsub)]
        logits = [[jnp.dot(ak_ref[pl.ds(st, tk), (g // 2) * LANES:(g // 2 + 1) * LANES], aqt_ref[g],
                           preferred_element_type=F32) for g in range(SA_KV_HEADS)]
                  for st in starts]
        m_all, l_all = m_ref[...], l_ref[...]
        m_cur = [m_all[hh:hh + 1] for hh in range(SA_HEADS)]
        l_cur = [l_all[hh:hh + 1] for hh in range(SA_HEADS)]
        acc_cur = [acc_ref[hh * HEAD_DIM:(hh + 1) * HEAD_DIM] for hh in range(SA_HEADS)]
        for u, st in enumerate(starts):
            sel = keys_ref[pl.ds(st, tk), :] >= thr
            probs, alphas = [], []
            for hh in range(SA_HEADS):
                g, j = hh // 2, hh % 2
                s = jnp.where(sel, logits[u][g][:, j * tq:(j + 1) * tq], NEG_BIG)
                m_new = jnp.maximum(m_cur[hh], jnp.max(s, axis=0, keepdims=True))
                alpha = jnp.exp2(m_cur[hh] - m_new)
                m_cur[hh] = m_new
                probs.append(jnp.exp2(s - m_new).astype(BF16))
                alphas.append(alpha)
            for hh in range(SA_HEADS):
                g = hh // 2
                vt = avt_ref[0, g * HEAD_DIM:(g + 1) * HEAD_DIM, pl.ds(st, tk)]
                pv = jnp.dot(jnp.concatenate([vt, ones_rows], axis=0), probs[hh],
                             preferred_element_type=F32)
                acc_cur[hh] = alphas[hh] * acc_cur[hh] + pv[:HEAD_DIM]
                l_cur[hh] = alphas[hh] * l_cur[hh] + pv[HEAD_DIM:HEAD_DIM + 1]
        m_ref[...] = jnp.concatenate(m_cur, axis=0)
        l_ref[...] = jnp.concatenate(l_cur, axis=0)
        acc_ref[...] = jnp.concatenate(acc_cur, axis=0)
        return c
    lax.fori_loop(0, n_super, att_body, 0)

    def head_out(hh):
        return acc_ref[hh * HEAD_DIM:(hh + 1) * HEAD_DIM] / l_ref[hh:hh + 1]

    for p in range(SA_KV_WIDTH // LANES):
        for j in range(2):
            ot = jnp.concatenate([head_out(2 * (2 * p) + j), head_out(2 * (2 * p + 1) + j)], axis=0)
            o_ref[:, (2 * p + j) * LANES:(2 * p + j + 1) * LANES] = ot.T.astype(o_ref.dtype)


def _dsa_prompt(iq, iwt, aq, ik2, ak, avt, batch, seq, tq, tk, nsub):
    assert seq % (nsub * tk) == 0
    nq = seq // tq
    k_top = min(TOP_K_MAX, seq // 4)
    n_index_bits = max(1, seq.bit_length())
    qspec = lambda w: pl.BlockSpec((tq, w), lambda b, i: (b * nq + i, 0))
    kspec = lambda w: pl.BlockSpec((seq, w), lambda b, i: (b, 0), pipeline_mode=pl.Buffered(1))
    return pl.pallas_call(
        functools.partial(_dsa_prompt_kernel, tq=tq, tk=tk, nsub=nsub, k_top=k_top,
                          n_index_bits=n_index_bits),
        grid=(batch, nq),
        in_specs=[qspec(IDX_WIDTH),
                  pl.BlockSpec((1, IDX_HEADS, tq), lambda b, i: (b, 0, i)),
                  qspec(SA_WIDTH), kspec(LANES), kspec(SA_KV_WIDTH),
                  pl.BlockSpec((1, SA_KV_WIDTH, seq), lambda b, i: (b, 0, 0),
                               pipeline_mode=pl.Buffered(1))],
        out_specs=qspec(SA_WIDTH),
        out_shape=jax.ShapeDtypeStruct((batch * seq, SA_WIDTH), BF16),
        scratch_shapes=[pltpu.VMEM((seq, tq), I32),
                        pltpu.VMEM((IDX_HEADS // 2, LANES, 2 * tq), BF16),
                        pltpu.VMEM((SA_KV_HEADS, LANES, 2 * tq), BF16),
                        pltpu.VMEM((SA_HEADS, tq), F32), pltpu.VMEM((SA_HEADS, tq), F32),
                        pltpu.VMEM((SA_HEADS * HEAD_DIM, tq), F32)],
        compiler_params=_cparams(("parallel", "arbitrary")),
        name="dsa_prompt",
    )(iq, iwt, aq, ik2, ak, avt)


def _sb_page_step(q, kc_ref, vc_ref, o_ref, surv_ref, tri):
    @pl.when(jnp.max(surv_ref[0]) > SURV_FLOOR)
    def _():
        kt = kc_ref[0].astype(BF16)
        vt = vc_ref[0].astype(BF16)
        z = jnp.dot(q, kt, preferred_element_type=F32)
        a, surv = _stick_break_tile(z, None, surv_ref[0][:, :1], tri)
        o_ref[0] += lax.dot_general(a.astype(BF16), vt, (((1,), (1,)), ((), ())),
                                    preferred_element_type=F32)
        surv_ref[0] = jnp.broadcast_to(surv, surv_ref.shape[1:])


def _sb_sample_head_kernel(pt_ref, q_ref, kn_ref, vn_ref, *rest, n_tok, n_head_pages):
    pages, (o_ref, surv_ref) = rest[:2 * n_head_pages], rest[2 * n_head_pages:]
    rows = q_ref.shape[1]
    q = q_ref[0]
    tri = _newer_tri(PAGE_SIZE)
    z = lax.dot_general(q, kn_ref[0], (((1,), (1,)), ((), ())), preferred_element_type=F32)
    t = lax.broadcasted_iota(I32, (rows, PAGE_SIZE), 0) % n_tok
    mask = lax.broadcasted_iota(I32, (rows, PAGE_SIZE), 1) < t
    a, surv = _stick_break_tile(z, mask, jnp.zeros((rows, 1), F32), tri)
    o_ref[0] = jnp.dot(a.astype(BF16), vn_ref[0], preferred_element_type=F32)
    surv_ref[0] = jnp.broadcast_to(surv, surv_ref.shape[1:])
    for i in range(n_head_pages):
        _sb_page_step(q, pages[2 * i], pages[2 * i + 1], o_ref, surv_ref, tri)


def _sb_sample_tail_kernel(pt_ref, q_ref, acc_ref, sin_ref, kc_ref, vc_ref, o_ref, surv_ref):
    @pl.when(pl.program_id(1) == 0)
    def _():
        o_ref[...] = acc_ref[...]
        surv_ref[...] = sin_ref[...]
    _sb_page_step(q_ref[0], kc_ref, vc_ref, o_ref, surv_ref, _newer_tri(PAGE_SIZE))


def _block_diag_queries(q, n_heads_q, n_heads_kv):
    group = n_heads_q // n_heads_kv
    blocks = []
    for h in range(n_heads_q):
        kv = h // group
        qh = q[:, :, h * HEAD_DIM:(h + 1) * HEAD_DIM]
        blocks.append(jnp.pad(qh, ((0, 0), (0, 0), (kv * HEAD_DIM, (n_heads_kv - 1 - kv) * HEAD_DIM))))
    return jnp.concatenate(blocks, axis=1)


def _take_block_diag(acc, n_tok, n_heads_q, n_heads_kv):
    group = n_heads_q // n_heads_kv
    return jnp.concatenate(
        [acc[:, h * n_tok:(h + 1) * n_tok, (h // group) * HEAD_DIM:(h // group + 1) * HEAD_DIM]
         for h in range(n_heads_q)], axis=2)


def _pages_t(cache):
    n_phys, page = cache.shape[:2]
    return jnp.moveaxis(cache, 1, -1).reshape(n_phys, -1, page)


def _pad_rows(a, rows):
    return jnp.pad(a, ((0, 0), (0, rows - a.shape[1]), (0, 0)))


def _sb_sample(q, k, v, cache_k, cache_v, page_table):
    b, n_tok, _ = q.shape
    n_pages = page_table.shape[1]
    rows = SB_HEADS * n_tok
    qbd = _block_diag_queries(q, SB_HEADS, SB_HEADS)
    kn, vn = _pad_rows(k, PAGE_SIZE), _pad_rows(v, PAGE_SIZE)
    n_head = min(SB_SAMPLE_HEAD_PAGES, n_pages)
    out_shape = [jax.ShapeDtypeStruct((b, rows, SB_WIDTH), F32),
                 jax.ShapeDtypeStruct((b, rows, LANES), F32)]

    per_b1 = lambda bi, pt: (bi, 0, 0)
    page_specs = []
    for i in range(n_head):
        spec = pl.BlockSpec((1, SB_WIDTH, PAGE_SIZE), lambda bi, pt, i=i: (pt[bi, n_pages - 1 - i], 0, 0))
        page_specs += [spec, spec]
    acc, surv = pl.pallas_call(
        functools.partial(_sb_sample_head_kernel, n_tok=n_tok, n_head_pages=n_head),
        grid_spec=pltpu.PrefetchScalarGridSpec(
            num_scalar_prefetch=1,
            grid=(b,),
            in_specs=[pl.BlockSpec((1, rows, SB_WIDTH), per_b1),
                      pl.BlockSpec((1, PAGE_SIZE, SB_WIDTH), per_b1),
                      pl.BlockSpec((1, PAGE_SIZE, SB_WIDTH), per_b1)] + page_specs,
            out_specs=[pl.BlockSpec((1, rows, SB_WIDTH), per_b1), pl.BlockSpec((1, rows, LANES), per_b1)]),
        out_shape=out_shape,
        compiler_params=_cparams(("parallel",)),
        name="sb_sample_head",
    )(page_table, qbd, kn, vn, *([cache_k, cache_v] * n_head))

    n_tail = n_pages - n_head
    if n_tail > 0:
        per_b = lambda bi, j, pt: (bi, 0, 0)
        page = lambda bi, j, pt: (pt[bi, n_tail - 1 - j], 0, 0)

        def older_pages(acc, surv):
            return pl.pallas_call(
                _sb_sample_tail_kernel,
                grid_spec=pltpu.PrefetchScalarGridSpec(
                    num_scalar_prefetch=1,
                    grid=(b, n_tail),
                    in_specs=[pl.BlockSpec((1, rows, SB_WIDTH), per_b),
                              pl.BlockSpec((1, rows, SB_WIDTH), per_b),
                              pl.BlockSpec((1, rows, LANES), per_b),
                              pl.BlockSpec((1, SB_WIDTH, PAGE_SIZE), page),
                              pl.BlockSpec((1, SB_WIDTH, PAGE_SIZE), page)],
                    out_specs=[pl.BlockSpec((1, rows, SB_WIDTH), per_b),
                               pl.BlockSpec((1, rows, LANES), per_b)]),
                out_shape=out_shape,
                compiler_params=_cparams(("parallel", "arbitrary")),
                name="sb_sample_tail",
            )(page_table, qbd, acc, surv, cache_k, cache_v)[0]

        acc = lax.cond(jnp.max(surv) > SURV_FLOOR, older_pages, lambda acc, surv: acc, acc, surv)
    return _take_block_diag(acc, n_tok, SB_HEADS, SB_HEADS)


def _dsa_sel_kernel(pt_ref, iq_ref, iw_ref, ikn_ref, *rest, n_tok, n_pages, group, k_top, n_index_bits):
    pages, (sel_ref, keys_ref) = rest[:group], rest[group:]
    j = pl.program_id(1)
    width = group * PAGE_SIZE

    def scores(s):
        s = jnp.maximum(s, 0.0) * iw_ref[0]
        return jnp.sum(s.reshape(IDX_HEADS, n_tok, s.shape[-1]), axis=0)

    kt = jnp.concatenate([p_ref[0].astype(BF16) for p_ref in pages], axis=1)
    key = _sortable(scores(jnp.dot(iq_ref[0], kt, preferred_element_type=F32)))
    keys_ref[:, pl.ds(pl.multiple_of(j * width, width), width)] = key

    @pl.when(j == n_pages // group - 1)
    def _():
        t = lax.broadcasted_iota(I32, (n_tok, PAGE_SIZE), 0)
        col = lax.broadcasted_iota(I32, (n_tok, PAGE_SIZE), 1)
        s_new = lax.dot_general(iq_ref[0], ikn_ref[0], (((1,), (1,)), ((), ())),
                                preferred_element_type=F32)
        key_new = jnp.where(col <= t, _sortable(scores(s_new)), jnp.int32(INT_MIN))
        keys_ref[:, n_pages * PAGE_SIZE:] = key_new
        t_abs = n_pages * PAGE_SIZE + lax.broadcasted_iota(I32, (n_tok, 1), 0)
        k_row = jnp.minimum(t_abs + 1, k_top)
        thr = _select_topk(keys_ref, 1, keys_ref.shape[1], k_row, n_index_bits)
        sel_ref[0] = (keys_ref[...] >= thr).astype(F32)


def _dsa_sample_kernel(pt_ref, q_ref, sel_ref, seln_ref, kn_ref, vn_ref, *rest, n_tok, n_pages, group):
    k_pages, v_pages, (o_ref, m_ref, l_ref) = rest[:group], rest[group:2 * group], rest[2 * group:]
    j = pl.program_id(1)
    rows = SA_HEADS * n_tok
    nt_dims = (((1,), (1,)), ((), ()))

    @pl.when(j == 0)
    def _():
        m_ref[...] = jnp.full_like(m_ref, NEG_BIG)
        l_ref[...] = jnp.zeros_like(l_ref)
        o_ref[...] = jnp.zeros_like(o_ref)

    def step(s, sel, pv):
        n = s.shape[-1]
        sel = jnp.broadcast_to(sel[None], (SA_HEADS, n_tok, n)).reshape(rows, n)
        s = jnp.where(sel > 0.5, s, NEG_BIG)
        m_old = m_ref[...]
        m_new = jnp.maximum(m_old, jnp.max(s, axis=1, keepdims=True))
        alpha = jnp.exp2(m_old - m_new)
        pr = jnp.exp2(s - m_new)
        l_ref[...] = alpha * l_ref[...] + jnp.sum(pr, axis=1, keepdims=True)
        o_ref[0] = alpha * o_ref[0] + pv(pr.astype(BF16))
        m_ref[...] = m_new

    kt = jnp.concatenate([p_ref[0].astype(BF16) for p_ref in k_pages], axis=1)
    vt = jnp.concatenate([p_ref[0].astype(BF16) for p_ref in v_pages], axis=1)
    step(jnp.dot(q_ref[0], kt, preferred_element_type=F32), sel_ref[0],
         lambda pr: lax.dot_general(pr, vt, nt_dims, preferred_element_type=F32))

    @pl.when(j == n_pages // group - 1)
    def _():
        step(lax.dot_general(q_ref[0], kn_ref[0], nt_dims, preferred_element_type=F32), seln_ref[0],
             lambda pr: jnp.dot(pr, vn_ref[0], preferred_element_type=F32))
        o_ref[0] = o_ref[0] / l_ref[...]


def _dsa_sample(aq, ak, av, iq, iw, ik, cache_k, cache_v, cache_ik, page_table):
    b, n_tok, _ = aq.shape
    n_pages = page_table.shape[1]
    n_keys = (n_pages + 1) * PAGE_SIZE
    k_top = min(TOP_K_MAX, (n_pages * PAGE_SIZE + n_tok) // 4)
    rows = SA_HEADS * n_tok
    per_b = lambda bi, j, pt: (bi, 0, 0)

    def page_specs(width, group):
        return [pl.BlockSpec((1, width, PAGE_SIZE), lambda bi, j, pt, i=i: (pt[bi, j * group + i], 0, 0))
                for i in range(group)]

    g_sel = _largest_divisor(n_pages, DSA_SEL_GROUP)
    iq_rows = iq.reshape(b, n_tok, IDX_HEADS, IDX_DIM).transpose(0, 2, 1, 3).reshape(b, rows, IDX_DIM)
    iw_rows = iw.transpose(0, 2, 1).reshape(b, rows, 1)
    sel = pl.pallas_call(
        functools.partial(_dsa_sel_kernel, n_tok=n_tok, n_pages=n_pages, group=g_sel, k_top=k_top,
                          n_index_bits=max(1, n_keys.bit_length())),
        grid_spec=pltpu.PrefetchScalarGridSpec(
            num_scalar_prefetch=1,
            grid=(b, n_pages // g_sel),
            in_specs=[pl.BlockSpec((1, rows, IDX_DIM), per_b),
                      pl.BlockSpec((1, rows, 1), per_b),
                      pl.BlockSpec((1, PAGE_SIZE, IDX_DIM), per_b)] + page_specs(IDX_DIM, g_sel),
            out_specs=pl.BlockSpec((1, n_tok, n_keys), per_b),
            scratch_shapes=[pltpu.VMEM((n_tok, n_keys), I32)]),
        out_shape=jax.ShapeDtypeStruct((b, n_tok, n_keys), F32),
        compiler_params=_cparams(("parallel", "arbitrary")),
        name="dsa_sel",
    )(page_table, iq_rows, iw_rows, _pad_rows(ik, PAGE_SIZE), *([cache_ik] * g_sel))

    g_att = _largest_divisor(n_pages, DSA_SAMPLE_GROUP)
    qbd = _block_diag_queries(aq, SA_HEADS, SA_KV_HEADS)
    acc = pl.pallas_call(
        functools.partial(_dsa_sample_kernel, n_tok=n_tok, n_pages=n_pages, group=g_att),
        grid_spec=pltpu.PrefetchScalarGridSpec(
            num_scalar_prefetch=1,
            grid=(b, n_pages // g_att),
            in_specs=[pl.BlockSpec((1, rows, SA_KV_WIDTH), per_b),
                      pl.BlockSpec((1, n_tok, g_att * PAGE_SIZE), lambda bi, j, pt: (bi, 0, j)),
                      pl.BlockSpec((1, n_tok, PAGE_SIZE), lambda bi, j, pt: (bi, 0, n_pages)),
                      pl.BlockSpec((1, PAGE_SIZE, SA_KV_WIDTH), per_b),
                      pl.BlockSpec((1, PAGE_SIZE, SA_KV_WIDTH), per_b)]
                     + page_specs(SA_KV_WIDTH, g_att) + page_specs(SA_KV_WIDTH, g_att),
            out_specs=pl.BlockSpec((1, rows, SA_KV_WIDTH), per_b),
            scratch_shapes=[pltpu.VMEM((rows, 1), F32), pltpu.VMEM((rows, 1), F32)]),
        out_shape=jax.ShapeDtypeStruct((b, rows, SA_KV_WIDTH), F32),
        compiler_params=_cparams(("parallel", "arbitrary")),
        name="dsa_sample",
    )(page_table, qbd, sel, sel, _pad_rows(ak, PAGE_SIZE), _pad_rows(av, PAGE_SIZE),
      *([cache_k] * g_att), *([cache_v] * g_att))
    return _take_block_diag(acc, n_tok, SA_HEADS, SA_KV_HEADS)


def _tail_kernel(x_ref, osb_ref, osa_ref, gate_ref, wsb_ref, wsa_ref, wo_ref, g2_ref, wup_ref,
                 wdn_ref, y_ref, *, ff_chunk):
    gate = gate_ref[...]
    merged = (gate[:, :D_MODEL] * jnp.dot(osb_ref[...], wsb_ref[...], preferred_element_type=F32)
              + gate[:, D_MODEL:] * jnp.dot(osa_ref[...], wsa_ref[...], preferred_element_type=F32))
    x1 = x_ref[...] + jnp.dot(merged.astype(BF16), wo_ref[...], preferred_element_type=F32)
    ms = jnp.mean(x1 * x1, axis=-1, keepdims=True)
    h = (x1 * lax.rsqrt(ms + RMS_EPS) * g2_ref[...]).astype(BF16)
    y = x1
    n_chunks = D_FF // ff_chunk
    up = lambda c: jnp.dot(h, wup_ref[:, c * ff_chunk:(c + 1) * ff_chunk], preferred_element_type=F32)
    u_next = up(0)
    for c in range(n_chunks):
        u, u_next = u_next, (up(c + 1) if c + 1 < n_chunks else None)
        u = jnp.maximum(u, 0.0)
        y = y + jnp.dot((u * u).astype(BF16), wdn_ref[c * ff_chunk:(c + 1) * ff_chunk, :],
                        preferred_element_type=F32)
    y_ref[...] = y


def _tail(x2d, o_sb, o_sa, gates, w_sb_out, w_sa_out, w_o, g2, w_up, w_down, tm):
    r = x2d.shape[0]
    row = lambda w: pl.BlockSpec((tm, w), lambda i: (i, 0))
    res = lambda shape: pl.BlockSpec(shape, lambda i: (0, 0), pipeline_mode=pl.Buffered(1))
    return pl.pallas_call(
        functools.partial(_tail_kernel, ff_chunk=1024),
        grid=(r // tm,),
        in_specs=[row(D_MODEL), row(SB_WIDTH), row(SA_WIDTH), row(2 * D_MODEL),
                  res((SB_WIDTH, D_MODEL)), res((SA_WIDTH, D_MODEL)), res((D_MODEL, D_MODEL)),
                  res((1, D_MODEL)), res((D_MODEL, D_FF)), res((D_FF, D_MODEL))],
        out_specs=row(D_MODEL),
        out_shape=jax.ShapeDtypeStruct((r, D_MODEL), F32),
        compiler_params=_cparams(("parallel",)),
        name="tail",
    )(x2d, o_sb, o_sa, gates, w_sb_out, w_sa_out, w_o, g2.reshape(1, D_MODEL), w_up, w_down)


def _unpermute_heads(a):
    shp = a.shape
    a = a.reshape(shp[:-1] + (SA_HEADS, HEAD_DIM))
    return a[..., jnp.array(SA_Q_PERM), :].reshape(shp)


def kernel(x_prompt, x_sample, cache_sb_k, cache_sb_v, cache_sa_k, cache_sa_v, cache_idx_k, page_table,
           norm1_g, w_in, b_gate, q_norm_g, k_norm_g, idx_k_norm_g, w_sb_out, w_sa_out, w_o,
           norm2_g, w_up, w_down):
    depth = w_in.shape[0]
    batch, seq, _ = x_prompt.shape
    dec_batch, dec_seq, _ = x_sample.shape
    n_phys = cache_sb_k.shape[1]
    past_len = page_table.shape[1] * PAGE_SIZE
    assert seq % 512 == 0 and cache_sb_k.shape[2] == PAGE_SIZE
    pos_p = jnp.arange(seq, dtype=F32)
    pos_s = past_len + jnp.arange(dec_seq, dtype=F32)
    tm_p = 512
    tm_s = 256 if (dec_batch * dec_seq) % 256 == 0 else dec_batch * dec_seq

    xp = x_prompt.reshape(batch * seq, D_MODEL)
    xs = x_sample.reshape(dec_batch * dec_seq, D_MODEL)
    new_p = [[] for _ in range(5)]
    new_s = [[] for _ in range(5)]
    for layer in range(depth):
        w_packed = _pack_w_in(w_in[layer])
        w_sa_perm = _unpermute_heads(w_sa_out[layer].T).T.astype(BF16)
        tail_w = (w_sb_out[layer].astype(BF16), w_sa_perm, w_o[layer].astype(BF16), norm2_g[layer],
                  w_up[layer].astype(BF16), w_down[layer].astype(BF16))
        norm_args = (norm1_g[layer], q_norm_g[layer], k_norm_g[layer], idx_k_norm_g[layer], b_gate[layer])

        (*new_kv, avtb, iwt, sbq, sbk, sbv, saq, sak, sav, iq, ik2, gates) = _project(
            xp, pos_p, batch, w_packed, *norm_args, tm=tm_p, group_cols=seq)
        o_sb = _sb_prompt(sbq, sbk, sbv, batch, seq, tq=256)
        o_sa = _dsa_prompt(iq, iwt, saq, ik2, sak, avtb, batch, seq, tq=128, tk=512, nsub=2)
        xp = _tail(xp, o_sb, o_sa, gates, *tail_w, tm=tm_p)
        for lst, arr in zip(new_p, new_kv):
            arr = jnp.moveaxis(arr.reshape(batch, -1, min(arr.shape[1], HEAD_DIM), seq), 3, 1)
            lst.append(arr if arr.shape[2] > 1 else arr[:, :, 0])

        n_s = dec_batch * dec_seq
        (*new_kv, avtb, iwt, sbq, sbk, sbv, saq, sak, sav, iq, ik2, gates) = _project(
            xs, pos_s, dec_batch, w_packed, *norm_args, tm=tm_s, group_cols=n_s)
        b3 = lambda a: a.reshape(dec_batch, dec_seq, a.shape[-1])
        iw = iwt[0].T
        o_sb = _sb_sample(b3(sbq), b3(sbk), b3(sbv),
                          _pages_t(cache_sb_k[layer]), _pages_t(cache_sb_v[layer]), page_table)
        o_sa = _dsa_sample(_unpermute_heads(b3(saq)), b3(sak), b3(sav), b3(iq), b3(iw),
                           b3(ik2)[..., :IDX_DIM],
                           _pages_t(cache_sa_k[layer]), _pages_t(cache_sa_v[layer]),
                           _pages_t(cache_idx_k[layer]), page_table)
        o_sb = o_sb.reshape(dec_batch * dec_seq, SB_WIDTH).astype(BF16)
        o_sa = _unpermute_heads(o_sa).reshape(dec_batch * dec_seq, SA_WIDTH).astype(BF16)
        xs = _tail(xs, o_sb, o_sa, gates, *tail_w, tm=tm_s)
        for lst, arr in zip(new_s, new_kv):
            arr = jnp.moveaxis(arr.reshape(-1, min(arr.shape[1], HEAD_DIM), dec_batch, dec_seq), (2, 3), (0, 1))
            lst.append(arr if arr.shape[2] > 1 else arr[:, :, 0])

    return (xp.reshape(batch, seq, D_MODEL), xs.reshape(dec_batch, dec_seq, D_MODEL),
            *(jnp.stack(l) for l in new_p), *(jnp.stack(l) for l in new_s))
```
